```python
import math
import jax, jax.numpy as jnp
from jax import lax
import numpy as np

D_MODEL = 2048
BATCH = 4
SEQ = 2048
DEPTH = 4
DEC_BATCH = 8
DEC_SEQ = 4
PAST_LEN = 16384
PAGE_SIZE = 128

N_A = DEPTH // 2
N_B = DEPTH - N_A
H_A = 8
DK_A = D_MODEL // H_A
DV_A = 2 * DK_A
QK_A = H_A * DK_A
V_A = H_A * DV_A
RET_CHUNK = 128
H_B = 16
DH_B = D_MODEL // H_B
MOBA_BLOCK = 256
MOBA_TOPK = 3
Q_CHUNK = 16
ROPE_BASE = 10000.0
EPS = 1e-6

kernel_name = 'yoco_retention_moba_decoder_step'


def _rmsnorm(x, g):
    x32 = x.astype(jnp.float32)
    y = x32 * lax.rsqrt(jnp.mean(x32 * x32, axis=-1, keepdims=True) + EPS)
    return (y * g.astype(jnp.float32)).astype(x.dtype)


def _modulated_norm(x, g, shift, scale):
    return _rmsnorm(x, g) * (1 + scale[:, None, :]) + shift[:, None, :]


def _rotary(x, pos):
    half = x.shape[-1] // 2
    inv = 1.0 / (ROPE_BASE ** jnp.linspace(0.0, 1.0, half, dtype=jnp.float32))
    ang = pos.astype(jnp.float32)[:, None] * inv[None, :]
    cos = jnp.cos(ang)[None, :, None, :]
    sin = jnp.sin(ang)[None, :, None, :]
    x1, x2 = x[..., :half], x[..., half:]
    return jnp.concatenate([x1 * cos - x2 * sin, x1 * sin + x2 * cos], axis=-1)


def _log_gamma():
    return jnp.log1p(-(2.0 ** (-5.0 - jnp.arange(H_A, dtype=jnp.float32))))


def _retention_chunk(s, q, k, v, log_g):
    L = q.shape[1]
    i = jnp.arange(L, dtype=jnp.float32)
    diff = i[:, None] - i[None, :]
    decay = jnp.where(diff >= 0, jnp.exp(jnp.maximum(diff, 0.0)[None] * log_g[:, None, None]), 0.0)
    scores = jnp.einsum('blhd,bmhd->bhlm', q, k) * decay[None]
    inner = jnp.einsum('bhlm,bmhe->blhe', scores, v)
    cross = jnp.einsum('blhd,bhde->blhe', q, s) * jnp.exp((i + 1.0)[:, None] * log_g[None, :])[None, :, :, None]
    wk = jnp.exp((L - 1.0 - i)[:, None] * log_g[None, :])
    s_new = jnp.exp(L * log_g)[None, :, None, None] * s + jnp.einsum('blhd,blhe->bhde', k * wk[None, :, :, None], v)
    return s_new, inner + cross


def _retention(q, k, v, s0):
    B, T = q.shape[:2]
    C = math.gcd(T, RET_CHUNK)
    nc = T // C
    log_g = _log_gamma()

    def to_chunks(a):
        return a.reshape(B, nc, C, *a.shape[2:]).swapaxes(0, 1)

    def step(s, qkv):
        return _retention_chunk(s, qkv[0], qkv[1], qkv[2], log_g)

    s_new, o = lax.scan(step, s0, (to_chunks(q), to_chunks(k), to_chunks(v)))
    return o.swapaxes(0, 1).reshape(B, T, H_A, DV_A), s_new


def _retention_layer(h, pos, s0, w_in, w_out):
    B, T, _ = h.shape
    q, k, v, g = jnp.split(h @ w_in, [QK_A, 2 * QK_A, 2 * QK_A + V_A], axis=-1)
    q = _rotary(q.reshape(B, T, H_A, DK_A).astype(jnp.float32), pos)
    k = _rotary(k.reshape(B, T, H_A, DK_A).astype(jnp.float32), pos) * (DK_A ** -0.5)
    v = v.reshape(B, T, H_A, DV_A).astype(jnp.float32)
    o, s_new = _retention(q, k, v, s0.astype(jnp.float32))
    o = o * lax.rsqrt(jnp.mean(o * o, axis=-1, keepdims=True) + EPS)
    o = o.reshape(B, T, V_A).astype(h.dtype)
    return (jax.nn.silu(g) * o) @ w_out, s_new.astype(s0.dtype)


def _moba_attend(q, q_pos, k_full, v_full):
    B, T, H, DH = q.shape
    Lk = k_full.shape[1]
    nb = -(-Lk // MOBA_BLOCK)
    pad = nb * MOBA_BLOCK - Lk

    def to_blocks(a):
        a = jnp.pad(a, ((0, 0), (0, pad), (0, 0), (0, 0)))
        return a.reshape(B, nb, MOBA_BLOCK, H, DH).transpose(0, 3, 1, 2, 4)

    kb, vb = to_blocks(k_full), to_blocks(v_full)
    means = jnp.mean(kb.astype(jnp.float32), axis=3)
    n_sel = min(MOBA_TOPK, nb)
    qc_len = math.gcd(T, Q_CHUNK)
    nq = T // qc_len
    qc = q.reshape(B, nq, qc_len, H, DH).transpose(1, 0, 3, 2, 4)
    pc = q_pos.reshape(nq, qc_len)
    bi = jnp.arange(B)[:, None, None, None]
    hi = jnp.arange(H)[None, :, None, None]
    offs = jnp.arange(MOBA_BLOCK, dtype=jnp.int32)
    scale = DH ** -0.5

    def attend(args):
        qi, pi = args
        own = pi // MOBA_BLOCK
        gate = jnp.einsum('bhqd,bhnd->bhqn', qi.astype(jnp.float32), means)
        is_past = jnp.arange(nb, dtype=jnp.int32)[None, :] < own[:, None]
        gate = jnp.where(is_past, gate, -jnp.inf)
        _, sel = lax.top_k(gate, n_sel)
        sel = sel.astype(jnp.int32)
        sel_ok = sel < own[:, None]
        own_b = jnp.broadcast_to(own[:, None], (B, H, qc_len, 1)).astype(jnp.int32)
        blocks = jnp.concatenate([sel, own_b], axis=-1)
        blk_ok = jnp.concatenate([sel_ok, jnp.ones(own_b.shape, dtype=bool)], axis=-1)
        kg = kb[bi, hi, blocks].reshape(B, H, qc_len, -1, DH)
        vg = vb[bi, hi, blocks].reshape(B, H, qc_len, -1, DH)
        key_pos = blocks[..., None] * MOBA_BLOCK + offs
        ok = (blk_ok[..., None] & (key_pos <= pi[None, None, :, None, None])).reshape(B, H, qc_len, -1)
        logits = jnp.einsum('bhqd,bhqkd->bhqk', qi, kg, preferred_element_type=jnp.float32) * scale
        p = jax.nn.softmax(jnp.where(ok, logits, -jnp.inf), axis=-1)
        return jnp.einsum('bhqk,bhqkd->bhqd', p.astype(vg.dtype), vg)

    o = lax.map(attend, (qc, pc))
    return o.transpose(1, 0, 3, 2, 4).reshape(B, T, H, DH)


def _moba_layer(h, pos, k_full, v_full, w_q, w_o):
    B, T, _ = h.shape
    q, g = jnp.split(h @ w_q, 2, axis=-1)
    o = _moba_attend(q.reshape(B, T, H_B, DH_B), pos, k_full, v_full).reshape(B, T, D_MODEL)
    return (jax.nn.silu(g) * o) @ w_o


def _run_group(x, c, s_in, past_k, past_v, norm_g, w_mod, b_mod, w_in_a, w_out_a, w_q_b, w_o_b,
               kv_norm_g, w_mod_kv, b_mod_kv, w_kv, final_g, w_mod_f, b_mod_f):
    B, T, _ = x.shape
    past = past_k.shape[1]
    pos = past + jnp.arange(T, dtype=jnp.int32)
    new_s = []
    k_new = v_new = k_full = v_full = None
    for l in range(DEPTH):
        shift, scale, gate = jnp.split(c @ w_mod[l] + b_mod[l], 3, axis=-1)
        h = _modulated_norm(x, norm_g[l], shift, scale)
        if l < N_A:
            out, s = _retention_layer(h, pos, s_in[l], w_in_a[l], w_out_a[l])
            new_s.append(s)
        else:
            out = _moba_layer(h, pos, k_full, v_full, w_q_b[l - N_A], w_o_b[l - N_A])
        x = x + gate[:, None, :] * out
        if l == N_A - 1:
            kv_shift, kv_scale = jnp.split(c @ w_mod_kv + b_mod_kv, 2, axis=-1)
            hk = _modulated_norm(x, kv_norm_g, kv_shift, kv_scale)
            k_new, v_new = jnp.split(hk @ w_kv, 2, axis=-1)
            k_new = k_new.reshape(B, T, H_B, DH_B)
            v_new = v_new.reshape(B, T, H_B, DH_B)
            k_full = jnp.concatenate([past_k.astype(k_new.dtype), k_new], axis=1)
            v_full = jnp.concatenate([past_v.astype(v_new.dtype), v_new], axis=1)
    f_shift, f_scale = jnp.split(c @ w_mod_f + b_mod_f, 2, axis=-1)
    y = _modulated_norm(x, final_g, f_shift, f_scale)
    return y, jnp.stack(new_s), k_new, v_new


def setup_inputs(seed: int = 0) -> dict:
    key = jax.random.key(seed)
    ks = jax.random.split(key, 24)
    f32 = jnp.float32
    n_pages = PAST_LEN // PAGE_SIZE
    n_used = DEC_BATCH * n_pages
    n_pool = n_used + max(1, n_used // 4)
    D = D_MODEL

    def nrm(k, shape, s):
        return jax.random.normal(k, shape, f32) * s

    page_table = jax.random.permutation(ks[0], n_pool)[:n_used].reshape(DEC_BATCH, n_pages).astype(jnp.int32)
    mod_s = 0.5 * D ** -0.5
    return {
        'x_prompt': nrm(ks[1], (BATCH, SEQ, D), 1.0),
        'x_sample': nrm(ks[2], (DEC_BATCH, DEC_SEQ, D), 1.0),
        'state_ret': nrm(ks[3], (N_A, DEC_BATCH, H_A, DK_A, DV_A), 0.5),
        'cache_k': nrm(ks[4], (n_pool, PAGE_SIZE, H_B, DH_B), 1.0),
        'cache_v': nrm(ks[5], (n_pool, PAGE_SIZE, H_B, DH_B), 1.0),
        'page_table': page_table,
        'c_prompt': nrm(ks[6], (BATCH, D), 1.0),
        'c_sample': nrm(ks[7], (DEC_BATCH, D), 1.0),
        'norm_g': 1.0 + nrm(ks[8], (DEPTH, D), 0.02),
        'w_mod': nrm(ks[9], (DEPTH, D, 3 * D), mod_s),
        'b_mod': nrm(ks[10], (DEPTH, 3 * D), 0.02),
        'w_in_a': nrm(ks[11], (N_A, D, 2 * QK_A + 2 * V_A), D ** -0.5),
        'w_out_a': nrm(ks[12], (N_A, V_A, D), V_A ** -0.5),
        'w_q_b': nrm(ks[13], (N_B, D, 2 * D), D ** -0.5),
        'w_o_b': nrm(ks[14], (N_B, D, D), D ** -0.5),
        'kv_norm_g': 1.0 + nrm(ks[15], (D,), 0.02),
        'w_mod_kv': nrm(ks[16], (D, 2 * D), mod_s),
        'b_mod_kv': nrm(ks[17], (2 * D,), 0.02),
        'w_kv': nrm(ks[18], (D, 2 * H_B * DH_B), D ** -0.5),
        'final_g': 1.0 + nrm(ks[19], (D,), 0.02),
        'w_mod_f': nrm(ks[20], (D, 2 * D), mod_s),
        'b_mod_f': nrm(ks[21], (2 * D,), 0.02),
    }


def reference(x_prompt, x_sample, state_ret, cache_k, cache_v, page_table, c_prompt, c_sample,
              norm_g, w_mod, b_mod, w_in_a, w_out_a, w_q_b, w_o_b,
              kv_norm_g, w_mod_kv, b_mod_kv, w_kv, final_g, w_mod_f, b_mod_f):
    Bp = x_prompt.shape[0]
    s0_prompt = jnp.zeros((N_A, Bp, H_A, DK_A, DV_A), state_ret.dtype)
    empty_k = jnp.zeros((Bp, 0, H_B, DH_B), cache_k.dtype)
    y_prompt, state_ret_prompt, k_prompt, v_prompt = _run_group(
        x_prompt, c_prompt, s0_prompt, empty_k, empty_k,
        norm_g, w_mod, b_mod, w_in_a, w_out_a, w_q_b, w_o_b,
        kv_norm_g, w_mod_kv, b_mod_kv, w_kv, final_g, w_mod_f, b_mod_f)
    Bd, n_pages = page_table.shape
    past_len = n_pages * cache_k.shape[1]
    past_k = cache_k[page_table].reshape(Bd, past_len, H_B, DH_B)
    past_v = cache_v[page_table].reshape(Bd, past_len, H_B, DH_B)
    y_sample, state_ret_sample, k_sample, v_sample = _run_group(
        x_sample, c_sample, state_ret, past_k, past_v,
        norm_g, w_mod, b_mod, w_in_a, w_out_a, w_q_b, w_o_b,
        kv_norm_g, w_mod_kv, b_mod_kv, w_kv, final_g, w_mod_f, b_mod_f)
    return (y_prompt, y_sample, state_ret_prompt, state_ret_sample, k_prompt, v_prompt, k_sample, v_sample)
```

```python
import functools
import math

import jax
import jax.numpy as jnp
from jax import lax
from jax.experimental import pallas as pl
from jax.experimental.pallas import tpu as pltpu

F32 = jnp.float32
BF16 = jnp.bfloat16

D_MODEL = 2048
DEPTH = 4
N_A = DEPTH // 2
H_A = 8
DK_A = D_MODEL // H_A
DV_A = 2 * DK_A
QK_A = H_A * DK_A
V_A = H_A * DV_A
RET_CHUNK = 128
H_B = 16
DH_B = D_MODEL // H_B
MOBA_BLOCK = 256
MOBA_TOPK = 3
ROPE_BASE = 10000.0
EPS = 1e-6

VMEM_LIMIT_BYTES = 56 * 1024 * 1024
MOD_ROWS = 16


def _params(*sem):
    return pltpu.CompilerParams(dimension_semantics=sem, vmem_limit_bytes=VMEM_LIMIT_BYTES)


def _mod_kernel(c_ref, w_ref, b_ref, o_ref):
    acc = jnp.dot(c_ref[...].astype(BF16), w_ref[...].astype(BF16), preferred_element_type=F32)
    o_ref[...] = acc + b_ref[...]


def _mod_matmul(c, w, b, tn=1024):
    n_l, d, n = w.shape
    return pl.pallas_call(
        _mod_kernel,
        grid=(n_l, n // tn),
        in_specs=[
            pl.BlockSpec((MOD_ROWS, d), lambda l, j: (0, 0)),
            pl.BlockSpec((None, d, tn), lambda l, j: (l, 0, j)),
            pl.BlockSpec((None, 1, tn), lambda l, j: (l, 0, j)),
        ],
        out_specs=pl.BlockSpec((None, MOD_ROWS, tn), lambda l, j: (l, 0, j)),
        out_shape=jax.ShapeDtypeStruct((n_l, MOD_ROWS, n), F32),
        compiler_params=_params("parallel", "parallel"),
        name="mod_matmul",
    )(c, w, b)


def _norm_mod_kernel(x_ref, g_ref, sh_ref, sc_ref, o_ref):
    x = x_ref[...]
    r = lax.rsqrt(jnp.mean(x * x, axis=-1, keepdims=True) + EPS)
    y = (x * r) * g_ref[...]
    o_ref[...] = (y * (1.0 + sc_ref[...]) + sh_ref[...]).astype(o_ref.dtype)


def _norm_mod(x, g, shift, scale, rows_per_batch, out_dtype, tm):
    r, d = x.shape
    if rows_per_batch:
        tiles_per_b = rows_per_batch // tm
        mod_spec = pl.BlockSpec((None, 1, d), lambda i: (i // tiles_per_b, 0, 0))
    else:
        mod_spec = pl.BlockSpec((tm, d), lambda i: (i, 0))
    return pl.pallas_call(
        _norm_mod_kernel,
        grid=(r // tm,),
        in_specs=[
            pl.BlockSpec((tm, d), lambda i: (i, 0)),
            pl.BlockSpec((1, d), lambda i: (0, 0)),
            mod_spec,
            mod_spec,
        ],
        out_specs=pl.BlockSpec((tm, d), lambda i: (i, 0)),
        out_shape=jax.ShapeDtypeStruct((r, d), out_dtype),
        compiler_params=_params("parallel"),
        name="norm_mod",
    )(x, g, shift, scale)


def _matmul_kernel(a_ref, w_ref, o_ref):
    o_ref[...] = jnp.dot(
        a_ref[...].astype(BF16), w_ref[...].astype(BF16), preferred_element_type=F32
    ).astype(o_ref.dtype)


def _matmul_res_kernel(a_ref, w_ref, x_ref, gm_ref, o_ref):
    acc = jnp.dot(a_ref[...].astype(BF16), w_ref[...].astype(BF16), preferred_element_type=F32)
    o_ref[...] = x_ref[...] + gm_ref[...] * acc


def _matmul(a, w, tm, tn, res=None, gate=None, rows_per_batch=0):
    r, k = a.shape
    n = w.shape[1]
    in_specs = [
        pl.BlockSpec((tm, k), lambda i, j: (i, 0)),
        pl.BlockSpec((k, tn), lambda i, j: (0, j)),
    ]
    args = [a, w]
    kernel = _matmul_kernel
    if res is not None:
        kernel = _matmul_res_kernel
        in_specs.append(pl.BlockSpec((tm, tn), lambda i, j: (i, j)))
        if rows_per_batch:
            tiles_per_b = rows_per_batch // tm
            in_specs.append(pl.BlockSpec((None, 1, tn), lambda i, j: (i // tiles_per_b, 0, j)))
        else:
            in_specs.append(pl.BlockSpec((tm, tn), lambda i, j: (i, j)))
        args += [res, gate]
    return pl.pallas_call(
        kernel,
        grid=(r // tm, n // tn),
        in_specs=in_specs,
        out_specs=pl.BlockSpec((tm, tn), lambda i, j: (i, j)),
        out_shape=jax.ShapeDtypeStruct((r, n), F32),
        compiler_params=_params("parallel", "parallel"),
        name="matmul_res" if res is not None else "matmul",
    )(*args)


def _rope_tables(pos):
    half = DK_A // 2
    inv = 1.0 / (ROPE_BASE ** jnp.linspace(0.0, 1.0, half, dtype=F32))
    ang = pos.astype(F32)[:, None] * inv[None, :]
    return jnp.cos(ang), jnp.sin(ang)


def _retention_tables(chunk):
    log_g = jnp.log1p(-(2.0 ** (-5.0 - jnp.arange(H_A, dtype=F32))))
    i = jnp.arange(chunk, dtype=F32)
    diff = i[:, None] - i[None, :]
    decay = jnp.where(diff >= 0, jnp.exp(jnp.maximum(diff, 0.0)[None] * log_g[:, None, None]), 0.0)
    cross_scale = jnp.exp((i + 1.0)[:, None] * log_g[None, :]).T[:, :, None]
    wk = jnp.exp((chunk - 1.0 - i)[:, None] * log_g[None, :]).T[:, :, None]
    state_decay = jnp.exp(chunk * log_g)
    return decay, cross_scale, wk, state_decay


def _retention_kernel(sdec_ref, q_ref, k_ref, v_ref, g_ref, cos_ref, sin_ref, dec_ref, cs_ref,
                      wk_ref, s0_ref, a_ref, sout_ref, s_scr, *, n_chunks):
    h = pl.program_id(1)
    c = pl.program_id(2)

    @pl.when(c == 0)
    def _():
        s_scr[...] = s0_ref[...]

    cos = cos_ref[...]
    sin = sin_ref[...]
    half = DK_A // 2

    def rot(x):
        x1 = x[:, :half]
        x2 = x[:, half:]
        return jnp.concatenate([x1 * cos - x2 * sin, x1 * sin + x2 * cos], axis=-1)

    q = rot(q_ref[...])
    k = rot(k_ref[...]) * (DK_A ** -0.5)
    vb = v_ref[...].astype(BF16)
    s = s_scr[...]
    qb = q.astype(BF16)
    kb = k.astype(BF16)
    scores = lax.dot_general(qb, kb, (((1,), (1,)), ((), ())), preferred_element_type=F32) * dec_ref[...]
    inner = jnp.dot(scores.astype(BF16), vb, preferred_element_type=F32)
    cross = jnp.dot(qb, s.astype(BF16), preferred_element_type=F32) * cs_ref[...]
    kw = (k * wk_ref[...]).astype(BF16)
    upd = lax.dot_general(kw, vb, (((0,), (0,)), ((), ())), preferred_element_type=F32)
    s_new = sdec_ref[h] * s + upd
    s_scr[...] = s_new

    o = inner + cross
    o = o * lax.rsqrt(jnp.mean(o * o, axis=-1, keepdims=True) + EPS)
    g = g_ref[...]
    a_ref[...] = (g * jax.nn.sigmoid(g) * o).astype(a_ref.dtype)

    @pl.when(c == n_chunks - 1)
    def _():
        sout_ref[...] = s_new


def _retention(qkvg, s0, pos, out_dtype):
    b, t, _ = qkvg.shape
    chunk = math.gcd(t, RET_CHUNK)
    n_chunks = t // chunk
    cos, sin = _rope_tables(pos)
    decay, cross_scale, wk, state_decay = _retention_tables(chunk)
    k_off = QK_A // DK_A
    v_off = 2 * QK_A // DV_A
    g_off = (2 * QK_A + V_A) // DV_A
    kernel = functools.partial(_retention_kernel, n_chunks=n_chunks)
    return pl.pallas_call(
        kernel,
        grid=(b, H_A, n_chunks),
        in_specs=[
            pl.BlockSpec(memory_space=pltpu.SMEM),
            pl.BlockSpec((None, chunk, DK_A), lambda bi, h, c: (bi, c, h)),
            pl.BlockSpec((None, chunk, DK_A), lambda bi, h, c: (bi, c, k_off + h)),
            pl.BlockSpec((None, chunk, DV_A), lambda bi, h, c: (bi, c, v_off + h)),
            pl.BlockSpec((None, chunk, DV_A), lambda bi, h, c: (bi, c, g_off + h)),
            pl.BlockSpec((chunk, DK_A // 2), lambda bi, h, c: (c, 0)),
            pl.BlockSpec((chunk, DK_A // 2), lambda bi, h, c: (c, 0)),
            pl.BlockSpec((None, chunk, chunk), lambda bi, h, c: (h, 0, 0)),
            pl.BlockSpec((None, chunk, 1), lambda bi, h, c: (h, 0, 0)),
            pl.BlockSpec((None, chunk, 1), lambda bi, h, c: (h, 0, 0)),
            pl.BlockSpec((None, None, DK_A, DV_A), lambda bi, h, c: (bi, h, 0, 0)),
        ],
        out_specs=[
            pl.BlockSpec((None, chunk, DV_A), lambda bi, h, c: (bi, c, h)),
            pl.BlockSpec((None, None, DK_A, DV_A), lambda bi, h, c: (bi, h, 0, 0)),
        ],
        out_shape=[
            jax.ShapeDtypeStruct((b, t, V_A), out_dtype),
            jax.ShapeDtypeStruct((b, H_A, DK_A, DV_A), F32),
        ],
        scratch_shapes=[pltpu.VMEM((DK_A, DV_A), F32)],
        compiler_params=_params("parallel", "parallel", "arbitrary"),
        name="retention",
    )(state_decay, qkvg, qkvg, qkvg, qkvg, cos, sin, decay, cross_scale, wk, s0)


def _select_topk(gate, n_sel):
    lane = lax.broadcasted_iota(jnp.int32, gate.shape, 1).astype(F32)
    picks = []
    for _ in range(n_sel):
        m = jnp.max(gate, axis=-1, keepdims=True)
        idx = jnp.min(jnp.where(gate == m, lane, float(gate.shape[-1])), axis=-1, keepdims=True)
        hit = (lane == idx) & (m > -jnp.inf)
        picks.append((idx, hit))
        gate = jnp.where(hit, -jnp.inf, gate)
    return picks


def _moba_prompt_kernel(q_ref, g_ref, k_ref, v_ref, a_ref, *, n_blocks):
    i = pl.program_id(2)
    bs = MOBA_BLOCK
    q = q_ref[...]
    k = k_ref[...]
    qb = q.astype(BF16)
    kb = k.astype(BF16)
    vb = v_ref[...].astype(BF16)

    means = jnp.mean(k.reshape(n_blocks, bs, DH_B), axis=1)
    means = jnp.concatenate([means, jnp.zeros((128 - n_blocks, DH_B), F32)], axis=0)
    gate = lax.dot_general(qb, means.astype(BF16), (((1,), (1,)), ((), ())), preferred_element_type=F32)
    lane = lax.broadcasted_iota(jnp.int32, gate.shape, 1)
    gate = jnp.where(lane < i, gate, -jnp.inf)
    sel = jnp.zeros(gate.shape, F32)
    for _, hit in _select_topk(gate, min(MOBA_TOPK, n_blocks)):
        sel = jnp.where(hit, 1.0, sel)

    s = lax.dot_general(qb, kb, (((1,), (1,)), ((), ())), preferred_element_type=F32) * (DH_B ** -0.5)
    row = lax.broadcasted_iota(jnp.int32, (bs, bs), 0)
    col = lax.broadcasted_iota(jnp.int32, (bs, bs), 1)
    causal = col <= row
    masked = []
    m = jnp.full((bs, 1), -jnp.inf, F32)
    for j in range(n_blocks):
        ok = (jnp.broadcast_to(sel[:, j:j + 1], (bs, bs)) > 0.5) | (causal & (i == j))
        sj = jnp.where(ok, s[:, j * bs:(j + 1) * bs], -jnp.inf)
        masked.append(sj)
        m = jnp.maximum(m, jnp.max(sj, axis=-1, keepdims=True))
    l = jnp.zeros((bs, 1), F32)
    acc = jnp.zeros((bs, DH_B), F32)
    for j in range(n_blocks):
        p = jnp.exp(masked[j] - m)
        l = l + jnp.sum(p, axis=-1, keepdims=True)
        acc = acc + jnp.dot(p.astype(BF16), vb[j * bs:(j + 1) * bs, :], preferred_element_type=F32)
    o = acc / l
    g = g_ref[...]
    a_ref[...] = (g * jax.nn.sigmoid(g) * o).astype(a_ref.dtype)


def _moba_prompt(qg, kv, out_dtype):
    b, t, _ = qg.shape
    n_blocks = t // MOBA_BLOCK
    assert n_blocks * MOBA_BLOCK == t and n_blocks <= 128
    kernel = functools.partial(_moba_prompt_kernel, n_blocks=n_blocks)
    return pl.pallas_call(
        kernel,
        grid=(b, H_B, n_blocks),
        in_specs=[
            pl.BlockSpec((None, MOBA_BLOCK, DH_B), lambda bi, h, i: (bi, i, h)),
            pl.BlockSpec((None, MOBA_BLOCK, DH_B), lambda bi, h, i: (bi, i, H_B + h)),
            pl.BlockSpec((None, t, DH_B), lambda bi, h, i: (bi, 0, h)),
            pl.BlockSpec((None, t, DH_B), lambda bi, h, i: (bi, 0, H_B + h)),
        ],
        out_specs=pl.BlockSpec((None, MOBA_BLOCK, DH_B), lambda bi, h, i: (bi, i, h)),
        out_shape=jax.ShapeDtypeStruct((b, t, D_MODEL), out_dtype),
        compiler_params=_params("parallel", "parallel", "arbitrary"),
        name="moba_prompt",
    )(qg, qg, kv, kv)


def _block_means_kernel(pt_ref, ka_ref, kb_ref, o_ref):
    total = jnp.sum(ka_ref[...], axis=0, keepdims=True) + jnp.sum(kb_ref[...], axis=0, keepdims=True)
    o_ref[...] = total / MOBA_BLOCK


def _block_means(cache_k, page_table_flat, n_seq, n_pages):
    _, page, width = cache_k.shape
    pages_per_block = MOBA_BLOCK // page
    assert pages_per_block == 2
    n_blocks = n_pages // pages_per_block

    def page_map(half):
        return lambda bi, j, pt: (pt[bi * n_pages + pages_per_block * j + half], 0, 0)

    return pl.pallas_call(
        _block_means_kernel,
        grid_spec=pltpu.PrefetchScalarGridSpec(
            num_scalar_prefetch=1,
            grid=(n_seq, n_blocks),
            in_specs=[
                pl.BlockSpec((None, page, width), page_map(0)),
                pl.BlockSpec((None, page, width), page_map(1)),
            ],
            out_specs=pl.BlockSpec((None, None, 1, width), lambda bi, j, pt: (bi, j, 0, 0)),
        ),
        out_shape=jax.ShapeDtypeStruct((n_seq, n_blocks, 1, width), F32),
        compiler_params=_params("parallel", "arbitrary"),
        name="block_means",
    )(page_table_flat, cache_k, cache_k)


def _moba_select_kernel(q_ref, means_ref, o_ref, *, t, n_blocks, past):
    q = q_ref[...]
    rows = t * H_B
    qrep = jnp.concatenate([jnp.broadcast_to(q[ti:ti + 1, :], (H_B, D_MODEL)) for ti in range(t)], axis=0)
    lane_h = lax.broadcasted_iota(jnp.int32, (rows, D_MODEL), 1) >> (DH_B.bit_length() - 1)
    row_h = lax.broadcasted_iota(jnp.int32, (rows, D_MODEL), 0) & (H_B - 1)
    qexp = jnp.where(lane_h == row_h, qrep, 0.0)
    means = jnp.concatenate([means_ref[...], jnp.zeros((128 - n_blocks, D_MODEL), F32)], axis=0)
    gate = lax.dot_general(qexp.astype(BF16), means.astype(BF16), (((1,), (1,)), ((), ())),
                           preferred_element_type=F32)
    lane = lax.broadcasted_iota(jnp.int32, gate.shape, 1)
    token = lax.broadcasted_iota(jnp.int32, gate.shape, 0) >> (H_B.bit_length() - 1)
    own = (past + token) >> (MOBA_BLOCK.bit_length() - 1)
    gate = jnp.where(lane < own, gate, -jnp.inf)
    out = jnp.zeros(gate.shape, F32)
    for r, (idx, _) in enumerate(_select_topk(gate, MOBA_TOPK)):
        out = jnp.where(lane == r, idx, out)
    o_ref[...] = out.astype(jnp.int32)


def _moba_select(qg, means, past):
    b, t, _ = qg.shape
    n_blocks = means.shape[1]
    assert MOBA_TOPK <= n_blocks <= 128
    kernel = functools.partial(_moba_select_kernel, t=t, n_blocks=n_blocks, past=past)
    return pl.pallas_call(
        kernel,
        grid=(b,),
        in_specs=[
            pl.BlockSpec((None, t, D_MODEL), lambda bi: (bi, 0, 0)),
            pl.BlockSpec((None, n_blocks, D_MODEL), lambda bi: (bi, 0, 0)),
        ],
        out_specs=pl.BlockSpec((None, t * H_B, 128), lambda bi: (bi, 0, 0)),
        out_shape=jax.ShapeDtypeStruct((b, t * H_B, 128), jnp.int32),
        compiler_params=_params("parallel"),
        name="moba_select",
    )(qg, means)


def _moba_sample_kernel(pt_ref, sel_ref, q_ref, g_ref, kn_ref, vn_ref, *refs, t, n_slots):
    k_refs = refs[:n_slots]
    v_refs = refs[n_slots:2 * n_slots]
    a_ref = refs[2 * n_slots]
    page = k_refs[0].shape[0]
    keys_per_token = (n_slots // t) * page

    q = q_ref[...]
    qb = q.astype(BF16)
    kall = jnp.concatenate([r[...] for r in k_refs], axis=0).astype(BF16)
    vall = jnp.concatenate([r[...] for r in v_refs], axis=0).astype(BF16)
    s = lax.dot_general(qb, kall, (((1,), (1,)), ((), ())), preferred_element_type=F32) * (DH_B ** -0.5)
    row = lax.broadcasted_iota(jnp.int32, s.shape, 0)
    col = lax.broadcasted_iota(jnp.int32, s.shape, 1)
    mine = (col >= row * keys_per_token) & (col < (row + 1) * keys_per_token)
    s = jnp.where(mine, s, -jnp.inf)

    kn = kn_ref[...]
    vn = vn_ref[...]
    rown = lax.broadcasted_iota(jnp.int32, (t, 1), 0)
    s_own = []
    for tj in range(t):
        sj = jnp.sum(q * kn[tj:tj + 1, :], axis=-1, keepdims=True) * (DH_B ** -0.5)
        s_own.append(jnp.where(rown >= tj, sj, -jnp.inf))
    m = jnp.max(s, axis=-1, keepdims=True)
    for sj in s_own:
        m = jnp.maximum(m, sj)
    p = jnp.exp(s - m)
    l = jnp.sum(p, axis=-1, keepdims=True)
    acc = jnp.dot(p.astype(BF16), vall, preferred_element_type=F32)
    for tj, sj in enumerate(s_own):
        pj = jnp.exp(sj - m)
        l = l + pj
        acc = acc + pj * vn[tj:tj + 1, :]
    o = acc / l
    g = g_ref[...]
    a_ref[...] = (g * jax.nn.sigmoid(g) * o).astype(a_ref.dtype)


def _moba_sample(qg, kv_new, cache_k, cache_v, page_table_flat, sel_flat, n_pages):
    b, t, _ = qg.shape
    page = cache_k.shape[1]
    pages_per_block = MOBA_BLOCK // page
    n_slots = t * MOBA_TOPK * pages_per_block

    def slot_map(ti, r, half):
        def index_map(bi, h, pt, sel):
            blk = sel[((bi * t + ti) * H_B + h) * MOBA_TOPK + r]
            return (pt[bi * n_pages + blk * pages_per_block + half], 0, h)
        return index_map

    def slot_specs():
        return [pl.BlockSpec((None, page, DH_B), slot_map(ti, r, half))
                for ti in range(t) for r in range(MOBA_TOPK) for half in range(pages_per_block)]

    kernel = functools.partial(_moba_sample_kernel, t=t, n_slots=n_slots)
    return pl.pallas_call(
        kernel,
        grid_spec=pltpu.PrefetchScalarGridSpec(
            num_scalar_prefetch=2,
            grid=(b, H_B),
            in_specs=[
                pl.BlockSpec((None, t, DH_B), lambda bi, h, pt, sel: (bi, 0, h)),
                pl.BlockSpec((None, t, DH_B), lambda bi, h, pt, sel: (bi, 0, H_B + h)),
                pl.BlockSpec((None, t, DH_B), lambda bi, h, pt, sel: (bi, 0, h)),
                pl.BlockSpec((None, t, DH_B), lambda bi, h, pt, sel: (bi, 0, H_B + h)),
            ] + slot_specs() + slot_specs(),
            out_specs=pl.BlockSpec((None, t, DH_B), lambda bi, h, pt, sel: (bi, 0, h)),
        ),
        out_shape=jax.ShapeDtypeStruct((b, t, D_MODEL), F32),
        compiler_params=_params("parallel", "arbitrary"),
        name="moba_sample",
    )(page_table_flat, sel_flat, qg, qg, kv_new, kv_new, *([cache_k] * n_slots), *([cache_v] * n_slots))


def _run_group(x, mods, mods_kv, mods_f, s_in, past, weights, paged):
    (norm_g, w_in_a, w_out_a, w_q_b, w_o_b, kv_norm_g, w_kv, final_g) = weights
    b, t, d = x.shape
    rows = b * t
    prompt = paged is None
    pos = past + jnp.arange(t, dtype=jnp.int32)
    if prompt:
        tm, tn, tm_norm, rpb = 1024, 512, 512, t
        act_dtype = BF16
        expand = lambda v: v[:, None, :]
    else:
        tm, tn, tm_norm, rpb = rows, 1024, rows, 0
        act_dtype = F32
        expand = lambda v: jnp.repeat(v, t, axis=0)

    x2 = x.reshape(rows, d)
    new_s = []
    kv = None
    for l in range(DEPTH):
        shift, scale, gate = (expand(mods[l][:, i * d:(i + 1) * d]) for i in range(3))
        h = _norm_mod(x2, norm_g[l][None, :], shift, scale, rpb, act_dtype, tm_norm)
        if l < N_A:
            qkvg = _matmul(h, w_in_a[l], tm, tn).reshape(b, t, -1)
            a, s = _retention(qkvg, s_in[l], pos, act_dtype)
            new_s.append(s)
            x2 = _matmul(a.reshape(rows, V_A), w_out_a[l], tm, tn, res=x2, gate=gate, rows_per_batch=rpb)
        else:
            lb = l - N_A
            qg = _matmul(h, w_q_b[lb], tm, tn).reshape(b, t, -1)
            if prompt:
                a = _moba_prompt(qg, kv.reshape(b, t, -1), act_dtype)
            else:
                cache_k, cache_v, pt_flat, means, n_pages = paged
                sel = _moba_select(qg, means, past)[:, :, :MOBA_TOPK].reshape(-1)
                a = _moba_sample(qg, kv.reshape(b, t, -1), cache_k, cache_v, pt_flat, sel, n_pages)
            x2 = _matmul(a.reshape(rows, d), w_o_b[lb], tm, tn, res=x2, gate=gate, rows_per_batch=rpb)
        if l == N_A - 1:
            kv_shift, kv_scale = (expand(mods_kv[:, i * d:(i + 1) * d]) for i in range(2))
            hk = _norm_mod(x2, kv_norm_g[None, :], kv_shift, kv_scale, rpb, act_dtype, tm_norm)
            kv = _matmul(hk, w_kv, tm, tn)
    f_shift, f_scale = (expand(mods_f[:, i * d:(i + 1) * d]) for i in range(2))
    y = _norm_mod(x2, final_g[None, :], f_shift, f_scale, rpb, F32, tm_norm)
    k_new = kv[:, :d].reshape(b, t, H_B, DH_B)
    v_new = kv[:, d:].reshape(b, t, H_B, DH_B)
    return y.reshape(b, t, d), jnp.stack(new_s), k_new, v_new


def kernel(x_prompt, x_sample, state_ret, cache_k, cache_v, page_table, c_prompt, c_sample,
           norm_g, w_mod, b_mod, w_in_a, w_out_a, w_q_b, w_o_b,
           kv_norm_g, w_mod_kv, b_mod_kv, w_kv, final_g, w_mod_f, b_mod_f):
    bp = x_prompt.shape[0]
    bd, n_pages = page_table.shape
    n_pool, page, _, _ = cache_k.shape
    past_len = n_pages * page
    assert bp + bd <= MOD_ROWS

    c_all = jnp.concatenate([c_prompt, c_sample, jnp.zeros((MOD_ROWS - bp - bd, D_MODEL), F32)], axis=0)
    mods = _mod_matmul(c_all, w_mod, b_mod[:, None, :])
    w2 = jnp.stack([w_mod_kv, w_mod_f])
    b2 = jnp.stack([b_mod_kv, b_mod_f])[:, None, :]
    mods2 = _mod_matmul(c_all, w2, b2)

    weights = (norm_g, w_in_a, w_out_a, w_q_b, w_o_b, kv_norm_g, w_kv, final_g)

    s0_prompt = jnp.zeros((N_A, bp, H_A, DK_A, DV_A), state_ret.dtype)
    y_p, s_p, k_p, v_p = _run_group(
        x_prompt, mods[:, :bp], mods2[0, :bp], mods2[1, :bp], s0_prompt, 0, weights, None)

    ck = cache_k.reshape(n_pool, page, H_B * DH_B)
    cv = cache_v.reshape(n_pool, page, H_B * DH_B)
    pt_flat = page_table.reshape(-1)
    means = _block_means(ck, pt_flat, bd, n_pages).reshape(bd, -1, H_B * DH_B)
    y_s, s_s, k_s, v_s = _run_group(
        x_sample, mods[:, bp:bp + bd], mods2[0, bp:bp + bd], mods2[1, bp:bp + bd], state_ret, past_len,
        weights, (ck, cv, pt_flat, means, n_pages))
    return (y_p, y_s, s_p, s_s, k_p, v_p, k_s, v_s)
```

```python
import functools
import math

import jax
import jax.numpy as jnp
from jax import lax
from jax.experimental import pallas as pl
from jax.experimental.pallas import tpu as pltpu

F32 = jnp.float32
BF16 = jnp.bfloat16

D_MODEL = 2048
DEPTH = 4
N_A = DEPTH // 2
H_A = 8
DK_A = D_MODEL // H_A
DV_A = 2 * DK_A
QK_A = H_A * DK_A
V_A = H_A * DV_A
RET_CHUNK = 128
H_B = 16
DH_B = D_MODEL // H_B
MOBA_BLOCK = 256
MOBA_TOPK = 3
ROPE_BASE = 10000.0
EPS = 1e-6

VMEM_LIMIT_BYTES = 56 * 1024 * 1024
MOD_ROWS = 16


def _params(*sem):
    return pltpu.CompilerParams(dimension_semantics=sem, vmem_limit_bytes=VMEM_LIMIT_BYTES)


def _mod_kernel(c_ref, w_ref, b_ref, o_ref):
    acc = jnp.dot(c_ref[...].astype(BF16), w_ref[...].astype(BF16), preferred_element_type=F32)
    o_ref[...] = acc + b_ref[...]


def _mod_matmul(c, w, b, tn=1024):
    n_l, d, n = w.shape
    return pl.pallas_call(
        _mod_kernel,
        grid=(n_l, n // tn),
        in_specs=[
            pl.BlockSpec((MOD_ROWS, d), lambda l, j: (0, 0)),
            pl.BlockSpec((None, d, tn), lambda l, j: (l, 0, j)),
            pl.BlockSpec((None, 1, tn), lambda l, j: (l, 0, j)),
        ],
        out_specs=pl.BlockSpec((None, MOD_ROWS, tn), lambda l, j: (l, 0, j)),
        out_shape=jax.ShapeDtypeStruct((n_l, MOD_ROWS, n), F32),
        compiler_params=_params("parallel", "parallel"),
        name="mod_matmul",
    )(c, w, b)


def _norm_mod_kernel(x_ref, g_ref, sh_ref, sc_ref, o_ref):
    x = x_ref[...]
    r = lax.rsqrt(jnp.mean(x * x, axis=-1, keepdims=True) + EPS)
    y = (x * r) * g_ref[...]
    o_ref[...] = (y * (1.0 + sc_ref[...]) + sh_ref[...]).astype(o_ref.dtype)


def _norm_mod(x, g, shift, scale, rows_per_batch, out_dtype, tm):
    r, d = x.shape
    if rows_per_batch:
        tiles_per_b = rows_per_batch // tm
        mod_spec = pl.BlockSpec((None, 1, d), lambda i: (i // tiles_per_b, 0, 0))
    else:
        mod_spec = pl.BlockSpec((tm, d), lambda i: (i, 0))
    return pl.pallas_call(
        _norm_mod_kernel,
        grid=(r // tm,),
        in_specs=[
            pl.BlockSpec((tm, d), lambda i: (i, 0)),
            pl.BlockSpec((1, d), lambda i: (0, 0)),
            mod_spec,
            mod_spec,
        ],
        out_specs=pl.BlockSpec((tm, d), lambda i: (i, 0)),
        out_shape=jax.ShapeDtypeStruct((r, d), out_dtype),
        compiler_params=_params("parallel"),
        name="norm_mod",
    )(x, g, shift, scale)


def _matmul_kernel(a_ref, w_ref, o_ref):
    o_ref[...] = jnp.dot(
        a_ref[...].astype(BF16), w_ref[...].astype(BF16), preferred_element_type=F32
    ).astype(o_ref.dtype)


def _matmul_res_kernel(a_ref, w_ref, x_ref, gm_ref, o_ref):
    acc = jnp.dot(a_ref[...].astype(BF16), w_ref[...].astype(BF16), preferred_element_type=F32)
    o_ref[...] = x_ref[...] + gm_ref[...] * acc


def _matmul(a, w, tm, tn, res=None, gate=None, rows_per_batch=0):
    r, k = a.shape
    n = w.shape[1]
    in_specs = [
        pl.BlockSpec((tm, k), lambda i, j: (i, 0)),
        pl.BlockSpec((k, tn), lambda i, j: (0, j)),
    ]
    args = [a, w]
    kernel = _matmul_kernel
    if res is not None:
        kernel = _matmul_res_kernel
        in_specs.append(pl.BlockSpec((tm, tn), lambda i, j: (i, j)))
        if rows_per_batch:
            tiles_per_b = rows_per_batch // tm
            in_specs.append(pl.BlockSpec((None, 1, tn), lambda i, j: (i // tiles_per_b, 0, j)))
        else:
            in_specs.append(pl.BlockSpec((tm, tn), lambda i, j: (i, j)))
        args += [res, gate]
    return pl.pallas_call(
        kernel,
        grid=(r // tm, n // tn),
        in_specs=in_specs,
        out_specs=pl.BlockSpec((tm, tn), lambda i, j: (i, j)),
        out_shape=jax.ShapeDtypeStruct((r, n), F32),
        compiler_params=_params("parallel", "parallel"),
        name="matmul_res" if res is not None else "matmul",
    )(*args)


def _rope_tables(pos):
    half = DK_A // 2
    inv = 1.0 / (ROPE_BASE ** jnp.linspace(0.0, 1.0, half, dtype=F32))
    ang = pos.astype(F32)[:, None] * inv[None, :]
    return jnp.cos(ang), jnp.sin(ang)


def _retention_tables(chunk):
    log_g = jnp.log1p(-(2.0 ** (-5.0 - jnp.arange(H_A, dtype=F32))))
    i = jnp.arange(chunk, dtype=F32)
    diff = i[:, None] - i[None, :]
    decay = jnp.where(diff >= 0, jnp.exp(jnp.maximum(diff, 0.0)[None] * log_g[:, None, None]), 0.0)
    cross_scale = jnp.exp((i + 1.0)[:, None] * log_g[None, :]).T[:, :, None]
    wk = jnp.exp((chunk - 1.0 - i)[:, None] * log_g[None, :]).T[:, :, None]
    state_decay = jnp.exp(chunk * log_g)
    return decay, cross_scale, wk, state_decay


def _retention_kernel(sdec_ref, q_ref, k_ref, v_ref, g_ref, cos_ref, sin_ref, dec_ref, cs_ref,
                      wk_ref, s0_ref, a_ref, sout_ref, s_scr, *, n_chunks):
    h = pl.program_id(1)
    c = pl.program_id(2)

    @pl.when(c == 0)
    def _():
        s_scr[...] = s0_ref[...]

    cos = cos_ref[...]
    sin = sin_ref[...]
    half = DK_A // 2

    def rot(x):
        x1 = x[:, :half]
        x2 = x[:, half:]
        return jnp.concatenate([x1 * cos - x2 * sin, x1 * sin + x2 * cos], axis=-1)

    q = rot(q_ref[...])
    k = rot(k_ref[...]) * (DK_A ** -0.5)
    vb = v_ref[...].astype(BF16)
    s = s_scr[...]
    qb = q.astype(BF16)
    kb = k.astype(BF16)
    scores = lax.dot_general(qb, kb, (((1,), (1,)), ((), ())), preferred_element_type=F32) * dec_ref[...]
    inner = jnp.dot(scores.astype(BF16), vb, preferred_element_type=F32)
    cross = jnp.dot(qb, s.astype(BF16), preferred_element_type=F32) * cs_ref[...]
    kw = (k * wk_ref[...]).astype(BF16)
    upd = lax.dot_general(kw, vb, (((0,), (0,)), ((), ())), preferred_element_type=F32)
    s_new = sdec_ref[h] * s + upd
    s_scr[...] = s_new

    o = inner + cross
    o = o * lax.rsqrt(jnp.mean(o * o, axis=-1, keepdims=True) + EPS)
    g = g_ref[...]
    a_ref[...] = (g * jax.nn.sigmoid(g) * o).astype(a_ref.dtype)

    @pl.when(c == n_chunks - 1)
    def _():
        sout_ref[...] = s_new


def _retention(qkvg, s0, pos, out_dtype):
    b, t, _ = qkvg.shape
    chunk = math.gcd(t, RET_CHUNK)
    n_chunks = t // chunk
    cos, sin = _rope_tables(pos)
    decay, cross_scale, wk, state_decay = _retention_tables(chunk)
    k_off = QK_A // DK_A
    v_off = 2 * QK_A // DV_A
    g_off = (2 * QK_A + V_A) // DV_A
    kernel = functools.partial(_retention_kernel, n_chunks=n_chunks)
    return pl.pallas_call(
        kernel,
        grid=(b, H_A, n_chunks),
        in_specs=[
            pl.BlockSpec(memory_space=pltpu.SMEM),
            pl.BlockSpec((None, chunk, DK_A), lambda bi, h, c: (bi, c, h)),
            pl.BlockSpec((None, chunk, DK_A), lambda bi, h, c: (bi, c, k_off + h)),
            pl.BlockSpec((None, chunk, DV_A), lambda bi, h, c: (bi, c, v_off + h)),
            pl.BlockSpec((None, chunk, DV_A), lambda bi, h, c: (bi, c, g_off + h)),
            pl.BlockSpec((chunk, DK_A // 2), lambda bi, h, c: (c, 0)),
            pl.BlockSpec((chunk, DK_A // 2), lambda bi, h, c: (c, 0)),
            pl.BlockSpec((None, chunk, chunk), lambda bi, h, c: (h, 0, 0)),
            pl.BlockSpec((None, chunk, 1), lambda bi, h, c: (h, 0, 0)),
            pl.BlockSpec((None, chunk, 1), lambda bi, h, c: (h, 0, 0)),
            pl.BlockSpec((None, None, DK_A, DV_A), lambda bi, h, c: (bi, h, 0, 0)),
        ],
        out_specs=[
            pl.BlockSpec((None, chunk, DV_A), lambda bi, h, c: (bi, c, h)),
            pl.BlockSpec((None, None, DK_A, DV_A), lambda bi, h, c: (bi, h, 0, 0)),
        ],
        out_shape=[
            jax.ShapeDtypeStruct((b, t, V_A), out_dtype),
            jax.ShapeDtypeStruct((b, H_A, DK_A, DV_A), F32),
        ],
        scratch_shapes=[pltpu.VMEM((DK_A, DV_A), F32)],
        compiler_params=_params("parallel", "parallel", "arbitrary"),
        name="retention",
    )(state_decay, qkvg, qkvg, qkvg, qkvg, cos, sin, decay, cross_scale, wk, s0)


def _select_topk(gate, n_sel):
    lane = lax.broadcasted_iota(jnp.int32, gate.shape, 1).astype(F32)
    picks = []
    for _ in range(n_sel):
        m = jnp.max(gate, axis=-1, keepdims=True)
        idx = jnp.min(jnp.where(gate == m, lane, float(gate.shape[-1])), axis=-1, keepdims=True)
        valid = m > -jnp.inf
        picks.append((idx, valid))
        gate = jnp.where((lane == idx) & valid, -jnp.inf, gate)
    return picks


def _moba_prompt_kernel(q_ref, g_ref, k_ref, v_ref, a_ref, *, n_blocks):
    bs = MOBA_BLOCK
    scale = DH_B ** -0.5
    n_sel = min(MOBA_TOPK, n_blocks)
    k = k_ref[...]
    kb = k.astype(BF16)
    vb = v_ref[...].astype(BF16)
    means = jnp.mean(k.reshape(n_blocks, bs, DH_B), axis=1)
    means_b = jnp.concatenate([means, jnp.zeros((128 - n_blocks, DH_B), F32)], axis=0).astype(BF16)
    row = lax.broadcasted_iota(jnp.int32, (bs, bs), 0)
    col = lax.broadcasted_iota(jnp.int32, (bs, bs), 1)
    causal = col <= row

    for i in range(n_blocks):
        rows = slice(i * bs, (i + 1) * bs)
        qb = q_ref[rows, :].astype(BF16)
        s = lax.dot_general(qb, kb[:(i + 1) * bs, :], (((1,), (1,)), ((), ())),
                            preferred_element_type=F32) * scale
        picks = None
        if i > n_sel:
            gate = lax.dot_general(qb, means_b, (((1,), (1,)), ((), ())), preferred_element_type=F32)
            lane = lax.broadcasted_iota(jnp.int32, gate.shape, 1)
            picks = _select_topk(jnp.where(lane < i, gate, -jnp.inf), n_sel)
        blocks = []
        for j in range(i + 1):
            sj = s[:, j * bs:(j + 1) * bs]
            if j == i:
                sj = jnp.where(causal, sj, -jnp.inf)
            elif picks is not None:
                chosen = picks[0][0] == float(j)
                for idx, _ in picks[1:]:
                    chosen = chosen | (idx == float(j))
                sj = jnp.where(chosen, sj, -jnp.inf)
            blocks.append(sj)
        m = jnp.max(functools.reduce(jnp.maximum, blocks), axis=-1, keepdims=True)
        p_sum = None
        acc = jnp.zeros((bs, DH_B), F32)
        for j, sj in enumerate(blocks):
            p = jnp.exp(sj - m)
            p_sum = p if p_sum is None else p_sum + p
            acc = acc + jnp.dot(p.astype(BF16), vb[j * bs:(j + 1) * bs, :], preferred_element_type=F32)
        l = jnp.sum(p_sum, axis=-1, keepdims=True)
        g = g_ref[rows, :]
        a_ref[rows, :] = (g * jax.nn.sigmoid(g) * (acc / l)).astype(a_ref.dtype)


def _moba_prompt(qg, kv, out_dtype):
    b, t, _ = qg.shape
    n_blocks = t // MOBA_BLOCK
    assert n_blocks * MOBA_BLOCK == t and n_blocks <= 128
    kernel = functools.partial(_moba_prompt_kernel, n_blocks=n_blocks)
    return pl.pallas_call(
        kernel,
        grid=(b, H_B),
        in_specs=[
            pl.BlockSpec((None, t, DH_B), lambda bi, h: (bi, 0, h)),
            pl.BlockSpec((None, t, DH_B), lambda bi, h: (bi, 0, H_B + h)),
            pl.BlockSpec((None, t, DH_B), lambda bi, h: (bi, 0, h)),
            pl.BlockSpec((None, t, DH_B), lambda bi, h: (bi, 0, H_B + h)),
        ],
        out_specs=pl.BlockSpec((None, t, DH_B), lambda bi, h: (bi, 0, h)),
        out_shape=jax.ShapeDtypeStruct((b, t, D_MODEL), out_dtype),
        compiler_params=_params("parallel", "parallel"),
        name="moba_prompt",
    )(qg, qg, kv, kv)


def _gather_cache_kernel(pt_ref, pa_ref, pb_ref, o_ref, *maybe_means_ref, page):
    for h in range(H_B):
        o_ref[h, 0:page, :] = pa_ref[pl.ds(h, page, stride=H_B), :].astype(BF16)
        o_ref[h, page:2 * page, :] = pb_ref[pl.ds(h, page, stride=H_B), :].astype(BF16)
    if maybe_means_ref:
        total = (jnp.sum(pa_ref[...].reshape(page, H_B, DH_B), axis=0)
                 + jnp.sum(pb_ref[...].reshape(page, H_B, DH_B), axis=0))
        maybe_means_ref[0][...] = total / MOBA_BLOCK


def _gather_cache(cache, page_table_flat, n_seq, n_pages, with_means):
    n_pool, page, n_h, dh = cache.shape
    pages_per_block = MOBA_BLOCK // page
    assert pages_per_block == 2 and n_h == H_B and dh == DH_B
    n_blocks = n_pages // pages_per_block
    rows = cache.reshape(n_pool, page * n_h, dh)

    def page_map(half):
        return lambda bi, j, pt: (pt[bi * n_pages + pages_per_block * j + half], 0, 0)

    out_specs = [pl.BlockSpec((None, n_h, MOBA_BLOCK, dh), lambda bi, j, pt: (bi, 0, j, 0))]
    out_shape = [jax.ShapeDtypeStruct((n_seq, n_h, n_pages * page, dh), BF16)]
    if with_means:
        out_specs.append(pl.BlockSpec((None, None, n_h, dh), lambda bi, j, pt: (bi, j, 0, 0)))
        out_shape.append(jax.ShapeDtypeStruct((n_seq, n_blocks, n_h, dh), F32))
    return pl.pallas_call(
        functools.partial(_gather_cache_kernel, page=page),
        grid_spec=pltpu.PrefetchScalarGridSpec(
            num_scalar_prefetch=1,
            grid=(n_seq, n_blocks),
            in_specs=[
                pl.BlockSpec((None, page * n_h, dh), page_map(0)),
                pl.BlockSpec((None, page * n_h, dh), page_map(1)),
            ],
            out_specs=out_specs,
        ),
        out_shape=out_shape,
        compiler_params=_params("parallel", "arbitrary"),
        name="gather_cache",
    )(page_table_flat, rows, rows)


def _moba_select_kernel(q_ref, means_ref, o_ref, *, t, n_blocks, past):
    q = q_ref[...]
    rows = t * H_B
    qrep = jnp.concatenate([jnp.broadcast_to(q[ti:ti + 1, :], (H_B, D_MODEL)) for ti in range(t)], axis=0)
    lane_h = lax.broadcasted_iota(jnp.int32, (rows, D_MODEL), 1) >> (DH_B.bit_length() - 1)
    row_h = lax.broadcasted_iota(jnp.int32, (rows, D_MODEL), 0) & (H_B - 1)
    qexp = jnp.where(lane_h == row_h, qrep, 0.0)
    means = jnp.concatenate([means_ref[...], jnp.zeros((128 - n_blocks, D_MODEL), F32)], axis=0)
    gate = lax.dot_general(qexp.astype(BF16), means.astype(BF16), (((1,), (1,)), ((), ())),
                           preferred_element_type=F32)
    lane = lax.broadcasted_iota(jnp.int32, gate.shape, 1)
    token = lax.broadcasted_iota(jnp.int32, gate.shape, 0) >> (H_B.bit_length() - 1)
    own = (past + token) >> (MOBA_BLOCK.bit_length() - 1)
    gate = jnp.where(lane < own, gate, -jnp.inf)
    out = jnp.zeros(gate.shape, F32)
    for r, (idx, _) in enumerate(_select_topk(gate, MOBA_TOPK)):
        out = jnp.where(lane == r, idx, out)
    o_ref[...] = out.astype(jnp.int32)


def _moba_select(qg, means, past):
    b, t, _ = qg.shape
    n_blocks = means.shape[1]
    assert MOBA_TOPK <= n_blocks <= 128
    kernel = functools.partial(_moba_select_kernel, t=t, n_blocks=n_blocks, past=past)
    return pl.pallas_call(
        kernel,
        grid=(b,),
        in_specs=[
            pl.BlockSpec((None, t, D_MODEL), lambda bi: (bi, 0, 0)),
            pl.BlockSpec((None, n_blocks, D_MODEL), lambda bi: (bi, 0, 0)),
        ],
        out_specs=pl.BlockSpec((None, t * H_B, 128), lambda bi: (bi, 0, 0)),
        out_shape=jax.ShapeDtypeStruct((b, t * H_B, 128), jnp.int32),
        compiler_params=_params("parallel"),
        name="moba_select",
    )(qg, means)


def _moba_sample_kernel(sel_ref, q_ref, g_ref, kn_ref, vn_ref, *refs, t, n_slots):
    k_refs = refs[:n_slots]
    v_refs = refs[n_slots:2 * n_slots]
    a_ref = refs[2 * n_slots]
    keys_per_token = (n_slots // t) * k_refs[0].shape[0]

    q = q_ref[...]
    qb = q.astype(BF16)
    kall = jnp.concatenate([r[...] for r in k_refs], axis=0)
    vall = jnp.concatenate([r[...] for r in v_refs], axis=0)
    s = lax.dot_general(qb, kall, (((1,), (1,)), ((), ())), preferred_element_type=F32) * (DH_B ** -0.5)
    row = lax.broadcasted_iota(jnp.int32, s.shape, 0)
    col = lax.broadcasted_iota(jnp.int32, s.shape, 1)
    mine = (col >= row * keys_per_token) & (col < (row + 1) * keys_per_token)
    s = jnp.where(mine, s, -jnp.inf)

    kn = kn_ref[...]
    vn = vn_ref[...]
    rown = lax.broadcasted_iota(jnp.int32, (t, 1), 0)
    s_own = []
    for tj in range(t):
        sj = jnp.sum(q * kn[tj:tj + 1, :], axis=-1, keepdims=True) * (DH_B ** -0.5)
        s_own.append(jnp.where(rown >= tj, sj, -jnp.inf))
    m = jnp.max(s, axis=-1, keepdims=True)
    for sj in s_own:
        m = jnp.maximum(m, sj)
    p = jnp.exp(s - m)
    l = jnp.sum(p, axis=-1, keepdims=True)
    acc = jnp.dot(p.astype(BF16), vall, preferred_element_type=F32)
    for tj, sj in enumerate(s_own):
        pj = jnp.exp(sj - m)
        l = l + pj
        acc = acc + pj * vn[tj:tj + 1, :]
    o = acc / l
    g = g_ref[...]
    a_ref[...] = (g * jax.nn.sigmoid(g) * o).astype(a_ref.dtype)


def _moba_sample(qg, kv_new, past_k, past_v, sel_flat):
    b, t, _ = qg.shape
    n_slots = t * MOBA_TOPK

    def slot_map(ti, r):
        def index_map(bi, h, sel):
            return (bi, h, sel[((bi * t + ti) * H_B + h) * MOBA_TOPK + r], 0)
        return index_map

    def slot_specs():
        return [pl.BlockSpec((None, None, MOBA_BLOCK, DH_B), slot_map(ti, r))
                for ti in range(t) for r in range(MOBA_TOPK)]

    kernel = functools.partial(_moba_sample_kernel, t=t, n_slots=n_slots)
    return pl.pallas_call(
        kernel,
        grid_spec=pltpu.PrefetchScalarGridSpec(
            num_scalar_prefetch=1,
            grid=(b, H_B),
            in_specs=[
                pl.BlockSpec((None, t, DH_B), lambda bi, h, sel: (bi, 0, h)),
                pl.BlockSpec((None, t, DH_B), lambda bi, h, sel: (bi, 0, H_B + h)),
                pl.BlockSpec((None, t, DH_B), lambda bi, h, sel: (bi, 0, h)),
                pl.BlockSpec((None, t, DH_B), lambda bi, h, sel: (bi, 0, H_B + h)),
            ] + slot_specs() + slot_specs(),
            out_specs=pl.BlockSpec((None, t, DH_B), lambda bi, h, sel: (bi, 0, h)),
        ),
        out_shape=jax.ShapeDtypeStruct((b, t, D_MODEL), F32),
        compiler_params=_params("parallel", "arbitrary"),
        name="moba_sample",
    )(sel_flat, qg, qg, kv_new, kv_new, *([past_k] * n_slots), *([past_v] * n_slots))


def _run_group(x, mods, mods_kv, mods_f, s_in, past, weights, paged):
    (norm_g, w_in_a, w_out_a, w_q_b, w_o_b, kv_norm_g, w_kv, final_g) = weights
    b, t, d = x.shape
    rows = b * t
    prompt = paged is None
    pos = past + jnp.arange(t, dtype=jnp.int32)
    if prompt:
        tm, tn, tm_norm, rpb = 1024, 512, 512, t
        act_dtype = BF16
        expand = lambda v: v[:, None, :]
    else:
        tm, tn, tm_norm, rpb = rows, 1024, rows, 0
        act_dtype = F32
        expand = lambda v: jnp.repeat(v, t, axis=0)

    x2 = x.reshape(rows, d)
    new_s = []
    kv = None
    for l in range(DEPTH):
        shift, scale, gate = (expand(mods[l][:, i * d:(i + 1) * d]) for i in range(3))
        h = _norm_mod(x2, norm_g[l][None, :], shift, scale, rpb, act_dtype, tm_norm)
        if l < N_A:
            qkvg = _matmul(h, w_in_a[l], tm, tn).reshape(b, t, -1)
            a, s = _retention(qkvg, s_in[l], pos, act_dtype)
            new_s.append(s)
            x2 = _matmul(a.reshape(rows, V_A), w_out_a[l], tm, tn, res=x2, gate=gate, rows_per_batch=rpb)
        else:
            lb = l - N_A
            qg = _matmul(h, w_q_b[lb], tm, tn).reshape(b, t, -1)
            if prompt:
                a = _moba_prompt(qg, kv.reshape(b, t, -1), act_dtype)
            else:
                past_k, past_v, means = paged
                sel = _moba_select(qg, means, past)[:, :, :MOBA_TOPK].reshape(-1)
                a = _moba_sample(qg, kv.reshape(b, t, -1), past_k, past_v, sel)
            x2 = _matmul(a.reshape(rows, d), w_o_b[lb], tm, tn, res=x2, gate=gate, rows_per_batch=rpb)
        if l == N_A - 1:
            kv_shift, kv_scale = (expand(mods_kv[:, i * d:(i + 1) * d]) for i in range(2))
            hk = _norm_mod(x2, kv_norm_g[None, :], kv_shift, kv_scale, rpb, act_dtype, tm_norm)
            kv = _matmul(hk, w_kv, tm, tn)
    f_shift, f_scale = (expand(mods_f[:, i * d:(i + 1) * d]) for i in range(2))
    y = _norm_mod(x2, final_g[None, :], f_shift, f_scale, rpb, F32, tm_norm)
    k_new = kv[:, :d].reshape(b, t, H_B, DH_B)
    v_new = kv[:, d:].reshape(b, t, H_B, DH_B)
    return y.reshape(b, t, d), jnp.stack(new_s), k_new, v_new


def kernel(x_prompt, x_sample, state_ret, cache_k, cache_v, page_table, c_prompt, c_sample,
           norm_g, w_mod, b_mod, w_in_a, w_out_a, w_q_b, w_o_b,
           kv_norm_g, w_mod_kv, b_mod_kv, w_kv, final_g, w_mod_f, b_mod_f):
    bp = x_prompt.shape[0]
    bd, n_pages = page_table.shape
    n_pool, page, _, _ = cache_k.shape
    past_len = n_pages * page
    assert bp + bd <= MOD_ROWS

    c_all = jnp.concatenate([c_prompt, c_sample, jnp.zeros((MOD_ROWS - bp - bd, D_MODEL), F32)], axis=0)
    mods = _mod_matmul(c_all, w_mod, b_mod[:, None, :])
    mods_kv = _mod_matmul(c_all, w_mod_kv[None], b_mod_kv[None, None, :])[0]
    mods_f = _mod_matmul(c_all, w_mod_f[None], b_mod_f[None, None, :])[0]

    weights = (norm_g, w_in_a, w_out_a, w_q_b, w_o_b, kv_norm_g, w_kv, final_g)

    s0_prompt = jnp.zeros((N_A, bp, H_A, DK_A, DV_A), state_ret.dtype)
    y_p, s_p, k_p, v_p = _run_group(
        x_prompt, mods[:, :bp], mods_kv[:bp], mods_f[:bp], s0_prompt, 0, weights, None)

    pt_flat = page_table.reshape(-1)
    past_k, means = _gather_cache(cache_k, pt_flat, bd, n_pages, with_means=True)
    (past_v,) = _gather_cache(cache_v, pt_flat, bd, n_pages, with_means=False)
    means = means.reshape(bd, -1, H_B * DH_B)
    y_s, s_s, k_s, v_s = _run_group(
        x_sample, mods[:, bp:bp + bd], mods_kv[bp:bp + bd], mods_f[bp:bp + bd], state_ret, past_len,
        weights, (past_k, past_v, means))
    return (y_p, y_s, s_p, s_s, k_p, v_p, k_s, v_s)
```

```python
import functools
import math

import jax
import jax.numpy as jnp
from jax import lax
from jax.experimental import pallas as pl
from jax.experimental.pallas import tpu as pltpu

F32 = jnp.float32
BF16 = jnp.bfloat16

D_MODEL = 2048
DEPTH = 4
N_A = DEPTH // 2
H_A = 8
DK_A = D_MODEL // H_A
DV_A = 2 * DK_A
QK_A = H_A * DK_A
V_A = H_A * DV_A
RET_CHUNK = 128
H_B = 16
DH_B = D_MODEL // H_B
MOBA_BLOCK = 256
MOBA_TOPK = 3
ROPE_BASE = 10000.0
EPS = 1e-6

VMEM_LIMIT_BYTES = 56 * 1024 * 1024
MOD_ROWS = 16


def _params(*sem):
    return pltpu.CompilerParams(dimension_semantics=sem, vmem_limit_bytes=VMEM_LIMIT_BYTES)


def _mod_kernel(c_ref, w_ref, b_ref, o_ref):
    acc = jnp.dot(c_ref[...].astype(BF16), w_ref[...].astype(BF16), preferred_element_type=F32)
    o_ref[...] = acc + b_ref[...]


def _mod_matmul(c, w, b, tn=1024):
    n_l, d, n = w.shape
    return pl.pallas_call(
        _mod_kernel,
        grid=(n_l, n // tn),
        in_specs=[
            pl.BlockSpec((MOD_ROWS, d), lambda l, j: (0, 0)),
            pl.BlockSpec((None, d, tn), lambda l, j: (l, 0, j)),
            pl.BlockSpec((None, 1, tn), lambda l, j: (l, 0, j)),
        ],
        out_specs=pl.BlockSpec((None, MOD_ROWS, tn), lambda l, j: (l, 0, j)),
        out_shape=jax.ShapeDtypeStruct((n_l, MOD_ROWS, n), F32),
        compiler_params=_params("parallel", "parallel"),
        name="mod_matmul",
    )(c, w, b)


def _norm_mod_kernel(x_ref, g_ref, sh_ref, sc_ref, o_ref):
    x = x_ref[...]
    r = lax.rsqrt(jnp.mean(x * x, axis=-1, keepdims=True) + EPS)
    y = (x * r) * g_ref[...]
    o_ref[...] = (y * (1.0 + sc_ref[...]) + sh_ref[...]).astype(o_ref.dtype)


def _norm_mod(x, g, shift, scale, rows_per_batch, out_dtype, tm):
    r, d = x.shape
    if rows_per_batch:
        tiles_per_b = rows_per_batch // tm
        mod_spec = pl.BlockSpec((None, 1, d), lambda i: (i // tiles_per_b, 0, 0))
    else:
        mod_spec = pl.BlockSpec((tm, d), lambda i: (i, 0))
    return pl.pallas_call(
        _norm_mod_kernel,
        grid=(r // tm,),
        in_specs=[
            pl.BlockSpec((tm, d), lambda i: (i, 0)),
            pl.BlockSpec((1, d), lambda i: (0, 0)),
            mod_spec,
            mod_spec,
        ],
        out_specs=pl.BlockSpec((tm, d), lambda i: (i, 0)),
        out_shape=jax.ShapeDtypeStruct((r, d), out_dtype),
        compiler_params=_params("parallel"),
        name="norm_mod",
    )(x, g, shift, scale)


def _matmul_kernel(a_ref, w_ref, o_ref):
    o_ref[...] = jnp.dot(
        a_ref[...].astype(BF16), w_ref[...].astype(BF16), preferred_element_type=F32
    ).astype(o_ref.dtype)


def _matmul_res_kernel(a_ref, w_ref, x_ref, gm_ref, o_ref):
    acc = jnp.dot(a_ref[...].astype(BF16), w_ref[...].astype(BF16), preferred_element_type=F32)
    o_ref[...] = x_ref[...] + gm_ref[...] * acc


MATMUL_W_TILE_ELEMS = 2 * 1024 * 1024


def _matmul(a, w, tm, tn, res=None, gate=None, rows_per_batch=0):
    r, k = a.shape
    n = w.shape[1]
    tn = min(tn, MATMUL_W_TILE_ELEMS // k)
    in_specs = [
        pl.BlockSpec((tm, k), lambda i, j: (i, 0)),
        pl.BlockSpec((k, tn), lambda i, j: (0, j)),
    ]
    args = [a, w]
    kernel = _matmul_kernel
    if res is not None:
        kernel = _matmul_res_kernel
        in_specs.append(pl.BlockSpec((tm, tn), lambda i, j: (i, j)))
        if rows_per_batch:
            tiles_per_b = rows_per_batch // tm
            in_specs.append(pl.BlockSpec((None, 1, tn), lambda i, j: (i // tiles_per_b, 0, j)))
        else:
            in_specs.append(pl.BlockSpec((tm, tn), lambda i, j: (i, j)))
        args += [res, gate]
    return pl.pallas_call(
        kernel,
        grid=(r // tm, n // tn),
        in_specs=in_specs,
        out_specs=pl.BlockSpec((tm, tn), lambda i, j: (i, j)),
        out_shape=jax.ShapeDtypeStruct((r, n), F32),
        compiler_params=_params("parallel", "parallel"),
        name="matmul_res" if res is not None else "matmul",
    )(*args)


def _rope_tables(pos):
    half = DK_A // 2
    inv = 1.0 / (ROPE_BASE ** jnp.linspace(0.0, 1.0, half, dtype=F32))
    ang = pos.astype(F32)[:, None] * inv[None, :]
    return jnp.cos(ang), jnp.sin(ang)


def _retention_tables(chunk):
    log_g = jnp.log1p(-(2.0 ** (-5.0 - jnp.arange(H_A, dtype=F32))))
    i = jnp.arange(chunk, dtype=F32)
    diff = i[:, None] - i[None, :]
    decay = jnp.where(diff >= 0, jnp.exp(jnp.maximum(diff, 0.0)[None] * log_g[:, None, None]), 0.0)
    cross_scale = jnp.exp((i + 1.0)[:, None] * log_g[None, :]).T[:, :, None]
    wk = jnp.exp((chunk - 1.0 - i)[:, None] * log_g[None, :]).T[:, :, None]
    state_decay = jnp.exp(chunk * log_g)
    return decay, cross_scale, wk, state_decay


RET_HEADS_PER_STEP = 4
RET_CHUNKS_PER_STEP = 2


def _retention_kernel(sdec_ref, q_ref, k_ref, v_ref, g_ref, cos_ref, sin_ref, dec_ref, cs_ref,
                      wk_ref, s0_ref, a_ref, sout_ref, s_scr, *, chunk, heads, chunks, n_steps):
    hg = pl.program_id(1)
    c = pl.program_id(2)

    @pl.when(c == 0)
    def _():
        s_scr[...] = s0_ref[...]

    half = DK_A // 2

    def rot(x, cos, sin):
        x1 = x[:, :half]
        x2 = x[:, half:]
        return jnp.concatenate([x1 * cos - x2 * sin, x1 * sin + x2 * cos], axis=-1)

    for hh in range(heads):
        s = s_scr[hh]
        kcols = slice(hh * DK_A, (hh + 1) * DK_A)
        vcols = slice(hh * DV_A, (hh + 1) * DV_A)
        for ci in range(chunks):
            rows = slice(ci * chunk, (ci + 1) * chunk)
            cos = cos_ref[rows, :]
            sin = sin_ref[rows, :]
            q = rot(q_ref[rows, kcols], cos, sin)
            k = rot(k_ref[rows, kcols], cos, sin) * (DK_A ** -0.5)
            vb = v_ref[rows, vcols].astype(BF16)
            qb = q.astype(BF16)
            kb = k.astype(BF16)
            scores = lax.dot_general(qb, kb, (((1,), (1,)), ((), ())),
                                     preferred_element_type=F32) * dec_ref[hh]
            inner = jnp.dot(scores.astype(BF16), vb, preferred_element_type=F32)
            cross = jnp.dot(qb, s.astype(BF16), preferred_element_type=F32) * cs_ref[hh]
            kw = (k * wk_ref[hh]).astype(BF16)
            upd = lax.dot_general(kw, vb, (((0,), (0,)), ((), ())), preferred_element_type=F32)
            s = sdec_ref[hg * heads + hh] * s + upd

            o = inner + cross
            o = o * lax.rsqrt(jnp.mean(o * o, axis=-1, keepdims=True) + EPS)
            g = g_ref[rows, vcols]
            a_ref[rows, vcols] = (g * jax.nn.sigmoid(g) * o).astype(a_ref.dtype)
        s_scr[hh] = s

        @pl.when(c == n_steps - 1)
        def _():
            sout_ref[hh] = s


def _retention(qkvg, s0, pos, out_dtype):
    b, t, _ = qkvg.shape
    chunk = math.gcd(t, RET_CHUNK)
    n_chunks = t // chunk
    heads = RET_HEADS_PER_STEP
    chunks = math.gcd(n_chunks, RET_CHUNKS_PER_STEP)
    n_steps = n_chunks // chunks
    rows = chunk * chunks
    assert H_A % heads == 0
    cos, sin = _rope_tables(pos)
    decay, cross_scale, wk, state_decay = _retention_tables(chunk)
    k_off = QK_A // (heads * DK_A)
    v_off = 2 * QK_A // (heads * DV_A)
    g_off = (2 * QK_A + V_A) // (heads * DV_A)
    kernel = functools.partial(_retention_kernel, chunk=chunk, heads=heads, chunks=chunks, n_steps=n_steps)
    return pl.pallas_call(
        kernel,
        grid=(b, H_A // heads, n_steps),
        in_specs=[
            pl.BlockSpec(memory_space=pltpu.SMEM),
            pl.BlockSpec((None, rows, heads * DK_A), lambda bi, h, c: (bi, c, h)),
            pl.BlockSpec((None, rows, heads * DK_A), lambda bi, h, c: (bi, c, k_off + h)),
            pl.BlockSpec((None, rows, heads * DV_A), lambda bi, h, c: (bi, c, v_off + h)),
            pl.BlockSpec((None, rows, heads * DV_A), lambda bi, h, c: (bi, c, g_off + h)),
            pl.BlockSpec((rows, DK_A // 2), lambda bi, h, c: (c, 0)),
            pl.BlockSpec((rows, DK_A // 2), lambda bi, h, c: (c, 0)),
            pl.BlockSpec((heads, chunk, chunk), lambda bi, h, c: (h, 0, 0)),
            pl.BlockSpec((heads, chunk, 1), lambda bi, h, c: (h, 0, 0)),
            pl.BlockSpec((heads, chunk, 1), lambda bi, h, c: (h, 0, 0)),
            pl.BlockSpec((None, heads, DK_A, DV_A), lambda bi, h, c: (bi, h, 0, 0)),
        ],
        out_specs=[
            pl.BlockSpec((None, rows, heads * DV_A), lambda bi, h, c: (bi, c, h)),
            pl.BlockSpec((None, heads, DK_A, DV_A), lambda bi, h, c: (bi, h, 0, 0)),
        ],
        out_shape=[
            jax.ShapeDtypeStruct((b, t, V_A), out_dtype),
            jax.ShapeDtypeStruct((b, H_A, DK_A, DV_A), F32),
        ],
        scratch_shapes=[pltpu.VMEM((heads, DK_A, DV_A), F32)],
        compiler_params=_params("parallel", "parallel", "arbitrary"),
        name="retention",
    )(state_decay, qkvg, qkvg, qkvg, qkvg, cos, sin, decay, cross_scale, wk, s0)


def _select_topk(gate, n_sel):
    lane = lax.broadcasted_iota(jnp.int32, gate.shape, 1).astype(F32)
    picks = []
    for _ in range(n_sel):
        m = jnp.max(gate, axis=-1, keepdims=True)
        idx = jnp.min(jnp.where(gate == m, lane, float(gate.shape[-1])), axis=-1, keepdims=True)
        valid = m > -jnp.inf
        picks.append((idx, valid))
        gate = jnp.where((lane == idx) & valid, -jnp.inf, gate)
    return picks


def _moba_prompt_kernel(q_ref, g_ref, k_ref, v_ref, a_ref, *, n_blocks):
    bs = MOBA_BLOCK
    scale = DH_B ** -0.5
    n_sel = min(MOBA_TOPK, n_blocks)
    k = k_ref[...]
    kb = k.astype(BF16)
    vb = v_ref[...].astype(BF16)
    means = jnp.mean(k.reshape(n_blocks, bs, DH_B), axis=1)
    means_b = jnp.concatenate([means, jnp.zeros((128 - n_blocks, DH_B), F32)], axis=0).astype(BF16)
    row = lax.broadcasted_iota(jnp.int32, (bs, bs), 0)
    col = lax.broadcasted_iota(jnp.int32, (bs, bs), 1)
    causal = col <= row

    for i in range(n_blocks):
        rows = slice(i * bs, (i + 1) * bs)
        qb = q_ref[rows, :].astype(BF16)
        s = lax.dot_general(qb, kb[:(i + 1) * bs, :], (((1,), (1,)), ((), ())),
                            preferred_element_type=F32) * scale
        picks = None
        if i > n_sel:
            gate = lax.dot_general(qb, means_b, (((1,), (1,)), ((), ())), preferred_element_type=F32)
            lane = lax.broadcasted_iota(jnp.int32, gate.shape, 1)
            picks = _select_topk(jnp.where(lane < i, gate, -jnp.inf), n_sel)
        blocks = []
        for j in range(i + 1):
            sj = s[:, j * bs:(j + 1) * bs]
            if j == i:
                sj = jnp.where(causal, sj, -jnp.inf)
            elif picks is not None:
                chosen = picks[0][0] == float(j)
                for idx, _ in picks[1:]:
                    chosen = chosen | (idx == float(j))
                sj = jnp.where(chosen, sj, -jnp.inf)
            blocks.append(sj)
        m = jnp.max(functools.reduce(jnp.maximum, blocks), axis=-1, keepdims=True)
        p_sum = None
        acc = jnp.zeros((bs, DH_B), F32)
        for j, sj in enumerate(blocks):
            p = jnp.exp(sj - m)
            p_sum = p if p_sum is None else p_sum + p
            acc = acc + jnp.dot(p.astype(BF16), vb[j * bs:(j + 1) * bs, :], preferred_element_type=F32)
        l = jnp.sum(p_sum, axis=-1, keepdims=True)
        g = g_ref[rows, :]
        a_ref[rows, :] = (g * jax.nn.sigmoid(g) * (acc / l)).astype(a_ref.dtype)


def _moba_prompt(qg, kv, out_dtype):
    b, t, _ = qg.shape
    n_blocks = t // MOBA_BLOCK
    assert n_blocks * MOBA_BLOCK == t and n_blocks <= 128
    kernel = functools.partial(_moba_prompt_kernel, n_blocks=n_blocks)
    return pl.pallas_call(
        kernel,
        grid=(b, H_B),
        in_specs=[
            pl.BlockSpec((None, t, DH_B), lambda bi, h: (bi, 0, h)),
            pl.BlockSpec((None, t, DH_B), lambda bi, h: (bi, 0, H_B + h)),
            pl.BlockSpec((None, t, DH_B), lambda bi, h: (bi, 0, h)),
            pl.BlockSpec((None, t, DH_B), lambda bi, h: (bi, 0, H_B + h)),
        ],
        out_specs=pl.BlockSpec((None, t, DH_B), lambda bi, h: (bi, 0, h)),
        out_shape=jax.ShapeDtypeStruct((b, t, D_MODEL), out_dtype),
        compiler_params=_params("parallel", "parallel"),
        name="moba_prompt",
    )(qg, qg, kv, kv)


HEAD_GROUP = 8


def _gather_cache_kernel(pt_ref, *refs, page, n_groups, with_means):
    n_in = 2 * n_groups
    o_ref = refs[n_in]
    for half in range(2):
        for grp in range(n_groups):
            src = refs[half * n_groups + grp].reshape(page * HEAD_GROUP, DH_B)
            for hg in range(HEAD_GROUP):
                rows = src[pl.ds(hg, page, stride=HEAD_GROUP), :]
                o_ref[grp * HEAD_GROUP + hg, half * page:(half + 1) * page, :] = rows.astype(BF16)
    if with_means:
        m_ref = refs[n_in + 1]
        for grp in range(n_groups):
            total = jnp.sum(refs[grp][...], axis=0) + jnp.sum(refs[n_groups + grp][...], axis=0)
            m_ref[grp * HEAD_GROUP:(grp + 1) * HEAD_GROUP, :] = total / MOBA_BLOCK


def _gather_cache(cache, page_table_flat, n_seq, n_pages, with_means):
    n_pool, page, n_h, dh = cache.shape
    pages_per_block = MOBA_BLOCK // page
    assert pages_per_block == 2 and n_h == H_B and dh == DH_B and n_h % HEAD_GROUP == 0
    n_blocks = n_pages // pages_per_block
    n_groups = n_h // HEAD_GROUP
    grouped = cache.reshape(n_pool, page, n_groups, HEAD_GROUP, dh)

    def page_map(half, grp):
        return lambda bi, j, pt: (pt[bi * n_pages + pages_per_block * j + half], 0, grp, 0, 0)

    in_specs = [pl.BlockSpec((None, page, None, HEAD_GROUP, dh), page_map(half, grp))
                for half in range(2) for grp in range(n_groups)]
    out_specs = [pl.BlockSpec((None, n_h, MOBA_BLOCK, dh), lambda bi, j, pt: (bi, 0, j, 0))]
    out_shape = [jax.ShapeDtypeStruct((n_seq, n_h, n_pages * page, dh), BF16)]
    if with_means:
        out_specs.append(pl.BlockSpec((None, None, n_h, dh), lambda bi, j, pt: (bi, j, 0, 0)))
        out_shape.append(jax.ShapeDtypeStruct((n_seq, n_blocks, n_h, dh), F32))
    return pl.pallas_call(
        functools.partial(_gather_cache_kernel, page=page, n_groups=n_groups, with_means=with_means),
        grid_spec=pltpu.PrefetchScalarGridSpec(
            num_scalar_prefetch=1,
            grid=(n_seq, n_blocks),
            in_specs=in_specs,
            out_specs=out_specs,
        ),
        out_shape=out_shape,
        compiler_params=_params("parallel", "arbitrary"),
        name="gather_cache",
    )(page_table_flat, *([grouped] * len(in_specs)))


def _moba_select_kernel(q_ref, means_ref, o_ref, *, t, n_blocks, past):
    q = q_ref[...]
    rows = t * H_B
    qrep = jnp.concatenate([jnp.broadcast_to(q[ti:ti + 1, :], (H_B, D_MODEL)) for ti in range(t)], axis=0)
    lane_h = lax.broadcasted_iota(jnp.int32, (rows, D_MODEL), 1) >> (DH_B.bit_length() - 1)
    row_h = lax.broadcasted_iota(jnp.int32, (rows, D_MODEL), 0) & (H_B - 1)
    qexp = jnp.where(lane_h == row_h, qrep, 0.0)
    means = jnp.concatenate([means_ref[...], jnp.zeros((128 - n_blocks, D_MODEL), F32)], axis=0)
    gate = lax.dot_general(qexp.astype(BF16), means.astype(BF16), (((1,), (1,)), ((), ())),
                           preferred_element_type=F32)
    lane = lax.broadcasted_iota(jnp.int32, gate.shape, 1)
    token = lax.broadcasted_iota(jnp.int32, gate.shape, 0) >> (H_B.bit_length() - 1)
    own = (past + token) >> (MOBA_BLOCK.bit_length() - 1)
    gate = jnp.where(lane < own, gate, -jnp.inf)
    out = jnp.zeros(gate.shape, F32)
    for r, (idx, _) in enumerate(_select_topk(gate, MOBA_TOPK)):
        out = jnp.where(lane == r, idx, out)
    o_ref[...] = out.astype(jnp.int32)


def _moba_select(qg, means, past):
    b, t, _ = qg.shape
    n_blocks = means.shape[1]
    assert MOBA_TOPK <= n_blocks <= 128
    kernel = functools.partial(_moba_select_kernel, t=t, n_blocks=n_blocks, past=past)
    return pl.pallas_call(
        kernel,
        grid=(b,),
        in_specs=[
            pl.BlockSpec((None, t, D_MODEL), lambda bi: (bi, 0, 0)),
            pl.BlockSpec((None, n_blocks, D_MODEL), lambda bi: (bi, 0, 0)),
        ],
        out_specs=pl.BlockSpec((None, t * H_B, 128), lambda bi: (bi, 0, 0)),
        out_shape=jax.ShapeDtypeStruct((b, t * H_B, 128), jnp.int32),
        compiler_params=_params("parallel"),
        name="moba_select",
    )(qg, means)


def _moba_sample_kernel(sel_ref, q_ref, g_ref, kn_ref, vn_ref, *refs, t, n_slots):
    k_refs = refs[:n_slots]
    v_refs = refs[n_slots:2 * n_slots]
    a_ref = refs[2 * n_slots]
    keys_per_token = (n_slots // t) * k_refs[0].shape[0]

    q = q_ref[...]
    qb = q.astype(BF16)
    kall = jnp.concatenate([r[...] for r in k_refs], axis=0)
    vall = jnp.concatenate([r[...] for r in v_refs], axis=0)
    s = lax.dot_general(qb, kall, (((1,), (1,)), ((), ())), preferred_element_type=F32) * (DH_B ** -0.5)
    row = lax.broadcasted_iota(jnp.int32, s.shape, 0)
    col = lax.broadcasted_iota(jnp.int32, s.shape, 1)
    mine = (col >= row * keys_per_token) & (col < (row + 1) * keys_per_token)
    s = jnp.where(mine, s, -jnp.inf)

    kn = kn_ref[...]
    vn = vn_ref[...]
    rown = lax.broadcasted_iota(jnp.int32, (t, 1), 0)
    s_own = []
    for tj in range(t):
        sj = jnp.sum(q * kn[tj:tj + 1, :], axis=-1, keepdims=True) * (DH_B ** -0.5)
        s_own.append(jnp.where(rown >= tj, sj, -jnp.inf))
    m = jnp.max(s, axis=-1, keepdims=True)
    for sj in s_own:
        m = jnp.maximum(m, sj)
    p = jnp.exp(s - m)
    l = jnp.sum(p, axis=-1, keepdims=True)
    acc = jnp.dot(p.astype(BF16), vall, preferred_element_type=F32)
    for tj, sj in enumerate(s_own):
        pj = jnp.exp(sj - m)
        l = l + pj
        acc = acc + pj * vn[tj:tj + 1, :]
    o = acc / l
    g = g_ref[...]
    a_ref[...] = (g * jax.nn.sigmoid(g) * o).astype(a_ref.dtype)


def _moba_sample(qg, kv_new, past_k, past_v, sel_flat):
    b, t, _ = qg.shape
    n_slots = t * MOBA_TOPK

    def slot_map(ti, r):
        def index_map(bi, h, sel):
            return (bi, h, sel[((bi * t + ti) * H_B + h) * MOBA_TOPK + r], 0)
        return index_map

    def slot_specs():
        return [pl.BlockSpec((None, None, MOBA_BLOCK, DH_B), slot_map(ti, r))
                for ti in range(t) for r in range(MOBA_TOPK)]

    kernel = functools.partial(_moba_sample_kernel, t=t, n_slots=n_slots)
    return pl.pallas_call(
        kernel,
        grid_spec=pltpu.PrefetchScalarGridSpec(
            num_scalar_prefetch=1,
            grid=(b, H_B),
            in_specs=[
                pl.BlockSpec((None, t, DH_B), lambda bi, h, sel: (bi, 0, h)),
                pl.BlockSpec((None, t, DH_B), lambda bi, h, sel: (bi, 0, H_B + h)),
                pl.BlockSpec((None, t, DH_B), lambda bi, h, sel: (bi, 0, h)),
                pl.BlockSpec((None, t, DH_B), lambda bi, h, sel: (bi, 0, H_B + h)),
            ] + slot_specs() + slot_specs(),
            out_specs=pl.BlockSpec((None, t, DH_B), lambda bi, h, sel: (bi, 0, h)),
        ),
        out_shape=jax.ShapeDtypeStruct((b, t, D_MODEL), F32),
        compiler_params=_params("parallel", "arbitrary"),
        name="moba_sample",
    )(sel_flat, qg, qg, kv_new, kv_new, *([past_k] * n_slots), *([past_v] * n_slots))


def _run_group(x, mods, mods_kv, mods_f, s_in, past, weights, paged):
    (norm_g, w_in_a, w_out_a, w_q_b, w_o_b, kv_norm_g, w_kv, final_g) = weights
    b, t, d = x.shape
    rows = b * t
    prompt = paged is None
    pos = past + jnp.arange(t, dtype=jnp.int32)
    if prompt:
        tm, tn, tm_norm, rpb = 1024, 1024, 512, t
        act_dtype = BF16
        expand = lambda v: v[:, None, :]
    else:
        tm, tn, tm_norm, rpb = rows, 1024, rows, 0
        act_dtype = F32
        expand = lambda v: jnp.repeat(v, t, axis=0)

    x2 = x.reshape(rows, d)
    new_s = []
    kv = None
    for l in range(DEPTH):
        shift, scale, gate = (expand(mods[l][:, i * d:(i + 1) * d]) for i in range(3))
        h = _norm_mod(x2, norm_g[l][None, :], shift, scale, rpb, act_dtype, tm_norm)
        if l < N_A:
            qkvg = _matmul(h, w_in_a[l], tm, tn).reshape(b, t, -1)
            a, s = _retention(qkvg, s_in[l], pos, act_dtype)
            new_s.append(s)
            x2 = _matmul(a.reshape(rows, V_A), w_out_a[l], tm, tn, res=x2, gate=gate, rows_per_batch=rpb)
        else:
            lb = l - N_A
            qg = _matmul(h, w_q_b[lb], tm, tn).reshape(b, t, -1)
            if prompt:
                a = _moba_prompt(qg, kv.reshape(b, t, -1), act_dtype)
            else:
                past_k, past_v, means = paged
                sel = _moba_select(qg, means, past)[:, :, :MOBA_TOPK].reshape(-1)
                a = _moba_sample(qg, kv.reshape(b, t, -1), past_k, past_v, sel)
            x2 = _matmul(a.reshape(rows, d), w_o_b[lb], tm, tn, res=x2, gate=gate, rows_per_batch=rpb)
        if l == N_A - 1:
            kv_shift, kv_scale = (expand(mods_kv[:, i * d:(i + 1) * d]) for i in range(2))
            hk = _norm_mod(x2, kv_norm_g[None, :], kv_shift, kv_scale, rpb, act_dtype, tm_norm)
            kv = _matmul(hk, w_kv, tm, tn)
    f_shift, f_scale = (expand(mods_f[:, i * d:(i + 1) * d]) for i in range(2))
    y = _norm_mod(x2, final_g[None, :], f_shift, f_scale, rpb, F32, tm_norm)
    k_new = kv[:, :d].reshape(b, t, H_B, DH_B)
    v_new = kv[:, d:].reshape(b, t, H_B, DH_B)
    return y.reshape(b, t, d), jnp.stack(new_s), k_new, v_new


def kernel(x_prompt, x_sample, state_ret, cache_k, cache_v, page_table, c_prompt, c_sample,
           norm_g, w_mod, b_mod, w_in_a, w_out_a, w_q_b, w_o_b,
           kv_norm_g, w_mod_kv, b_mod_kv, w_kv, final_g, w_mod_f, b_mod_f):
    bp = x_prompt.shape[0]
    bd, n_pages = page_table.shape
    n_pool, page, _, _ = cache_k.shape
    past_len = n_pages * page
    assert bp + bd <= MOD_ROWS

    c_all = jnp.concatenate([c_prompt, c_sample, jnp.zeros((MOD_ROWS - bp - bd, D_MODEL), F32)], axis=0)
    mods = _mod_matmul(c_all, w_mod, b_mod[:, None, :])
    mods_kv = _mod_matmul(c_all, w_mod_kv[None], b_mod_kv[None, None, :])[0]
    mods_f = _mod_matmul(c_all, w_mod_f[None], b_mod_f[None, None, :])[0]

    weights = (norm_g, w_in_a, w_out_a, w_q_b, w_o_b, kv_norm_g, w_kv, final_g)

    s0_prompt = jnp.zeros((N_A, bp, H_A, DK_A, DV_A), state_ret.dtype)
    y_p, s_p, k_p, v_p = _run_group(
        x_prompt, mods[:, :bp], mods_kv[:bp], mods_f[:bp], s0_prompt, 0, weights, None)

    pt_flat = page_table.reshape(-1)
    past_k, means = _gather_cache(cache_k, pt_flat, bd, n_pages, with_means=True)
    (past_v,) = _gather_cache(cache_v, pt_flat, bd, n_pages, with_means=False)
    means = means.reshape(bd, -1, H_B * DH_B)
    y_s, s_s, k_s, v_s = _run_group(
        x_sample, mods[:, bp:bp + bd], mods_kv[bp:bp + bd], mods_f[bp:bp + bd], state_ret, past_len,
        weights, (past_k, past_v, means))
    return (y_p, y_s, s_p, s_s, k_p, v_p, k_s, v_s)
```

```python
import functools
import math

import jax
import jax.numpy as jnp
from jax import lax
from jax.experimental import pallas as pl
from jax.experimental.pallas import tpu as pltpu

F32 = jnp.float32
BF16 = jnp.bfloat16

D_MODEL = 2048
DEPTH = 4
N_A = DEPTH // 2
H_A = 8
DK_A = D_MODEL // H_A
DV_A = 2 * DK_A
QK_A = H_A * DK_A
V_A = H_A * DV_A
RET_CHUNK = 128
H_B = 16
DH_B = D_MODEL // H_B
MOBA_BLOCK = 256
MOBA_TOPK = 3
ROPE_BASE = 10000.0
EPS = 1e-6
LOG2_E = math.log2(math.e)

VMEM_LIMIT_BYTES = 56 * 1024 * 1024
MOD_ROWS = 16


def _params(*sem):
    return pltpu.CompilerParams(dimension_semantics=sem, vmem_limit_bytes=VMEM_LIMIT_BYTES)


def _mod_kernel(c_ref, w_ref, b_ref, o_ref):
    acc = jnp.dot(c_ref[...].astype(BF16), w_ref[...].astype(BF16), preferred_element_type=F32)
    o_ref[...] = acc + b_ref[...]


def _mod_matmul(c, w, b, tn=1024):
    n_l, d, n = w.shape
    return pl.pallas_call(
        _mod_kernel,
        grid=(n_l, n // tn),
        in_specs=[
            pl.BlockSpec((MOD_ROWS, d), lambda l, j: (0, 0)),
            pl.BlockSpec((None, d, tn), lambda l, j: (l, 0, j)),
            pl.BlockSpec((None, 1, tn), lambda l, j: (l, 0, j)),
        ],
        out_specs=pl.BlockSpec((None, MOD_ROWS, tn), lambda l, j: (l, 0, j)),
        out_shape=jax.ShapeDtypeStruct((n_l, MOD_ROWS, n), F32),
        compiler_params=_params("parallel", "parallel"),
        name="mod_matmul",
    )(c, w, b)


def _norm_mod_kernel(x_ref, g_ref, sh_ref, sc_ref, o_ref):
    x = x_ref[...]
    r = lax.rsqrt(jnp.mean(x * x, axis=-1, keepdims=True) + EPS)
    y = (x * r) * g_ref[...]
    o_ref[...] = (y * (1.0 + sc_ref[...]) + sh_ref[...]).astype(o_ref.dtype)


def _norm_mod(x, g, shift, scale, rows_per_batch, out_dtype, tm):
    r, d = x.shape
    if rows_per_batch:
        tiles_per_b = rows_per_batch // tm
        mod_spec = pl.BlockSpec((None, 1, d), lambda i: (i // tiles_per_b, 0, 0))
    else:
        mod_spec = pl.BlockSpec((tm, d), lambda i: (i, 0))
    return pl.pallas_call(
        _norm_mod_kernel,
        grid=(r // tm,),
        in_specs=[
            pl.BlockSpec((tm, d), lambda i: (i, 0)),
            pl.BlockSpec((1, d), lambda i: (0, 0)),
            mod_spec,
            mod_spec,
        ],
        out_specs=pl.BlockSpec((tm, d), lambda i: (i, 0)),
        out_shape=jax.ShapeDtypeStruct((r, d), out_dtype),
        compiler_params=_params("parallel"),
        name="norm_mod",
    )(x, g, shift, scale)


def _matmul_kernel(a_ref, w_ref, o_ref):
    o_ref[...] = jnp.dot(
        a_ref[...].astype(BF16), w_ref[...].astype(BF16), preferred_element_type=F32
    ).astype(o_ref.dtype)


def _matmul_res_kernel(a_ref, w_ref, x_ref, gm_ref, o_ref):
    acc = jnp.dot(a_ref[...].astype(BF16), w_ref[...].astype(BF16), preferred_element_type=F32)
    o_ref[...] = x_ref[...] + gm_ref[...] * acc


MATMUL_W_TILE_ELEMS = 2 * 1024 * 1024


def _matmul(a, w, tm, tn, res=None, gate=None, rows_per_batch=0):
    r, k = a.shape
    n = w.shape[1]
    tn = min(tn, MATMUL_W_TILE_ELEMS // k)
    in_specs = [
        pl.BlockSpec((tm, k), lambda i, j: (i, 0)),
        pl.BlockSpec((k, tn), lambda i, j: (0, j)),
    ]
    args = [a, w]
    kernel = _matmul_kernel
    if res is not None:
        kernel = _matmul_res_kernel
        in_specs.append(pl.BlockSpec((tm, tn), lambda i, j: (i, j)))
        if rows_per_batch:
            tiles_per_b = rows_per_batch // tm
            in_specs.append(pl.BlockSpec((None, 1, tn), lambda i, j: (i // tiles_per_b, 0, j)))
        else:
            in_specs.append(pl.BlockSpec((tm, tn), lambda i, j: (i, j)))
        args += [res, gate]
    return pl.pallas_call(
        kernel,
        grid=(r // tm, n // tn),
        in_specs=in_specs,
        out_specs=pl.BlockSpec((tm, tn), lambda i, j: (i, j)),
        out_shape=jax.ShapeDtypeStruct((r, n), F32),
        compiler_params=_params("parallel", "parallel"),
        name="matmul_res" if res is not None else "matmul",
    )(*args)


def _norm_matmul_kernel(x_ref, g_ref, sh_ref, sc_ref, w_ref, o_ref, h_scr):
    @pl.when(pl.program_id(1) == 0)
    def _():
        x = x_ref[...]
        r = lax.rsqrt(jnp.mean(x * x, axis=-1, keepdims=True) + EPS)
        y = (x * r) * g_ref[...]
        h_scr[...] = (y * (1.0 + sc_ref[...]) + sh_ref[...]).astype(BF16)

    o_ref[...] = jnp.dot(h_scr[...], w_ref[...].astype(BF16), preferred_element_type=F32)


def _norm_matmul(x, g, shift, scale, w, rows_per_batch, tm, tn):
    r, k = x.shape
    n = w.shape[1]
    tn = min(tn, MATMUL_W_TILE_ELEMS // k)
    tiles_per_b = rows_per_batch // tm
    mod_spec = pl.BlockSpec((None, 1, k), lambda i, j: (i // tiles_per_b, 0, 0))
    return pl.pallas_call(
        _norm_matmul_kernel,
        grid=(r // tm, n // tn),
        in_specs=[
            pl.BlockSpec((tm, k), lambda i, j: (i, 0)),
            pl.BlockSpec((1, k), lambda i, j: (0, 0)),
            mod_spec,
            mod_spec,
            pl.BlockSpec((k, tn), lambda i, j: (0, j)),
        ],
        out_specs=pl.BlockSpec((tm, tn), lambda i, j: (i, j)),
        out_shape=jax.ShapeDtypeStruct((r, n), F32),
        scratch_shapes=[pltpu.VMEM((tm, k), BF16)],
        compiler_params=_params("parallel", "arbitrary"),
        name="norm_matmul",
    )(x, g, shift, scale, w)


def _split_heads_kernel(kv_ref, k_out_ref, v_out_ref, k_hm_ref, vt_hm_ref, *, tm):
    for h in range(H_B):
        kh = kv_ref[:, h * DH_B:(h + 1) * DH_B]
        vh = kv_ref[:, (H_B + h) * DH_B:(H_B + h + 1) * DH_B]
        k_out_ref[pl.ds(h, tm, stride=H_B), :] = kh
        v_out_ref[pl.ds(h, tm, stride=H_B), :] = vh
        k_hm_ref[h] = kh.astype(BF16)
        vt_hm_ref[h] = vh.T.astype(BF16)


def _split_heads(kv, b, t, tm=256):
    r = b * t
    tiles_per_b = t // tm
    kernel = functools.partial(_split_heads_kernel, tm=tm)
    rows_spec = pl.BlockSpec((tm * H_B, DH_B), lambda i: (i, 0))
    k_out, v_out, k_hm, vt_hm = pl.pallas_call(
        kernel,
        grid=(r // tm,),
        in_specs=[pl.BlockSpec((tm, 2 * D_MODEL), lambda i: (i, 0))],
        out_specs=[
            rows_spec,
            rows_spec,
            pl.BlockSpec((None, H_B, tm, DH_B), lambda i: (i // tiles_per_b, 0, i % tiles_per_b, 0)),
            pl.BlockSpec((None, H_B, DH_B, tm), lambda i: (i // tiles_per_b, 0, 0, i % tiles_per_b)),
        ],
        out_shape=[
            jax.ShapeDtypeStruct((r * H_B, DH_B), F32),
            jax.ShapeDtypeStruct((r * H_B, DH_B), F32),
            jax.ShapeDtypeStruct((b, H_B, t, DH_B), BF16),
            jax.ShapeDtypeStruct((b, H_B, DH_B, t), BF16),
        ],
        compiler_params=_params("parallel"),
        name="split_heads",
    )(kv)
    return k_out.reshape(b, t, H_B, DH_B), v_out.reshape(b, t, H_B, DH_B), k_hm, vt_hm


def _rope_tables(pos):
    half = DK_A // 2
    inv = 1.0 / (ROPE_BASE ** jnp.linspace(0.0, 1.0, half, dtype=F32))
    ang = pos.astype(F32)[:, None] * inv[None, :]
    return jnp.cos(ang), jnp.sin(ang)


def _retention_tables(chunk):
    log_g = jnp.log1p(-(2.0 ** (-5.0 - jnp.arange(H_A, dtype=F32))))
    i = jnp.arange(chunk, dtype=F32)
    diff = i[:, None] - i[None, :]
    decay = jnp.where(diff >= 0, jnp.exp(jnp.maximum(diff, 0.0)[None] * log_g[:, None, None]), 0.0)
    cross_scale = jnp.exp((i + 1.0)[:, None] * log_g[None, :]).T[:, :, None]
    wk = jnp.exp((chunk - 1.0 - i)[:, None] * log_g[None, :]).T[:, :, None]
    state_decay = jnp.exp(chunk * log_g)
    return decay, cross_scale, wk, state_decay


RET_HEADS_PER_STEP = 4
RET_CHUNKS_PER_STEP = 2


def _retention_kernel(sdec_ref, q_ref, k_ref, v_ref, g_ref, cos_ref, sin_ref, dec_ref, cs_ref,
                      wk_ref, s0_ref, a_ref, sout_ref, s_scr, *, chunk, heads, chunks, n_steps):
    hg = pl.program_id(1)
    c = pl.program_id(2)

    @pl.when(c == 0)
    def _():
        s_scr[...] = s0_ref[...]

    half = DK_A // 2

    def rot(x, cos, sin):
        x1 = x[:, :half]
        x2 = x[:, half:]
        return jnp.concatenate([x1 * cos - x2 * sin, x1 * sin + x2 * cos], axis=-1)

    for hh in range(heads):
        s = s_scr[hh]
        kcols = slice(hh * DK_A, (hh + 1) * DK_A)
        vcols = slice(hh * DV_A, (hh + 1) * DV_A)
        for ci in range(chunks):
            rows = slice(ci * chunk, (ci + 1) * chunk)
            cos = cos_ref[rows, :]
            sin = sin_ref[rows, :]
            q = rot(q_ref[rows, kcols], cos, sin)
            k = rot(k_ref[rows, kcols], cos, sin) * (DK_A ** -0.5)
            vb = v_ref[rows, vcols].astype(BF16)
            qb = q.astype(BF16)
            kb = k.astype(BF16)
            scores = lax.dot_general(qb, kb, (((1,), (1,)), ((), ())),
                                     preferred_element_type=F32) * dec_ref[hh]
            inner = jnp.dot(scores.astype(BF16), vb, preferred_element_type=F32)
            cross = jnp.dot(qb, s.astype(BF16), preferred_element_type=F32) * cs_ref[hh]
            kw = (k * wk_ref[hh]).astype(BF16)
            upd = lax.dot_general(kw, vb, (((0,), (0,)), ((), ())), preferred_element_type=F32)
            s = sdec_ref[hg * heads + hh] * s + upd

            o = inner + cross
            o = o * lax.rsqrt(jnp.mean(o * o, axis=-1, keepdims=True) + EPS)
            g = g_ref[rows, vcols]
            a_ref[rows, vcols] = (g * jax.nn.sigmoid(g) * o).astype(a_ref.dtype)
        s_scr[hh] = s

        @pl.when(c == n_steps - 1)
        def _():
            sout_ref[hh] = s


def _retention(qkvg, s0, pos, out_dtype):
    b, t, _ = qkvg.shape
    chunk = math.gcd(t, RET_CHUNK)
    n_chunks = t // chunk
    heads = RET_HEADS_PER_STEP
    chunks = math.gcd(n_chunks, RET_CHUNKS_PER_STEP)
    n_steps = n_chunks // chunks
    rows = chunk * chunks
    assert H_A % heads == 0
    cos, sin = _rope_tables(pos)
    decay, cross_scale, wk, state_decay = _retention_tables(chunk)
    k_off = QK_A // (heads * DK_A)
    v_off = 2 * QK_A // (heads * DV_A)
    g_off = (2 * QK_A + V_A) // (heads * DV_A)
    kernel = functools.partial(_retention_kernel, chunk=chunk, heads=heads, chunks=chunks, n_steps=n_steps)
    return pl.pallas_call(
        kernel,
        grid=(b, H_A // heads, n_steps),
        in_specs=[
            pl.BlockSpec(memory_space=pltpu.SMEM),
            pl.BlockSpec((None, rows, heads * DK_A), lambda bi, h, c: (bi, c, h)),
            pl.BlockSpec((None, rows, heads * DK_A), lambda bi, h, c: (bi, c, k_off + h)),
            pl.BlockSpec((None, rows, heads * DV_A), lambda bi, h, c: (bi, c, v_off + h)),
            pl.BlockSpec((None, rows, heads * DV_A), lambda bi, h, c: (bi, c, g_off + h)),
            pl.BlockSpec((rows, DK_A // 2), lambda bi, h, c: (c, 0)),
            pl.BlockSpec((rows, DK_A // 2), lambda bi, h, c: (c, 0)),
            pl.BlockSpec((heads, chunk, chunk), lambda bi, h, c: (h, 0, 0)),
            pl.BlockSpec((heads, chunk, 1), lambda bi, h, c: (h, 0, 0)),
            pl.BlockSpec((heads, chunk, 1), lambda bi, h, c: (h, 0, 0)),
            pl.BlockSpec((None, heads, DK_A, DV_A), lambda bi, h, c: (bi, h, 0, 0)),
        ],
        out_specs=[
            pl.BlockSpec((None, rows, heads * DV_A), lambda bi, h, c: (bi, c, h)),
            pl.BlockSpec((None, heads, DK_A, DV_A), lambda bi, h, c: (bi, h, 0, 0)),
        ],
        out_shape=[
            jax.ShapeDtypeStruct((b, t, V_A), out_dtype),
            jax.ShapeDtypeStruct((b, H_A, DK_A, DV_A), F32),
        ],
        scratch_shapes=[pltpu.VMEM((heads, DK_A, DV_A), F32)],
        compiler_params=_params("parallel", "parallel", "arbitrary"),
        name="retention",
    )(state_decay, qkvg, qkvg, qkvg, qkvg, cos, sin, decay, cross_scale, wk, s0)


def _select_topk(gate, n_sel, axis=1):
    pos = lax.broadcasted_iota(jnp.int32, gate.shape, axis).astype(F32)
    picks = []
    for _ in range(n_sel):
        m = jnp.max(gate, axis=axis, keepdims=True)
        idx = jnp.min(jnp.where(gate == m, pos, float(gate.shape[axis])), axis=axis, keepdims=True)
        valid = m > -jnp.inf
        picks.append((idx, valid))
        gate = jnp.where((pos == idx) & valid, -jnp.inf, gate)
    return picks


def _moba_prompt_kernel(q_ref, g_ref, k_ref, vt_ref, a_ref, *, n_blocks):
    bs = MOBA_BLOCK
    scale = DH_B ** -0.5
    n_sel = min(MOBA_TOPK, n_blocks)
    kb = k_ref[...]
    ones_rows = 16
    vt = jnp.concatenate([vt_ref[...], jnp.ones((ones_rows, kb.shape[0]), BF16)], axis=0)
    means_b = jnp.mean(kb.astype(F32).reshape(n_blocks, bs, DH_B), axis=1).astype(BF16)
    key = lax.broadcasted_iota(jnp.int32, (bs, bs), 0)
    qry = lax.broadcasted_iota(jnp.int32, (bs, bs), 1)
    causal = key <= qry

    for i in range(n_blocks):
        rows = slice(i * bs, (i + 1) * bs)
        q = q_ref[rows, :]
        qs = (q * (scale * LOG2_E)).astype(BF16)
        picks = None
        if i > n_sel:
            gate = lax.dot_general(means_b, q.astype(BF16), (((1,), (1,)), ((), ())),
                                   preferred_element_type=F32)
            blk = lax.broadcasted_iota(jnp.int32, gate.shape, 0)
            picks = _select_topk(jnp.where(blk < i, gate, -jnp.inf), n_sel, axis=0)
        blocks = []
        for j in range(i + 1):
            sj = lax.dot_general(kb[j * bs:(j + 1) * bs, :], qs, (((1,), (1,)), ((), ())),
                                 preferred_element_type=F32)
            if j == i:
                sj = jnp.where(causal, sj, -jnp.inf)
            elif picks is not None:
                chosen = picks[0][0] == float(j)
                for idx, _ in picks[1:]:
                    chosen = chosen | (idx == float(j))
                sj = jnp.where(chosen, sj, -jnp.inf)
            blocks.append(sj)
        m = jnp.max(functools.reduce(jnp.maximum, blocks), axis=0, keepdims=True)
        acc = jnp.zeros((DH_B + ones_rows, bs), F32)
        for j, sj in enumerate(blocks):
            p = jnp.exp2(sj - m).astype(BF16)
            acc = acc + jnp.dot(vt[:, j * bs:(j + 1) * bs], p, preferred_element_type=F32)
        o = (acc[:DH_B, :] / acc[DH_B:DH_B + 1, :]).T
        g = g_ref[rows, :]
        a_ref[rows, :] = (g * jax.nn.sigmoid(g) * o).astype(a_ref.dtype)


def _moba_prompt(qg, k_hm, vt_hm, out_dtype):
    b, t, _ = qg.shape
    n_blocks = t // MOBA_BLOCK
    assert n_blocks * MOBA_BLOCK == t and n_blocks <= 128
    kernel = functools.partial(_moba_prompt_kernel, n_blocks=n_blocks)
    return pl.pallas_call(
        kernel,
        grid=(b, H_B),
        in_specs=[
            pl.BlockSpec((None, t, DH_B), lambda bi, h: (bi, 0, h)),
            pl.BlockSpec((None, t, DH_B), lambda bi, h: (bi, 0, H_B + h)),
            pl.BlockSpec((None, None, t, DH_B), lambda bi, h: (bi, h, 0, 0)),
            pl.BlockSpec((None, None, DH_B, t), lambda bi, h: (bi, h, 0, 0)),
        ],
        out_specs=pl.BlockSpec((None, t, DH_B), lambda bi, h: (bi, 0, h)),
        out_shape=jax.ShapeDtypeStruct((b, t, D_MODEL), out_dtype),
        compiler_params=_params("parallel", "parallel"),
        name="moba_prompt",
    )(qg, qg, k_hm, vt_hm)


HEAD_GROUP = 8


GATHER_PAGES_PER_STEP = 8


def _gather_cache_kernel(pt_ref, *refs, page, n_groups, pages_per_step, pages_per_block, with_means):
    n_in = pages_per_step * n_groups
    o_ref = refs[n_in]
    for pg in range(pages_per_step):
        for grp in range(n_groups):
            src = refs[pg * n_groups + grp].reshape(page * HEAD_GROUP, DH_B)
            for hg in range(HEAD_GROUP):
                rows = src[pl.ds(hg, page, stride=HEAD_GROUP), :]
                o_ref[grp * HEAD_GROUP + hg, pg * page:(pg + 1) * page, :] = rows.astype(BF16)
    if with_means:
        m_ref = refs[n_in + 1]
        for blk in range(pages_per_step // pages_per_block):
            for grp in range(n_groups):
                total = None
                for pg in range(blk * pages_per_block, (blk + 1) * pages_per_block):
                    part = jnp.sum(refs[pg * n_groups + grp][...], axis=0)
                    total = part if total is None else total + part
                m_ref[blk, grp * HEAD_GROUP:(grp + 1) * HEAD_GROUP, :] = total / MOBA_BLOCK


def _gather_cache(cache, page_table_flat, n_seq, n_pages, with_means):
    n_pool, page, n_h, dh = cache.shape
    pages_per_block = MOBA_BLOCK // page
    pps = GATHER_PAGES_PER_STEP
    assert n_h == H_B and dh == DH_B and n_h % HEAD_GROUP == 0
    assert pps % pages_per_block == 0 and n_pages % pps == 0
    n_blocks = n_pages // pages_per_block
    n_groups = n_h // HEAD_GROUP
    grouped = cache.reshape(n_pool, page, n_groups, HEAD_GROUP, dh)

    def page_map(pg, grp):
        return lambda bi, j, pt: (pt[bi * n_pages + pps * j + pg], 0, grp, 0, 0)

    in_specs = [pl.BlockSpec((None, page, None, HEAD_GROUP, dh), page_map(pg, grp))
                for pg in range(pps) for grp in range(n_groups)]
    out_specs = [pl.BlockSpec((None, n_h, pps * page, dh), lambda bi, j, pt: (bi, 0, j, 0))]
    out_shape = [jax.ShapeDtypeStruct((n_seq, n_h, n_pages * page, dh), BF16)]
    if with_means:
        blocks_per_step = pps // pages_per_block
        out_specs.append(pl.BlockSpec((None, blocks_per_step, n_h, dh), lambda bi, j, pt: (bi, j, 0, 0)))
        out_shape.append(jax.ShapeDtypeStruct((n_seq, n_blocks, n_h, dh), F32))
    kernel = functools.partial(_gather_cache_kernel, page=page, n_groups=n_groups, pages_per_step=pps,
                               pages_per_block=pages_per_block, with_means=with_means)
    return pl.pallas_call(
        kernel,
        grid_spec=pltpu.PrefetchScalarGridSpec(
            num_scalar_prefetch=1,
            grid=(n_seq, n_pages // pps),
            in_specs=in_specs,
            out_specs=out_specs,
        ),
        out_shape=out_shape,
        compiler_params=_params("parallel", "arbitrary"),
        name="gather_cache",
    )(page_table_flat, *([grouped] * len(in_specs)))


def _moba_select_kernel(q_ref, means_ref, o_ref, *, t, n_blocks, past):
    q = q_ref[...]
    rows = t * H_B
    qrep = jnp.concatenate([jnp.broadcast_to(q[ti:ti + 1, :], (H_B, D_MODEL)) for ti in range(t)], axis=0)
    lane_h = lax.broadcasted_iota(jnp.int32, (rows, D_MODEL), 1) >> (DH_B.bit_length() - 1)
    row_h = lax.broadcasted_iota(jnp.int32, (rows, D_MODEL), 0) & (H_B - 1)
    qexp = jnp.where(lane_h == row_h, qrep, 0.0)
    means = jnp.concatenate([means_ref[...], jnp.zeros((128 - n_blocks, D_MODEL), F32)], axis=0)
    gate = lax.dot_general(qexp.astype(BF16), means.astype(BF16), (((1,), (1,)), ((), ())),
                           preferred_element_type=F32)
    lane = lax.broadcasted_iota(jnp.int32, gate.shape, 1)
    token = lax.broadcasted_iota(jnp.int32, gate.shape, 0) >> (H_B.bit_length() - 1)
    own = (past + token) >> (MOBA_BLOCK.bit_length() - 1)
    gate = jnp.where(lane < own, gate, -jnp.inf)
    out = jnp.zeros(gate.shape, F32)
    for r, (idx, _) in enumerate(_select_topk(gate, MOBA_TOPK)):
        out = jnp.where(lane == r, idx, out)
    o_ref[...] = out.astype(jnp.int32)


def _moba_select(qg, means, past):
    b, t, _ = qg.shape
    n_blocks = means.shape[1]
    assert MOBA_TOPK <= n_blocks <= 128
    kernel = functools.partial(_moba_select_kernel, t=t, n_blocks=n_blocks, past=past)
    return pl.pallas_call(
        kernel,
        grid=(b,),
        in_specs=[
            pl.BlockSpec((None, t, D_MODEL), lambda bi: (bi, 0, 0)),
            pl.BlockSpec((None, n_blocks, D_MODEL), lambda bi: (bi, 0, 0)),
        ],
        out_specs=pl.BlockSpec((None, t * H_B, 128), lambda bi: (bi, 0, 0)),
        out_shape=jax.ShapeDtypeStruct((b, t * H_B, 128), jnp.int32),
        compiler_params=_params("parallel"),
        name="moba_select",
    )(qg, means)


def _moba_sample_kernel(sel_ref, q_ref, g_ref, kn_ref, vn_ref, *refs, t, n_slots):
    k_refs = refs[:n_slots]
    v_refs = refs[n_slots:2 * n_slots]
    a_ref = refs[2 * n_slots]
    keys_per_token = (n_slots // t) * k_refs[0].shape[0]

    q = q_ref[...]
    qb = q.astype(BF16)
    kall = jnp.concatenate([r[...] for r in k_refs], axis=0)
    vall = jnp.concatenate([r[...] for r in v_refs], axis=0)
    s = lax.dot_general(qb, kall, (((1,), (1,)), ((), ())), preferred_element_type=F32) * (DH_B ** -0.5)
    row = lax.broadcasted_iota(jnp.int32, s.shape, 0)
    col = lax.broadcasted_iota(jnp.int32, s.shape, 1)
    mine = (col >= row * keys_per_token) & (col < (row + 1) * keys_per_token)
    s = jnp.where(mine, s, -jnp.inf)

    kn = kn_ref[...]
    vn = vn_ref[...]
    rown = lax.broadcasted_iota(jnp.int32, (t, 1), 0)
    s_own = []
    for tj in range(t):
        sj = jnp.sum(q * kn[tj:tj + 1, :], axis=-1, keepdims=True) * (DH_B ** -0.5)
        s_own.append(jnp.where(rown >= tj, sj, -jnp.inf))
    m = jnp.max(s, axis=-1, keepdims=True)
    for sj in s_own:
        m = jnp.maximum(m, sj)
    p = jnp.exp(s - m)
    l = jnp.sum(p, axis=-1, keepdims=True)
    acc = jnp.dot(p.astype(BF16), vall, preferred_element_type=F32)
    for tj, sj in enumerate(s_own):
        pj = jnp.exp(sj - m)
        l = l + pj
        acc = acc + pj * vn[tj:tj + 1, :]
    o = acc / l
    g = g_ref[...]
    a_ref[...] = (g * jax.nn.sigmoid(g) * o).astype(a_ref.dtype)


def _moba_sample(qg, kv_new, past_k, past_v, sel_flat):
    b, t, _ = qg.shape
    n_slots = t * MOBA_TOPK

    def slot_map(ti, r):
        def index_map(bi, h, sel):
            return (bi, h, sel[((bi * t + ti) * H_B + h) * MOBA_TOPK + r], 0)
        return index_map

    def slot_specs():
        return [pl.BlockSpec((None, None, MOBA_BLOCK, DH_B), slot_map(ti, r))
                for ti in range(t) for r in range(MOBA_TOPK)]

    kernel = functools.partial(_moba_sample_kernel, t=t, n_slots=n_slots)
    return pl.pallas_call(
        kernel,
        grid_spec=pltpu.PrefetchScalarGridSpec(
            num_scalar_prefetch=1,
            grid=(b, H_B),
            in_specs=[
                pl.BlockSpec((None, t, DH_B), lambda bi, h, sel: (bi, 0, h)),
                pl.BlockSpec((None, t, DH_B), lambda bi, h, sel: (bi, 0, H_B + h)),
                pl.BlockSpec((None, t, DH_B), lambda bi, h, sel: (bi, 0, h)),
                pl.BlockSpec((None, t, DH_B), lambda bi, h, sel: (bi, 0, H_B + h)),
            ] + slot_specs() + slot_specs(),
            out_specs=pl.BlockSpec((None, t, DH_B), lambda bi, h, sel: (bi, 0, h)),
        ),
        out_shape=jax.ShapeDtypeStruct((b, t, D_MODEL), F32),
        compiler_params=_params("parallel", "arbitrary"),
        name="moba_sample",
    )(sel_flat, qg, qg, kv_new, kv_new, *([past_k] * n_slots), *([past_v] * n_slots))


def _run_group(x, mods, mods_kv, mods_f, s_in, past, weights, paged):
    (norm_g, w_in_a, w_out_a, w_q_b, w_o_b, kv_norm_g, w_kv, final_g) = weights
    b, t, d = x.shape
    rows = b * t
    prompt = paged is None
    pos = past + jnp.arange(t, dtype=jnp.int32)
    if prompt:
        tm, tn, tm_norm, rpb = 1024, 1024, 512, t
        act_dtype = BF16
        expand = lambda v: v[:, None, :]
    else:
        tm, tn, tm_norm, rpb = rows, 1024, rows, 0
        act_dtype = F32
        expand = lambda v: jnp.repeat(v, t, axis=0)

    def project(x2, g, shift, scale, w):
        if prompt:
            return _norm_matmul(x2, g[None, :], shift, scale, w, rpb, tm, tn)
        h = _norm_mod(x2, g[None, :], shift, scale, rpb, act_dtype, tm_norm)
        return _matmul(h, w, tm, tn)

    x2 = x.reshape(rows, d)
    new_s = []
    kv = k_new = v_new = k_hm = vt_hm = None
    for l in range(DEPTH):
        shift, scale, gate = (expand(mods[l][:, i * d:(i + 1) * d]) for i in range(3))
        if l < N_A:
            qkvg = project(x2, norm_g[l], shift, scale, w_in_a[l]).reshape(b, t, -1)
            a, s = _retention(qkvg, s_in[l], pos, act_dtype)
            new_s.append(s)
            x2 = _matmul(a.reshape(rows, V_A), w_out_a[l], tm, tn, res=x2, gate=gate, rows_per_batch=rpb)
        else:
            lb = l - N_A
            qg = project(x2, norm_g[l], shift, scale, w_q_b[lb]).reshape(b, t, -1)
            if prompt:
                a = _moba_prompt(qg, k_hm, vt_hm, act_dtype)
            else:
                past_k, past_v, means = paged
                sel = _moba_select(qg, means, past)[:, :, :MOBA_TOPK].reshape(-1)
                a = _moba_sample(qg, kv.reshape(b, t, -1), past_k, past_v, sel)
            x2 = _matmul(a.reshape(rows, d), w_o_b[lb], tm, tn, res=x2, gate=gate, rows_per_batch=rpb)
        if l == N_A - 1:
            kv_shift, kv_scale = (expand(mods_kv[:, i * d:(i + 1) * d]) for i in range(2))
            kv = project(x2, kv_norm_g, kv_shift, kv_scale, w_kv)
            if prompt:
                k_new, v_new, k_hm, vt_hm = _split_heads(kv, b, t)
            else:
                k_new = kv[:, :d].reshape(b, t, H_B, DH_B)
                v_new = kv[:, d:].reshape(b, t, H_B, DH_B)
    f_shift, f_scale = (expand(mods_f[:, i * d:(i + 1) * d]) for i in range(2))
    y = _norm_mod(x2, final_g[None, :], f_shift, f_scale, rpb, F32, tm_norm)
    return y.reshape(b, t, d), jnp.stack(new_s), k_new, v_new


def kernel(x_prompt, x_sample, state_ret, cache_k, cache_v, page_table, c_prompt, c_sample,
           norm_g, w_mod, b_mod, w_in_a, w_out_a, w_q_b, w_o_b,
           kv_norm_g, w_mod_kv, b_mod_kv, w_kv, final_g, w_mod_f, b_mod_f):
    bp = x_prompt.shape[0]
    bd, n_pages = page_table.shape
    n_pool, page, _, _ = cache_k.shape
    past_len = n_pages * page
    assert bp + bd <= MOD_ROWS

    c_all = jnp.concatenate([c_prompt, c_sample, jnp.zeros((MOD_ROWS - bp - bd, D_MODEL), F32)], axis=0)
    mods = _mod_matmul(c_all, w_mod, b_mod[:, None, :])
    mods_kv = _mod_matmul(c_all, w_mod_kv[None], b_mod_kv[None, None, :])[0]
    mods_f = _mod_matmul(c_all, w_mod_f[None], b_mod_f[None, None, :])[0]

    weights = (norm_g, w_in_a, w_out_a, w_q_b, w_o_b, kv_norm_g, w_kv, final_g)

    s0_prompt = jnp.zeros((N_A, bp, H_A, DK_A, DV_A), state_ret.dtype)
    y_p, s_p, k_p, v_p = _run_group(
        x_prompt, mods[:, :bp], mods_kv[:bp], mods_f[:bp], s0_prompt, 0, weights, None)

    pt_flat = page_table.reshape(-1)
    past_k, means = _gather_cache(cache_k, pt_flat, bd, n_pages, with_means=True)
    (past_v,) = _gather_cache(cache_v, pt_flat, bd, n_pages, with_means=False)
    means = means.reshape(bd, -1, H_B * DH_B)
    y_s, s_s, k_s, v_s = _run_group(
        x_sample, mods[:, bp:bp + bd], mods_kv[bp:bp + bd], mods_f[bp:bp + bd], state_ret, past_len,
        weights, (past_k, past_v, means))
    return (y_p, y_s, s_p, s_s, k_p, v_p, k_s, v_s)
```

```python
import functools
import math

import jax
import jax.numpy as jnp
from jax import lax
from jax.experimental import pallas as pl
from jax.experimental.pallas import tpu as pltpu

F32 = jnp.float32
BF16 = jnp.bfloat16

D_MODEL = 2048
DEPTH = 4
N_A = DEPTH // 2
H_A = 8
DK_A = D_MODEL // H_A
DV_A = 2 * DK_A
QK_A = H_A * DK_A
V_A = H_A * DV_A
RET_CHUNK = 128
H_B = 16
DH_B = D_MODEL // H_B
MOBA_BLOCK = 256
MOBA_TOPK = 3
ROPE_BASE = 10000.0
EPS = 1e-6
LOG2_E = math.log2(math.e)

VMEM_LIMIT_BYTES = 56 * 1024 * 1024
MOD_ROWS = 16


def _params(*sem):
    return pltpu.CompilerParams(dimension_semantics=sem, vmem_limit_bytes=VMEM_LIMIT_BYTES)


def _mod_kernel(c_ref, w_ref, b_ref, o_ref):
    acc = jnp.dot(c_ref[...].astype(BF16), w_ref[...].astype(BF16), preferred_element_type=F32)
    o_ref[...] = acc + b_ref[...]


def _mod_matmul(c, w, b, tn=1024):
    n_l, d, n = w.shape
    return pl.pallas_call(
        _mod_kernel,
        grid=(n_l, n // tn),
        in_specs=[
            pl.BlockSpec((MOD_ROWS, d), lambda l, j: (0, 0)),
            pl.BlockSpec((None, d, tn), lambda l, j: (l, 0, j)),
            pl.BlockSpec((None, 1, tn), lambda l, j: (l, 0, j)),
        ],
        out_specs=pl.BlockSpec((None, MOD_ROWS, tn), lambda l, j: (l, 0, j)),
        out_shape=jax.ShapeDtypeStruct((n_l, MOD_ROWS, n), F32),
        compiler_params=_params("parallel", "parallel"),
        name="mod_matmul",
    )(c, w, b)


def _norm_mod_kernel(x_ref, g_ref, sh_ref, sc_ref, o_ref):
    x = x_ref[...]
    r = lax.rsqrt(jnp.mean(x * x, axis=-1, keepdims=True) + EPS)
    y = (x * r) * g_ref[...]
    o_ref[...] = (y * (1.0 + sc_ref[...]) + sh_ref[...]).astype(o_ref.dtype)


def _norm_mod(x, g, shift, scale, rows_per_batch, out_dtype, tm):
    r, d = x.shape
    if rows_per_batch:
        tiles_per_b = rows_per_batch // tm
        mod_spec = pl.BlockSpec((None, 1, d), lambda i: (i // tiles_per_b, 0, 0))
    else:
        mod_spec = pl.BlockSpec((tm, d), lambda i: (i, 0))
    return pl.pallas_call(
        _norm_mod_kernel,
        grid=(r // tm,),
        in_specs=[
            pl.BlockSpec((tm, d), lambda i: (i, 0)),
            pl.BlockSpec((1, d), lambda i: (0, 0)),
            mod_spec,
            mod_spec,
        ],
        out_specs=pl.BlockSpec((tm, d), lambda i: (i, 0)),
        out_shape=jax.ShapeDtypeStruct((r, d), out_dtype),
        compiler_params=_params("parallel"),
        name="norm_mod",
    )(x, g, shift, scale)


def _matmul_kernel(a_ref, w_ref, o_ref):
    o_ref[...] = jnp.dot(
        a_ref[...].astype(BF16), w_ref[...].astype(BF16), preferred_element_type=F32
    ).astype(o_ref.dtype)


def _matmul_res_kernel(a_ref, w_ref, x_ref, gm_ref, o_ref):
    acc = jnp.dot(a_ref[...].astype(BF16), w_ref[...].astype(BF16), preferred_element_type=F32)
    o_ref[...] = x_ref[...] + gm_ref[...] * acc


MATMUL_W_TILE_ELEMS = 2 * 1024 * 1024


def _matmul(a, w, layer, tm, tn, res=None, gate=None, rows_per_batch=0):
    r, k = a.shape
    n = w.shape[2]
    tn = min(tn, MATMUL_W_TILE_ELEMS // k)
    in_specs = [
        pl.BlockSpec((tm, k), lambda i, j: (i, 0)),
        pl.BlockSpec((None, k, tn), lambda i, j: (layer, 0, j)),
    ]
    args = [a, w]
    kernel = _matmul_kernel
    if res is not None:
        kernel = _matmul_res_kernel
        in_specs.append(pl.BlockSpec((tm, tn), lambda i, j: (i, j)))
        if rows_per_batch:
            tiles_per_b = rows_per_batch // tm
            in_specs.append(pl.BlockSpec((None, 1, tn), lambda i, j: (i // tiles_per_b, 0, j)))
        else:
            in_specs.append(pl.BlockSpec((tm, tn), lambda i, j: (i, j)))
        args += [res, gate]
    return pl.pallas_call(
        kernel,
        grid=(r // tm, n // tn),
        in_specs=in_specs,
        out_specs=pl.BlockSpec((tm, tn), lambda i, j: (i, j)),
        out_shape=jax.ShapeDtypeStruct((r, n), F32),
        compiler_params=_params("parallel", "parallel"),
        name="matmul_res" if res is not None else "matmul",
    )(*args)


def _norm_matmul_kernel(x_ref, g_ref, sh_ref, sc_ref, w_ref, o_ref, h_scr):
    @pl.when(pl.program_id(1) == 0)
    def _():
        x = x_ref[...]
        r = lax.rsqrt(jnp.mean(x * x, axis=-1, keepdims=True) + EPS)
        y = (x * r) * g_ref[...]
        h_scr[...] = (y * (1.0 + sc_ref[...]) + sh_ref[...]).astype(BF16)

    o_ref[...] = jnp.dot(h_scr[...], w_ref[...].astype(BF16), preferred_element_type=F32)


def _norm_matmul(x, g, shift, scale, w, layer, rows_per_batch, tm, tn):
    r, k = x.shape
    n = w.shape[2]
    tn = min(tn, MATMUL_W_TILE_ELEMS // k)
    tiles_per_b = rows_per_batch // tm
    mod_spec = pl.BlockSpec((None, 1, k), lambda i, j: (i // tiles_per_b, 0, 0))
    return pl.pallas_call(
        _norm_matmul_kernel,
        grid=(r // tm, n // tn),
        in_specs=[
            pl.BlockSpec((tm, k), lambda i, j: (i, 0)),
            pl.BlockSpec((1, k), lambda i, j: (0, 0)),
            mod_spec,
            mod_spec,
            pl.BlockSpec((None, k, tn), lambda i, j: (layer, 0, j)),
        ],
        out_specs=pl.BlockSpec((tm, tn), lambda i, j: (i, j)),
        out_shape=jax.ShapeDtypeStruct((r, n), F32),
        scratch_shapes=[pltpu.VMEM((tm, k), BF16)],
        compiler_params=_params("parallel", "arbitrary"),
        name="norm_matmul",
    )(x, g, shift, scale, w)


def _split_heads_kernel(kv_ref, k_out_ref, v_out_ref, k_hm_ref, vt_hm_ref, *, tm):
    for h in range(H_B):
        kh = kv_ref[:, h * DH_B:(h + 1) * DH_B]
        vh = kv_ref[:, (H_B + h) * DH_B:(H_B + h + 1) * DH_B]
        k_out_ref[pl.ds(h, tm, stride=H_B), :] = kh
        v_out_ref[pl.ds(h, tm, stride=H_B), :] = vh
        k_hm_ref[h] = kh.astype(BF16)
        vt_hm_ref[h] = vh.T.astype(BF16)


def _split_heads(kv, b, t, tm=256):
    r = b * t
    tiles_per_b = t // tm
    kernel = functools.partial(_split_heads_kernel, tm=tm)
    rows_spec = pl.BlockSpec((tm * H_B, DH_B), lambda i: (i, 0))
    k_out, v_out, k_hm, vt_hm = pl.pallas_call(
        kernel,
        grid=(r // tm,),
        in_specs=[pl.BlockSpec((tm, 2 * D_MODEL), lambda i: (i, 0))],
        out_specs=[
            rows_spec,
            rows_spec,
            pl.BlockSpec((None, H_B, tm, DH_B), lambda i: (i // tiles_per_b, 0, i % tiles_per_b, 0)),
            pl.BlockSpec((None, H_B, DH_B, tm), lambda i: (i // tiles_per_b, 0, 0, i % tiles_per_b)),
        ],
        out_shape=[
            jax.ShapeDtypeStruct((r * H_B, DH_B), F32),
            jax.ShapeDtypeStruct((r * H_B, DH_B), F32),
            jax.ShapeDtypeStruct((b, H_B, t, DH_B), BF16),
            jax.ShapeDtypeStruct((b, H_B, DH_B, t), BF16),
        ],
        compiler_params=_params("parallel"),
        name="split_heads",
    )(kv)
    return k_out.reshape(b, t, H_B, DH_B), v_out.reshape(b, t, H_B, DH_B), k_hm, vt_hm


def _rope_tables(pos):
    half = DK_A // 2
    inv = 1.0 / (ROPE_BASE ** jnp.linspace(0.0, 1.0, half, dtype=F32))
    ang = pos.astype(F32)[:, None] * inv[None, :]
    return jnp.cos(ang), jnp.sin(ang)


def _retention_tables(chunk):
    log_g = jnp.log1p(-(2.0 ** (-5.0 - jnp.arange(H_A, dtype=F32))))
    i = jnp.arange(chunk, dtype=F32)
    diff = i[:, None] - i[None, :]
    decay = jnp.where(diff >= 0, jnp.exp(jnp.maximum(diff, 0.0)[None] * log_g[:, None, None]), 0.0)
    cross_scale = jnp.exp((i + 1.0)[:, None] * log_g[None, :]).T[:, :, None]
    wk = jnp.exp((chunk - 1.0 - i)[:, None] * log_g[None, :]).T[:, :, None]
    state_decay = jnp.exp(chunk * log_g)
    return decay, cross_scale, wk, state_decay


RET_HEADS_PER_STEP = 4
RET_CHUNKS_PER_STEP = 2


def _retention_kernel(sdec_ref, q_ref, k_ref, v_ref, g_ref, cos_ref, sin_ref, dec_ref, cs_ref,
                      wk_ref, *refs, chunk, heads, chunks, n_steps, zero_state):
    a_ref, sout_ref, s_scr = refs[-3:]
    hg = pl.program_id(1)
    c = pl.program_id(2)

    @pl.when(c == 0)
    def _():
        s_scr[...] = jnp.zeros(s_scr.shape, F32) if zero_state else refs[0][...]

    half = DK_A // 2

    def rot(x, cos, sin):
        x1 = x[:, :half]
        x2 = x[:, half:]
        return jnp.concatenate([x1 * cos - x2 * sin, x1 * sin + x2 * cos], axis=-1)

    for hh in range(heads):
        s = s_scr[hh]
        kcols = slice(hh * DK_A, (hh + 1) * DK_A)
        vcols = slice(hh * DV_A, (hh + 1) * DV_A)
        for ci in range(chunks):
            rows = slice(ci * chunk, (ci + 1) * chunk)
            cos = cos_ref[rows, :]
            sin = sin_ref[rows, :]
            q = rot(q_ref[rows, kcols], cos, sin)
            k = rot(k_ref[rows, kcols], cos, sin) * (DK_A ** -0.5)
            vb = v_ref[rows, vcols].astype(BF16)
            qb = q.astype(BF16)
            kb = k.astype(BF16)
            scores = lax.dot_general(qb, kb, (((1,), (1,)), ((), ())),
                                     preferred_element_type=F32) * dec_ref[hh]
            inner = jnp.dot(scores.astype(BF16), vb, preferred_element_type=F32)
            cross = jnp.dot(qb, s.astype(BF16), preferred_element_type=F32) * cs_ref[hh]
            kw = (k * wk_ref[hh]).astype(BF16)
            upd = lax.dot_general(kw, vb, (((0,), (0,)), ((), ())), preferred_element_type=F32)
            s = sdec_ref[hg * heads + hh] * s + upd

            o = inner + cross
            o = o * lax.rsqrt(jnp.mean(o * o, axis=-1, keepdims=True) + EPS)
            g = g_ref[rows, vcols]
            a_ref[rows, vcols] = (g * jax.nn.sigmoid(g) * o).astype(a_ref.dtype)
        s_scr[hh] = s

        @pl.when(c == n_steps - 1)
        def _():
            sout_ref[hh] = s


def _retention(qkvg, s_in, layer, pos, out_dtype):
    b, t, _ = qkvg.shape
    chunk = math.gcd(t, RET_CHUNK)
    n_chunks = t // chunk
    heads = RET_HEADS_PER_STEP
    chunks = math.gcd(n_chunks, RET_CHUNKS_PER_STEP)
    n_steps = n_chunks // chunks
    rows = chunk * chunks
    assert H_A % heads == 0
    cos, sin = _rope_tables(pos)
    decay, cross_scale, wk, state_decay = _retention_tables(chunk)
    k_off = QK_A // (heads * DK_A)
    v_off = 2 * QK_A // (heads * DV_A)
    g_off = (2 * QK_A + V_A) // (heads * DV_A)
    kernel = functools.partial(_retention_kernel, chunk=chunk, heads=heads, chunks=chunks, n_steps=n_steps,
                               zero_state=s_in is None)
    state_args, state_specs = [], []
    if s_in is not None:
        state_args = [s_in]
        state_specs = [pl.BlockSpec((None, None, heads, DK_A, DV_A), lambda bi, h, c: (layer, bi, h, 0, 0))]
    return pl.pallas_call(
        kernel,
        grid=(b, H_A // heads, n_steps),
        in_specs=[
            pl.BlockSpec(memory_space=pltpu.SMEM),
            pl.BlockSpec((None, rows, heads * DK_A), lambda bi, h, c: (bi, c, h)),
            pl.BlockSpec((None, rows, heads * DK_A), lambda bi, h, c: (bi, c, k_off + h)),
            pl.BlockSpec((None, rows, heads * DV_A), lambda bi, h, c: (bi, c, v_off + h)),
            pl.BlockSpec((None, rows, heads * DV_A), lambda bi, h, c: (bi, c, g_off + h)),
            pl.BlockSpec((rows, DK_A // 2), lambda bi, h, c: (c, 0)),
            pl.BlockSpec((rows, DK_A // 2), lambda bi, h, c: (c, 0)),
            pl.BlockSpec((heads, chunk, chunk), lambda bi, h, c: (h, 0, 0)),
            pl.BlockSpec((heads, chunk, 1), lambda bi, h, c: (h, 0, 0)),
            pl.BlockSpec((heads, chunk, 1), lambda bi, h, c: (h, 0, 0)),
        ] + state_specs,
        out_specs=[
            pl.BlockSpec((None, rows, heads * DV_A), lambda bi, h, c: (bi, c, h)),
            pl.BlockSpec((None, heads, DK_A, DV_A), lambda bi, h, c: (bi, h, 0, 0)),
        ],
        out_shape=[
            jax.ShapeDtypeStruct((b, t, V_A), out_dtype),
            jax.ShapeDtypeStruct((b, H_A, DK_A, DV_A), F32),
        ],
        scratch_shapes=[pltpu.VMEM((heads, DK_A, DV_A), F32)],
        compiler_params=_params("parallel", "parallel", "arbitrary"),
        name="retention",
    )(state_decay, qkvg, qkvg, qkvg, qkvg, cos, sin, decay, cross_scale, wk, *state_args)


def _select_topk(gate, n_sel, axis=1):
    pos = lax.broadcasted_iota(jnp.int32, gate.shape, axis).astype(F32)
    picks = []
    for _ in range(n_sel):
        m = jnp.max(gate, axis=axis, keepdims=True)
        idx = jnp.min(jnp.where(gate == m, pos, float(gate.shape[axis])), axis=axis, keepdims=True)
        valid = m > -jnp.inf
        picks.append((idx, valid))
        gate = jnp.where((pos == idx) & valid, -jnp.inf, gate)
    return picks


def _moba_prompt_kernel(q_ref, g_ref, k_ref, vt_ref, a_ref, *, n_blocks):
    bs = MOBA_BLOCK
    scale = DH_B ** -0.5
    n_sel = min(MOBA_TOPK, n_blocks)
    kb = k_ref[...]
    ones_rows = 16
    vt = jnp.concatenate([vt_ref[...], jnp.ones((ones_rows, kb.shape[0]), BF16)], axis=0)
    means_b = jnp.mean(kb.astype(F32).reshape(n_blocks, bs, DH_B), axis=1).astype(BF16)
    key = lax.broadcasted_iota(jnp.int32, (bs, bs), 0)
    qry = lax.broadcasted_iota(jnp.int32, (bs, bs), 1)
    causal = key <= qry

    for i in range(n_blocks):
        rows = slice(i * bs, (i + 1) * bs)
        q = q_ref[rows, :]
        qs = (q * (scale * LOG2_E)).astype(BF16)
        picks = None
        if i > n_sel:
            gate = lax.dot_general(means_b, q.astype(BF16), (((1,), (1,)), ((), ())),
                                   preferred_element_type=F32)
            blk = lax.broadcasted_iota(jnp.int32, gate.shape, 0)
            picks = _select_topk(jnp.where(blk < i, gate, -jnp.inf), n_sel, axis=0)
        blocks = []
        for j in range(i + 1):
            sj = lax.dot_general(kb[j * bs:(j + 1) * bs, :], qs, (((1,), (1,)), ((), ())),
                                 preferred_element_type=F32)
            if j == i:
                sj = jnp.where(causal, sj, -jnp.inf)
            elif picks is not None:
                chosen = picks[0][0] == float(j)
                for idx, _ in picks[1:]:
                    chosen = chosen | (idx == float(j))
                sj = jnp.where(chosen, sj, -jnp.inf)
            blocks.append(sj)
        m = jnp.max(functools.reduce(jnp.maximum, blocks), axis=0, keepdims=True)
        acc = jnp.zeros((DH_B + ones_rows, bs), F32)
        for j, sj in enumerate(blocks):
            p = jnp.exp2(sj - m).astype(BF16)
            acc = acc + jnp.dot(vt[:, j * bs:(j + 1) * bs], p, preferred_element_type=F32)
        o = (acc[:DH_B, :] / acc[DH_B:DH_B + 1, :]).T
        g = g_ref[rows, :]
        a_ref[rows, :] = (g * jax.nn.sigmoid(g) * o).astype(a_ref.dtype)


def _moba_prompt(qg, k_hm, vt_hm, out_dtype):
    b, t, _ = qg.shape
    n_blocks = t // MOBA_BLOCK
    assert n_blocks * MOBA_BLOCK == t and n_blocks <= 128
    kernel = functools.partial(_moba_prompt_kernel, n_blocks=n_blocks)
    return pl.pallas_call(
        kernel,
        grid=(b, H_B),
        in_specs=[
            pl.BlockSpec((None, t, DH_B), lambda bi, h: (bi, 0, h)),
            pl.BlockSpec((None, t, DH_B), lambda bi, h: (bi, 0, H_B + h)),
            pl.BlockSpec((None, None, t, DH_B), lambda bi, h: (bi, h, 0, 0)),
            pl.BlockSpec((None, None, DH_B, t), lambda bi, h: (bi, h, 0, 0)),
        ],
        out_specs=pl.BlockSpec((None, t, DH_B), lambda bi, h: (bi, 0, h)),
        out_shape=jax.ShapeDtypeStruct((b, t, D_MODEL), out_dtype),
        compiler_params=_params("parallel", "parallel"),
        name="moba_prompt",
    )(qg, qg, k_hm, vt_hm)


HEAD_GROUP = 8


GATHER_PAGES_PER_STEP = 4


def _gather_cache_kernel(pt_ref, *refs, page, n_groups, pages_per_step, pages_per_block):
    n_slabs = pages_per_step * n_groups
    o_ref, m_ref = refs[2 * n_slabs], refs[2 * n_slabs + 1]
    for part in range(2):
        lanes = slice(part * DH_B, (part + 1) * DH_B)
        for pg in range(pages_per_step):
            for grp in range(n_groups):
                src = refs[part * n_slabs + pg * n_groups + grp]
                src = src.reshape(page * HEAD_GROUP, DH_B)
                for hg in range(HEAD_GROUP):
                    rows = src[pl.ds(hg, page, stride=HEAD_GROUP), :]
                    o_ref[grp * HEAD_GROUP + hg, pg * page:(pg + 1) * page, lanes] = rows.astype(BF16)
    for blk in range(pages_per_step // pages_per_block):
        for grp in range(n_groups):
            total = None
            for pg in range(blk * pages_per_block, (blk + 1) * pages_per_block):
                part_sum = jnp.sum(refs[pg * n_groups + grp][...], axis=0)
                total = part_sum if total is None else total + part_sum
            m_ref[blk, grp * HEAD_GROUP:(grp + 1) * HEAD_GROUP, :] = total / MOBA_BLOCK


def _gather_cache(cache_k, cache_v, page_table_flat, n_seq, n_pages):
    n_pool, page, n_h, dh = cache_k.shape
    pages_per_block = MOBA_BLOCK // page
    pps = GATHER_PAGES_PER_STEP
    assert n_h == H_B and dh == DH_B and n_h % HEAD_GROUP == 0
    assert pps % pages_per_block == 0 and n_pages % pps == 0
    n_blocks = n_pages // pages_per_block
    n_groups = n_h // HEAD_GROUP
    grouped = [c.reshape(n_pool, page, n_groups, HEAD_GROUP, dh) for c in (cache_k, cache_v)]

    def page_map(pg, grp):
        return lambda bi, j, pt: (pt[bi * n_pages + pps * j + pg], 0, grp, 0, 0)

    def slab_specs():
        return [pl.BlockSpec((None, page, None, HEAD_GROUP, dh), page_map(pg, grp))
                for pg in range(pps) for grp in range(n_groups)]

    n_slabs = pps * n_groups
    blocks_per_step = pps // pages_per_block
    kernel = functools.partial(_gather_cache_kernel, page=page, n_groups=n_groups, pages_per_step=pps,
                               pages_per_block=pages_per_block)
    return pl.pallas_call(
        kernel,
        grid_spec=pltpu.PrefetchScalarGridSpec(
            num_scalar_prefetch=1,
            grid=(n_seq, n_pages // pps),
            in_specs=slab_specs() + slab_specs(),
            out_specs=[
                pl.BlockSpec((None, n_h, pps * page, 2 * dh), lambda bi, j, pt: (bi, 0, j, 0)),
                pl.BlockSpec((None, blocks_per_step, n_h, dh), lambda bi, j, pt: (bi, j, 0, 0)),
            ],
        ),
        out_shape=[
            jax.ShapeDtypeStruct((n_seq, n_h, n_pages * page, 2 * dh), BF16),
            jax.ShapeDtypeStruct((n_seq, n_blocks, n_h, dh), F32),
        ],
        compiler_params=_params("parallel", "arbitrary"),
        name="gather_cache",
    )(page_table_flat, *([grouped[0]] * n_slabs), *([grouped[1]] * n_slabs))


def _moba_select_kernel(q_ref, means_ref, o_ref, *, t, n_blocks, past):
    q = q_ref[...]
    rows = t * H_B
    qrep = jnp.concatenate([jnp.broadcast_to(q[ti:ti + 1, :], (H_B, D_MODEL)) for ti in range(t)], axis=0)
    lane_h = lax.broadcasted_iota(jnp.int32, (rows, D_MODEL), 1) >> (DH_B.bit_length() - 1)
    row_h = lax.broadcasted_iota(jnp.int32, (rows, D_MODEL), 0) & (H_B - 1)
    qexp = jnp.where(lane_h == row_h, qrep, 0.0)
    means = jnp.concatenate([means_ref[...], jnp.zeros((128 - n_blocks, D_MODEL), F32)], axis=0)
    gate = lax.dot_general(qexp.astype(BF16), means.astype(BF16), (((1,), (1,)), ((), ())),
                           preferred_element_type=F32)
    lane = lax.broadcasted_iota(jnp.int32, gate.shape, 1)
    token = lax.broadcasted_iota(jnp.int32, gate.shape, 0) >> (H_B.bit_length() - 1)
    own = (past + token) >> (MOBA_BLOCK.bit_length() - 1)
    gate = jnp.where(lane < own, gate, -jnp.inf)
    out = jnp.zeros(gate.shape, F32)
    for r, (idx, _) in enumerate(_select_topk(gate, MOBA_TOPK)):
        out = jnp.where(lane == r, idx, out)
    o_ref[...] = out.astype(jnp.int32)


def _moba_select(qg, means, past):
    b, t, _ = qg.shape
    n_blocks = means.shape[1]
    assert MOBA_TOPK <= n_blocks <= 128
    kernel = functools.partial(_moba_select_kernel, t=t, n_blocks=n_blocks, past=past)
    return pl.pallas_call(
        kernel,
        grid=(b,),
        in_specs=[
            pl.BlockSpec((None, t, D_MODEL), lambda bi: (bi, 0, 0)),
            pl.BlockSpec((None, n_blocks, D_MODEL), lambda bi: (bi, 0, 0)),
        ],
        out_specs=pl.BlockSpec((None, t * H_B, 128), lambda bi: (bi, 0, 0)),
        out_shape=jax.ShapeDtypeStruct((b, t * H_B, 128), jnp.int32),
        compiler_params=_params("parallel"),
        name="moba_select",
    )(qg, means)


def _moba_sample_kernel(sel_ref, q_ref, g_ref, kn_ref, vn_ref, *refs, t, n_slots):
    kv_refs = refs[:n_slots]
    a_ref = refs[n_slots]
    keys_per_token = (n_slots // t) * kv_refs[0].shape[0]

    q = q_ref[...]
    qb = q.astype(BF16)
    kall = jnp.concatenate([r[:, :DH_B] for r in kv_refs], axis=0)
    vall = jnp.concatenate([r[:, DH_B:] for r in kv_refs], axis=0)
    s = lax.dot_general(qb, kall, (((1,), (1,)), ((), ())), preferred_element_type=F32) * (DH_B ** -0.5)
    row = lax.broadcasted_iota(jnp.int32, s.shape, 0)
    col = lax.broadcasted_iota(jnp.int32, s.shape, 1)
    mine = (col >= row * keys_per_token) & (col < (row + 1) * keys_per_token)
    s = jnp.where(mine, s, -jnp.inf)

    kn = kn_ref[...]
    vn = vn_ref[...]
    rown = lax.broadcasted_iota(jnp.int32, (t, 1), 0)
    s_own = []
    for tj in range(t):
        sj = jnp.sum(q * kn[tj:tj + 1, :], axis=-1, keepdims=True) * (DH_B ** -0.5)
        s_own.append(jnp.where(rown >= tj, sj, -jnp.inf))
    m = jnp.max(s, axis=-1, keepdims=True)
    for sj in s_own:
        m = jnp.maximum(m, sj)
    p = jnp.exp(s - m)
    l = jnp.sum(p, axis=-1, keepdims=True)
    acc = jnp.dot(p.astype(BF16), vall, preferred_element_type=F32)
    for tj, sj in enumerate(s_own):
        pj = jnp.exp(sj - m)
        l = l + pj
        acc = acc + pj * vn[tj:tj + 1, :]
    o = acc / l
    g = g_ref[...]
    a_ref[...] = (g * jax.nn.sigmoid(g) * o).astype(a_ref.dtype)


def _moba_sample(qg, kv_new, past_kv, sel_flat):
    b, t, _ = qg.shape
    n_slots = t * MOBA_TOPK

    def slot_map(ti, r):
        def index_map(bi, h, sel):
            return (bi, h, sel[((bi * t + ti) * H_B + h) * MOBA_TOPK + r], 0)
        return index_map

    slot_specs = [pl.BlockSpec((None, None, MOBA_BLOCK, 2 * DH_B), slot_map(ti, r))
                  for ti in range(t) for r in range(MOBA_TOPK)]

    kernel = functools.partial(_moba_sample_kernel, t=t, n_slots=n_slots)
    return pl.pallas_call(
        kernel,
        grid_spec=pltpu.PrefetchScalarGridSpec(
            num_scalar_prefetch=1,
            grid=(b, H_B),
            in_specs=[
                pl.BlockSpec((None, t, DH_B), lambda bi, h, sel: (bi, 0, h)),
                pl.BlockSpec((None, t, DH_B), lambda bi, h, sel: (bi, 0, H_B + h)),
                pl.BlockSpec((None, t, DH_B), lambda bi, h, sel: (bi, 0, h)),
                pl.BlockSpec((None, t, DH_B), lambda bi, h, sel: (bi, 0, H_B + h)),
            ] + slot_specs,
            out_specs=pl.BlockSpec((None, t, DH_B), lambda bi, h, sel: (bi, 0, h)),
        ),
        out_shape=jax.ShapeDtypeStruct((b, t, D_MODEL), F32),
        compiler_params=_params("parallel", "arbitrary"),
        name="moba_sample",
    )(sel_flat, qg, qg, kv_new, kv_new, *([past_kv] * n_slots))


def _run_group(x, mods, mods_kv, mods_f, s_in, past, weights, paged):
    (norm_g, w_in_a, w_out_a, w_q_b, w_o_b, kv_norm_g, w_kv, final_g) = weights
    b, t, d = x.shape
    rows = b * t
    prompt = paged is None
    pos = past + jnp.arange(t, dtype=jnp.int32)
    if prompt:
        tm, tn, tm_norm, rpb = 1024, 1024, 512, t
        act_dtype = BF16
        expand = lambda v: v[:, None, :]
    else:
        tm, tn, tm_norm, rpb = rows, 1024, rows, 0
        act_dtype = F32
        expand = lambda v: jnp.repeat(v, t, axis=0)

    def project(x2, g, shift, scale, w, layer):
        if prompt:
            return _norm_matmul(x2, g[None, :], shift, scale, w, layer, rpb, tm, tn)
        h = _norm_mod(x2, g[None, :], shift, scale, rpb, act_dtype, tm_norm)
        return _matmul(h, w, layer, tm, tn)

    x2 = x.reshape(rows, d)
    new_s = []
    kv = k_new = v_new = k_hm = vt_hm = None
    for l in range(DEPTH):
        shift, scale, gate = (expand(mods[l][:, i * d:(i + 1) * d]) for i in range(3))
        if l < N_A:
            qkvg = project(x2, norm_g[l], shift, scale, w_in_a, l).reshape(b, t, -1)
            a, s = _retention(qkvg, s_in, l, pos, act_dtype)
            new_s.append(s)
            x2 = _matmul(a.reshape(rows, V_A), w_out_a, l, tm, tn, res=x2, gate=gate, rows_per_batch=rpb)
        else:
            lb = l - N_A
            qg = project(x2, norm_g[l], shift, scale, w_q_b, lb).reshape(b, t, -1)
            if prompt:
                a = _moba_prompt(qg, k_hm, vt_hm, act_dtype)
            else:
                past_kv, means = paged
                sel = _moba_select(qg, means, past)[:, :, :MOBA_TOPK].reshape(-1)
                a = _moba_sample(qg, kv.reshape(b, t, -1), past_kv, sel)
            x2 = _matmul(a.reshape(rows, d), w_o_b, lb, tm, tn, res=x2, gate=gate, rows_per_batch=rpb)
        if l == N_A - 1:
            kv_shift, kv_scale = (expand(mods_kv[:, i * d:(i + 1) * d]) for i in range(2))
            kv = project(x2, kv_norm_g, kv_shift, kv_scale, w_kv[None], 0)
            if prompt:
                k_new, v_new, k_hm, vt_hm = _split_heads(kv, b, t)
            else:
                k_new = kv[:, :d].reshape(b, t, H_B, DH_B)
                v_new = kv[:, d:].reshape(b, t, H_B, DH_B)
    f_shift, f_scale = (expand(mods_f[:, i * d:(i + 1) * d]) for i in range(2))
    y = _norm_mod(x2, final_g[None, :], f_shift, f_scale, rpb, F32, tm_norm)
    return y.reshape(b, t, d), jnp.stack(new_s), k_new, v_new


def kernel(x_prompt, x_sample, state_ret, cache_k, cache_v, page_table, c_prompt, c_sample,
           norm_g, w_mod, b_mod, w_in_a, w_out_a, w_q_b, w_o_b,
           kv_norm_g, w_mod_kv, b_mod_kv, w_kv, final_g, w_mod_f, b_mod_f):
    bp = x_prompt.shape[0]
    bd, n_pages = page_table.shape
    n_pool, page, _, _ = cache_k.shape
    past_len = n_pages * page
    assert bp + bd <= MOD_ROWS

    c_all = jnp.concatenate([c_prompt, c_sample, jnp.zeros((MOD_ROWS - bp - bd, D_MODEL), F32)], axis=0)
    mods = _mod_matmul(c_all, w_mod, b_mod[:, None, :])
    mods_kv = _mod_matmul(c_all, w_mod_kv[None], b_mod_kv[None, None, :])[0]
    mods_f = _mod_matmul(c_all, w_mod_f[None], b_mod_f[None, None, :])[0]

    weights = (norm_g, w_in_a, w_out_a, w_q_b, w_o_b, kv_norm_g, w_kv, final_g)

    y_p, s_p, k_p, v_p = _run_group(
        x_prompt, mods[:, :bp], mods_kv[:bp], mods_f[:bp], None, 0, weights, None)

    pt_flat = page_table.reshape(-1)
    past_kv, means = _gather_cache(cache_k, cache_v, pt_flat, bd, n_pages)
    means = means.reshape(bd, -1, H_B * DH_B)
    y_s, s_s, k_s, v_s = _run_group(
        x_sample, mods[:, bp:bp + bd], mods_kv[bp:bp + bd], mods_f[bp:bp + bd], state_ret, past_len,
        weights, (past_kv, means))
    return (y_p, y_s, s_p, s_s, k_p, v_p, k_s, v_s)
```

```python
import functools
import math

import jax
import jax.numpy as jnp
from jax import lax
from jax.experimental import pallas as pl
from jax.experimental.pallas import tpu as pltpu

F32 = jnp.float32
BF16 = jnp.bfloat16

D_MODEL = 2048
DEPTH = 4
N_A = DEPTH // 2
H_A = 8
DK_A = D_MODEL // H_A
DV_A = 2 * DK_A
QK_A = H_A * DK_A
V_A = H_A * DV_A
RET_CHUNK = 256
H_B = 16
DH_B = D_MODEL // H_B
MOBA_BLOCK = 256
MOBA_TOPK = 3
ROPE_BASE = 10000.0
EPS = 1e-6
LOG2_E = math.log2(math.e)

VMEM_LIMIT_BYTES = 56 * 1024 * 1024
MOD_ROWS = 16


def _params(*sem):
    return pltpu.CompilerParams(dimension_semantics=sem, vmem_limit_bytes=VMEM_LIMIT_BYTES)


def _mod_kernel(c_ref, w_ref, b_ref, o_ref):
    acc = jnp.dot(c_ref[...].astype(BF16), w_ref[...].astype(BF16), preferred_element_type=F32)
    o_ref[...] = acc + b_ref[...]


def _mod_matmul(c, w, b, tn=1024):
    n_l, d, n = w.shape
    return pl.pallas_call(
        _mod_kernel,
        grid=(n_l, n // tn),
        in_specs=[
            pl.BlockSpec((MOD_ROWS, d), lambda l, j: (0, 0)),
            pl.BlockSpec((None, d, tn), lambda l, j: (l, 0, j)),
            pl.BlockSpec((None, 1, tn), lambda l, j: (l, 0, j)),
        ],
        out_specs=pl.BlockSpec((None, MOD_ROWS, tn), lambda l, j: (l, 0, j)),
        out_shape=jax.ShapeDtypeStruct((n_l, MOD_ROWS, n), F32),
        compiler_params=_params("parallel", "parallel"),
        name="mod_matmul",
    )(c, w, b)


def _norm_mod_kernel(x_ref, g_ref, sh_ref, sc_ref, o_ref):
    x = x_ref[...]
    r = lax.rsqrt(jnp.mean(x * x, axis=-1, keepdims=True) + EPS)
    y = (x * r) * g_ref[...]
    o_ref[...] = (y * (1.0 + sc_ref[...]) + sh_ref[...]).astype(o_ref.dtype)


def _norm_mod(x, g, shift, scale, rows_per_batch, out_dtype, tm):
    r, d = x.shape
    if rows_per_batch:
        tiles_per_b = rows_per_batch // tm
        mod_spec = pl.BlockSpec((None, 1, d), lambda i: (i // tiles_per_b, 0, 0))
    else:
        mod_spec = pl.BlockSpec((tm, d), lambda i: (i, 0))
    return pl.pallas_call(
        _norm_mod_kernel,
        grid=(r // tm,),
        in_specs=[
            pl.BlockSpec((tm, d), lambda i: (i, 0)),
            pl.BlockSpec((1, d), lambda i: (0, 0)),
            mod_spec,
            mod_spec,
        ],
        out_specs=pl.BlockSpec((tm, d), lambda i: (i, 0)),
        out_shape=jax.ShapeDtypeStruct((r, d), out_dtype),
        compiler_params=_params("parallel"),
        name="norm_mod",
    )(x, g, shift, scale)


def _matmul_kernel(a_ref, w_ref, o_ref):
    o_ref[...] = jnp.dot(
        a_ref[...].astype(BF16), w_ref[...].astype(BF16), preferred_element_type=F32
    ).astype(o_ref.dtype)


def _matmul_res_kernel(a_ref, w_ref, x_ref, gm_ref, o_ref):
    acc = jnp.dot(a_ref[...].astype(BF16), w_ref[...].astype(BF16), preferred_element_type=F32)
    o_ref[...] = x_ref[...] + gm_ref[...] * acc


MATMUL_W_TILE_ELEMS = 2 * 1024 * 1024


def _matmul(a, w, layer, tm, tn, res=None, gate=None, rows_per_batch=0):
    r, k = a.shape
    n = w.shape[2]
    tn = min(tn, MATMUL_W_TILE_ELEMS // k)
    in_specs = [
        pl.BlockSpec((tm, k), lambda i, j: (i, 0)),
        pl.BlockSpec((None, k, tn), lambda i, j: (layer, 0, j)),
    ]
    args = [a, w]
    kernel = _matmul_kernel
    if res is not None:
        kernel = _matmul_res_kernel
        in_specs.append(pl.BlockSpec((tm, tn), lambda i, j: (i, j)))
        if rows_per_batch:
            tiles_per_b = rows_per_batch // tm
            in_specs.append(pl.BlockSpec((None, 1, tn), lambda i, j: (i // tiles_per_b, 0, j)))
        else:
            in_specs.append(pl.BlockSpec((tm, tn), lambda i, j: (i, j)))
        args += [res, gate]
    return pl.pallas_call(
        kernel,
        grid=(r // tm, n // tn),
        in_specs=in_specs,
        out_specs=pl.BlockSpec((tm, tn), lambda i, j: (i, j)),
        out_shape=jax.ShapeDtypeStruct((r, n), F32),
        compiler_params=_params("parallel", "parallel"),
        name="matmul_res" if res is not None else "matmul",
    )(*args)


def _norm_matmul_kernel(x_ref, g_ref, sh_ref, sc_ref, w_ref, o_ref, h_scr):
    @pl.when(pl.program_id(1) == 0)
    def _():
        x = x_ref[...]
        r = lax.rsqrt(jnp.mean(x * x, axis=-1, keepdims=True) + EPS)
        y = (x * r) * g_ref[...]
        h_scr[...] = (y * (1.0 + sc_ref[...]) + sh_ref[...]).astype(BF16)

    o_ref[...] = jnp.dot(h_scr[...], w_ref[...].astype(BF16), preferred_element_type=F32)


def _norm_matmul(x, g, shift, scale, w, layer, rows_per_batch, tm, tn):
    r, k = x.shape
    n = w.shape[2]
    tn = min(tn, MATMUL_W_TILE_ELEMS // k)
    tiles_per_b = rows_per_batch // tm
    mod_spec = pl.BlockSpec((None, 1, k), lambda i, j: (i // tiles_per_b, 0, 0))
    return pl.pallas_call(
        _norm_matmul_kernel,
        grid=(r // tm, n // tn),
        in_specs=[
            pl.BlockSpec((tm, k), lambda i, j: (i, 0)),
            pl.BlockSpec((1, k), lambda i, j: (0, 0)),
            mod_spec,
            mod_spec,
            pl.BlockSpec((None, k, tn), lambda i, j: (layer, 0, j)),
        ],
        out_specs=pl.BlockSpec((tm, tn), lambda i, j: (i, j)),
        out_shape=jax.ShapeDtypeStruct((r, n), F32),
        scratch_shapes=[pltpu.VMEM((tm, k), BF16)],
        compiler_params=_params("parallel", "arbitrary"),
        name="norm_matmul",
    )(x, g, shift, scale, w)


def _split_heads_kernel(kv_ref, k_out_ref, v_out_ref, k_hm_ref, vt_hm_ref, *, tm):
    for h in range(H_B):
        kh = kv_ref[:, h * DH_B:(h + 1) * DH_B]
        vh = kv_ref[:, (H_B + h) * DH_B:(H_B + h + 1) * DH_B]
        k_out_ref[pl.ds(h, tm, stride=H_B), :] = kh
        v_out_ref[pl.ds(h, tm, stride=H_B), :] = vh
        k_hm_ref[h] = kh.astype(BF16)
        vt_hm_ref[h] = vh.T.astype(BF16)


def _split_heads(kv, b, t, tm=256):
    r = b * t
    tiles_per_b = t // tm
    kernel = functools.partial(_split_heads_kernel, tm=tm)
    rows_spec = pl.BlockSpec((tm * H_B, DH_B), lambda i: (i, 0))
    k_out, v_out, k_hm, vt_hm = pl.pallas_call(
        kernel,
        grid=(r // tm,),
        in_specs=[pl.BlockSpec((tm, 2 * D_MODEL), lambda i: (i, 0))],
        out_specs=[
            rows_spec,
            rows_spec,
            pl.BlockSpec((None, H_B, tm, DH_B), lambda i: (i // tiles_per_b, 0, i % tiles_per_b, 0)),
            pl.BlockSpec((None, H_B, DH_B, tm), lambda i: (i // tiles_per_b, 0, 0, i % tiles_per_b)),
        ],
        out_shape=[
            jax.ShapeDtypeStruct((r * H_B, DH_B), F32),
            jax.ShapeDtypeStruct((r * H_B, DH_B), F32),
            jax.ShapeDtypeStruct((b, H_B, t, DH_B), BF16),
            jax.ShapeDtypeStruct((b, H_B, DH_B, t), BF16),
        ],
        compiler_params=_params("parallel"),
        name="split_heads",
    )(kv)
    return k_out.reshape(b, t, H_B, DH_B), v_out.reshape(b, t, H_B, DH_B), k_hm, vt_hm


def _rope_tables(pos):
    half = DK_A // 2
    inv = 1.0 / (ROPE_BASE ** jnp.linspace(0.0, 1.0, half, dtype=F32))
    ang = pos.astype(F32)[:, None] * inv[None, :]
    return jnp.cos(ang), jnp.sin(ang)


def _retention_tables(chunk):
    log_g = jnp.log1p(-(2.0 ** (-5.0 - jnp.arange(H_A, dtype=F32))))
    i = jnp.arange(chunk, dtype=F32)
    diff = i[:, None] - i[None, :]
    decay = jnp.where(diff >= 0, jnp.exp(jnp.maximum(diff, 0.0)[None] * log_g[:, None, None]), 0.0)
    cross_scale = jnp.exp((i + 1.0)[:, None] * log_g[None, :]).T[:, :, None]
    wk = jnp.exp((chunk - 1.0 - i)[:, None] * log_g[None, :]).T[:, :, None]
    state_decay = jnp.exp(chunk * log_g)
    return decay, cross_scale, wk, state_decay


RET_HEADS_PER_STEP = 4
RET_CHUNKS_PER_STEP = 1


def _retention_kernel(sdec_ref, q_ref, k_ref, v_ref, g_ref, cos_ref, sin_ref, dec_ref, cs_ref,
                      wk_ref, *refs, chunk, heads, chunks, n_steps, zero_state):
    a_ref, sout_ref, s_scr = refs[-3:]
    hg = pl.program_id(1)
    c = pl.program_id(2)

    @pl.when(c == 0)
    def _():
        s_scr[...] = jnp.zeros(s_scr.shape, F32) if zero_state else refs[0][...]

    half = DK_A // 2

    def rot(x, cos, sin):
        x1 = x[:, :half]
        x2 = x[:, half:]
        return jnp.concatenate([x1 * cos - x2 * sin, x1 * sin + x2 * cos], axis=-1)

    for hh in range(heads):
        s = s_scr[hh]
        kcols = slice(hh * DK_A, (hh + 1) * DK_A)
        vcols = slice(hh * DV_A, (hh + 1) * DV_A)
        for ci in range(chunks):
            rows = slice(ci * chunk, (ci + 1) * chunk)
            cos = cos_ref[rows, :]
            sin = sin_ref[rows, :]
            q = rot(q_ref[rows, kcols], cos, sin)
            k = rot(k_ref[rows, kcols], cos, sin) * (DK_A ** -0.5)
            vb = v_ref[rows, vcols].astype(BF16)
            qb = q.astype(BF16)
            kb = k.astype(BF16)
            scores = lax.dot_general(qb, kb, (((1,), (1,)), ((), ())),
                                     preferred_element_type=F32) * dec_ref[hh]
            inner = jnp.dot(scores.astype(BF16), vb, preferred_element_type=F32)
            cross = jnp.dot(qb, s.astype(BF16), preferred_element_type=F32) * cs_ref[hh]
            kw = (k * wk_ref[hh]).astype(BF16)
            upd = lax.dot_general(kw, vb, (((0,), (0,)), ((), ())), preferred_element_type=F32)
            s = sdec_ref[hg * heads + hh] * s + upd

            o = inner + cross
            o = o * lax.rsqrt(jnp.mean(o * o, axis=-1, keepdims=True) + EPS)
            g = g_ref[rows, vcols]
            a_ref[rows, vcols] = (g * jax.nn.sigmoid(g) * o).astype(a_ref.dtype)
        s_scr[hh] = s

        @pl.when(c == n_steps - 1)
        def _():
            sout_ref[hh] = s


def _retention(qkvg, s_in, layer, pos, out_dtype, s_out_prev):
    b, t, _ = qkvg.shape
    chunk = math.gcd(t, RET_CHUNK)
    n_chunks = t // chunk
    heads = RET_HEADS_PER_STEP
    chunks = math.gcd(n_chunks, RET_CHUNKS_PER_STEP)
    n_steps = n_chunks // chunks
    rows = chunk * chunks
    assert H_A % heads == 0
    cos, sin = _rope_tables(pos)
    decay, cross_scale, wk, state_decay = _retention_tables(chunk)
    k_off = QK_A // (heads * DK_A)
    v_off = 2 * QK_A // (heads * DV_A)
    g_off = (2 * QK_A + V_A) // (heads * DV_A)
    kernel = functools.partial(_retention_kernel, chunk=chunk, heads=heads, chunks=chunks, n_steps=n_steps,
                               zero_state=s_in is None)
    state_spec = pl.BlockSpec((None, None, heads, DK_A, DV_A), lambda bi, h, c: (layer, bi, h, 0, 0))
    state_args, state_specs, aliases = [], [], {}
    if s_in is not None:
        state_args.append(s_in)
        state_specs.append(state_spec)
    if s_out_prev is not None:
        aliases = {10 + len(state_args): 1}
        state_args.append(s_out_prev)
        state_specs.append(pl.BlockSpec(memory_space=pl.ANY))
    return pl.pallas_call(
        kernel,
        grid=(b, H_A // heads, n_steps),
        in_specs=[
            pl.BlockSpec(memory_space=pltpu.SMEM),
            pl.BlockSpec((None, rows, heads * DK_A), lambda bi, h, c: (bi, c, h)),
            pl.BlockSpec((None, rows, heads * DK_A), lambda bi, h, c: (bi, c, k_off + h)),
            pl.BlockSpec((None, rows, heads * DV_A), lambda bi, h, c: (bi, c, v_off + h)),
            pl.BlockSpec((None, rows, heads * DV_A), lambda bi, h, c: (bi, c, g_off + h)),
            pl.BlockSpec((rows, DK_A // 2), lambda bi, h, c: (c, 0)),
            pl.BlockSpec((rows, DK_A // 2), lambda bi, h, c: (c, 0)),
            pl.BlockSpec((heads, chunk, chunk), lambda bi, h, c: (h, 0, 0)),
            pl.BlockSpec((heads, chunk, 1), lambda bi, h, c: (h, 0, 0)),
            pl.BlockSpec((heads, chunk, 1), lambda bi, h, c: (h, 0, 0)),
        ] + state_specs,
        out_specs=[
            pl.BlockSpec((None, rows, heads * DV_A), lambda bi, h, c: (bi, c, h)),
            state_spec,
        ],
        out_shape=[
            jax.ShapeDtypeStruct((b, t, V_A), out_dtype),
            jax.ShapeDtypeStruct((N_A, b, H_A, DK_A, DV_A), F32),
        ],
        scratch_shapes=[pltpu.VMEM((heads, DK_A, DV_A), F32)],
        input_output_aliases=aliases,
        compiler_params=_params("parallel", "parallel", "arbitrary"),
        name="retention",
    )(state_decay, qkvg, qkvg, qkvg, qkvg, cos, sin, decay, cross_scale, wk, *state_args)


def _select_topk(gate, n_sel, axis=1):
    pos = lax.broadcasted_iota(jnp.int32, gate.shape, axis).astype(F32)
    picks = []
    for _ in range(n_sel):
        m = jnp.max(gate, axis=axis, keepdims=True)
        idx = jnp.min(jnp.where(gate == m, pos, float(gate.shape[axis])), axis=axis, keepdims=True)
        valid = m > -jnp.inf
        picks.append((idx, valid))
        gate = jnp.where((pos == idx) & valid, -jnp.inf, gate)
    return picks


def _moba_prompt_kernel(q_ref, g_ref, k_ref, vt_ref, a_ref, *, n_blocks):
    bs = MOBA_BLOCK
    scale = DH_B ** -0.5
    n_sel = min(MOBA_TOPK, n_blocks)
    kb = k_ref[...]
    ones_rows = 16
    vt = jnp.concatenate([vt_ref[...], jnp.ones((ones_rows, kb.shape[0]), BF16)], axis=0)
    means_b = jnp.mean(kb.astype(F32).reshape(n_blocks, bs, DH_B), axis=1).astype(BF16)
    key = lax.broadcasted_iota(jnp.int32, (bs, bs), 0)
    qry = lax.broadcasted_iota(jnp.int32, (bs, bs), 1)
    causal = key <= qry

    for i in range(n_blocks):
        rows = slice(i * bs, (i + 1) * bs)
        q = q_ref[rows, :]
        qs = (q * (scale * LOG2_E)).astype(BF16)
        picks = None
        if i > n_sel:
            gate = lax.dot_general(means_b, q.astype(BF16), (((1,), (1,)), ((), ())),
                                   preferred_element_type=F32)
            blk = lax.broadcasted_iota(jnp.int32, gate.shape, 0)
            picks = _select_topk(jnp.where(blk < i, gate, -jnp.inf), n_sel, axis=0)
        blocks = []
        for j in range(i + 1):
            sj = lax.dot_general(kb[j * bs:(j + 1) * bs, :], qs, (((1,), (1,)), ((), ())),
                                 preferred_element_type=F32)
            if j == i:
                sj = jnp.where(causal, sj, -jnp.inf)
            elif picks is not None:
                chosen = picks[0][0] == float(j)
                for idx, _ in picks[1:]:
                    chosen = chosen | (idx == float(j))
                sj = jnp.where(chosen, sj, -jnp.inf)
            blocks.append(sj)
        m = jnp.max(functools.reduce(jnp.maximum, blocks), axis=0, keepdims=True)
        acc = jnp.zeros((DH_B + ones_rows, bs), F32)
        for j, sj in enumerate(blocks):
            p = jnp.exp2(sj - m).astype(BF16)
            acc = acc + jnp.dot(vt[:, j * bs:(j + 1) * bs], p, preferred_element_type=F32)
        o = (acc[:DH_B, :] / acc[DH_B:DH_B + 1, :]).T
        g = g_ref[rows, :]
        a_ref[rows, :] = (g * jax.nn.sigmoid(g) * o).astype(a_ref.dtype)


def _moba_prompt(qg, k_hm, vt_hm, out_dtype):
    b, t, _ = qg.shape
    n_blocks = t // MOBA_BLOCK
    assert n_blocks * MOBA_BLOCK == t and n_blocks <= 128
    kernel = functools.partial(_moba_prompt_kernel, n_blocks=n_blocks)
    return pl.pallas_call(
        kernel,
        grid=(b, H_B),
        in_specs=[
            pl.BlockSpec((None, t, DH_B), lambda bi, h: (bi, 0, h)),
            pl.BlockSpec((None, t, DH_B), lambda bi, h: (bi, 0, H_B + h)),
            pl.BlockSpec((None, None, t, DH_B), lambda bi, h: (bi, h, 0, 0)),
            pl.BlockSpec((None, None, DH_B, t), lambda bi, h: (bi, h, 0, 0)),
        ],
        out_specs=pl.BlockSpec((None, t, DH_B), lambda bi, h: (bi, 0, h)),
        out_shape=jax.ShapeDtypeStruct((b, t, D_MODEL), out_dtype),
        compiler_params=_params("parallel", "parallel"),
        name="moba_prompt",
    )(qg, qg, k_hm, vt_hm)


HEAD_GROUP = 8


GATHER_PAGES_PER_STEP = 4


def _gather_cache_kernel(pt_ref, *refs, page, n_groups, pages_per_step, pages_per_block):
    n_slabs = pages_per_step * n_groups
    o_ref, m_ref = refs[2 * n_slabs], refs[2 * n_slabs + 1]
    for part in range(2):
        lanes = slice(part * DH_B, (part + 1) * DH_B)
        for pg in range(pages_per_step):
            for grp in range(n_groups):
                src = refs[part * n_slabs + pg * n_groups + grp]
                src = src.reshape(page * HEAD_GROUP, DH_B)
                for hg in range(HEAD_GROUP):
                    rows = src[pl.ds(hg, page, stride=HEAD_GROUP), :]
                    o_ref[grp * HEAD_GROUP + hg, pg * page:(pg + 1) * page, lanes] = rows.astype(BF16)
    for blk in range(pages_per_step // pages_per_block):
        for grp in range(n_groups):
            total = None
            for pg in range(blk * pages_per_block, (blk + 1) * pages_per_block):
                part_sum = jnp.sum(refs[pg * n_groups + grp][...], axis=0)
                total = part_sum if total is None else total + part_sum
            m_ref[blk, grp * HEAD_GROUP:(grp + 1) * HEAD_GROUP, :] = total / MOBA_BLOCK


def _gather_cache(cache_k, cache_v, page_table_flat, n_seq, n_pages):
    n_pool, page, n_h, dh = cache_k.shape
    pages_per_block = MOBA_BLOCK // page
    pps = GATHER_PAGES_PER_STEP
    assert n_h == H_B and dh == DH_B and n_h % HEAD_GROUP == 0
    assert pps % pages_per_block == 0 and n_pages % pps == 0
    n_blocks = n_pages // pages_per_block
    n_groups = n_h // HEAD_GROUP
    grouped = [c.reshape(n_pool, page, n_groups, HEAD_GROUP, dh) for c in (cache_k, cache_v)]

    def page_map(pg, grp):
        return lambda bi, j, pt: (pt[bi * n_pages + pps * j + pg], 0, grp, 0, 0)

    def slab_specs():
        return [pl.BlockSpec((None, page, None, HEAD_GROUP, dh), page_map(pg, grp))
                for pg in range(pps) for grp in range(n_groups)]

    n_slabs = pps * n_groups
    blocks_per_step = pps // pages_per_block
    kernel = functools.partial(_gather_cache_kernel, page=page, n_groups=n_groups, pages_per_step=pps,
                               pages_per_block=pages_per_block)
    return pl.pallas_call(
        kernel,
        grid_spec=pltpu.PrefetchScalarGridSpec(
            num_scalar_prefetch=1,
            grid=(n_seq, n_pages // pps),
            in_specs=slab_specs() + slab_specs(),
            out_specs=[
                pl.BlockSpec((None, n_h, pps * page, 2 * dh), lambda bi, j, pt: (bi, 0, j, 0)),
                pl.BlockSpec((None, blocks_per_step, n_h, dh), lambda bi, j, pt: (bi, j, 0, 0)),
            ],
        ),
        out_shape=[
            jax.ShapeDtypeStruct((n_seq, n_h, n_pages * page, 2 * dh), BF16),
            jax.ShapeDtypeStruct((n_seq, n_blocks, n_h, dh), F32),
        ],
        compiler_params=_params("parallel", "arbitrary"),
        name="gather_cache",
    )(page_table_flat, *([grouped[0]] * n_slabs), *([grouped[1]] * n_slabs))


def _moba_select_kernel(q_ref, means_ref, o_ref, *, t, n_blocks, past):
    q = q_ref[...]
    rows = t * H_B
    qrep = jnp.concatenate([jnp.broadcast_to(q[ti:ti + 1, :], (H_B, D_MODEL)) for ti in range(t)], axis=0)
    lane_h = lax.broadcasted_iota(jnp.int32, (rows, D_MODEL), 1) >> (DH_B.bit_length() - 1)
    row_h = lax.broadcasted_iota(jnp.int32, (rows, D_MODEL), 0) & (H_B - 1)
    qexp = jnp.where(lane_h == row_h, qrep, 0.0)
    means = jnp.concatenate([means_ref[...], jnp.zeros((128 - n_blocks, D_MODEL), F32)], axis=0)
    gate = lax.dot_general(qexp.astype(BF16), means.astype(BF16), (((1,), (1,)), ((), ())),
                           preferred_element_type=F32)
    lane = lax.broadcasted_iota(jnp.int32, gate.shape, 1)
    token = lax.broadcasted_iota(jnp.int32, gate.shape, 0) >> (H_B.bit_length() - 1)
    own = (past + token) >> (MOBA_BLOCK.bit_length() - 1)
    gate = jnp.where(lane < own, gate, -jnp.inf)
    out = jnp.zeros(gate.shape, F32)
    for r, (idx, _) in enumerate(_select_topk(gate, MOBA_TOPK)):
        out = jnp.where(lane == r, idx, out)
    o_ref[...] = out.astype(jnp.int32)


def _moba_select(qg, means, past):
    b, t, _ = qg.shape
    n_blocks = means.shape[1]
    assert MOBA_TOPK <= n_blocks <= 128
    kernel = functools.partial(_moba_select_kernel, t=t, n_blocks=n_blocks, past=past)
    return pl.pallas_call(
        kernel,
        grid=(b,),
        in_specs=[
            pl.BlockSpec((None, t, D_MODEL), lambda bi: (bi, 0, 0)),
            pl.BlockSpec((None, n_blocks, D_MODEL), lambda bi: (bi, 0, 0)),
        ],
        out_specs=pl.BlockSpec((None, t * H_B, 128), lambda bi: (bi, 0, 0)),
        out_shape=jax.ShapeDtypeStruct((b, t * H_B, 128), jnp.int32),
        compiler_params=_params("parallel"),
        name="moba_select",
    )(qg, means)


def _moba_sample_kernel(sel_ref, new_ref, *refs, t, n_slots):
    kv_refs = refs[:n_slots]
    a_ref = refs[n_slots]
    keys_per_token = (n_slots // t) * kv_refs[0].shape[0]

    q = new_ref[:, 0:DH_B]
    qb = q.astype(BF16)
    kall = jnp.concatenate([r[:, :DH_B] for r in kv_refs], axis=0)
    vall = jnp.concatenate([r[:, DH_B:] for r in kv_refs], axis=0)
    s = lax.dot_general(qb, kall, (((1,), (1,)), ((), ())), preferred_element_type=F32) * (DH_B ** -0.5)
    row = lax.broadcasted_iota(jnp.int32, s.shape, 0)
    col = lax.broadcasted_iota(jnp.int32, s.shape, 1)
    mine = (col >= row * keys_per_token) & (col < (row + 1) * keys_per_token)
    s = jnp.where(mine, s, -jnp.inf)

    kn = new_ref[:, 2 * DH_B:3 * DH_B]
    vn = new_ref[:, 3 * DH_B:4 * DH_B]
    rown = lax.broadcasted_iota(jnp.int32, (t, 1), 0)
    s_own = []
    for tj in range(t):
        sj = jnp.sum(q * kn[tj:tj + 1, :], axis=-1, keepdims=True) * (DH_B ** -0.5)
        s_own.append(jnp.where(rown >= tj, sj, -jnp.inf))
    m = jnp.max(s, axis=-1, keepdims=True)
    for sj in s_own:
        m = jnp.maximum(m, sj)
    p = jnp.exp(s - m)
    l = jnp.sum(p, axis=-1, keepdims=True)
    acc = jnp.dot(p.astype(BF16), vall, preferred_element_type=F32)
    for tj, sj in enumerate(s_own):
        pj = jnp.exp(sj - m)
        l = l + pj
        acc = acc + pj * vn[tj:tj + 1, :]
    o = acc / l
    g = new_ref[:, DH_B:2 * DH_B]
    a_ref[...] = (g * jax.nn.sigmoid(g) * o).astype(a_ref.dtype)


def _moba_sample(qg, kv_new, past_kv, sel_flat):
    b, t, _ = qg.shape
    n_slots = t * MOBA_TOPK
    per_head = [a.reshape(b, t, H_B, DH_B) for a in
                (qg[:, :, :D_MODEL], qg[:, :, D_MODEL:], kv_new[:, :, :D_MODEL], kv_new[:, :, D_MODEL:])]
    new_rows = jnp.concatenate(per_head, axis=-1).transpose(0, 2, 1, 3)

    def slot_map(ti, r):
        def index_map(bi, h, sel):
            return (bi, h, sel[((bi * t + ti) * H_B + h) * MOBA_TOPK + r], 0)
        return index_map

    slot_specs = [pl.BlockSpec((None, None, MOBA_BLOCK, 2 * DH_B), slot_map(ti, r))
                  for ti in range(t) for r in range(MOBA_TOPK)]

    kernel = functools.partial(_moba_sample_kernel, t=t, n_slots=n_slots)
    return pl.pallas_call(
        kernel,
        grid_spec=pltpu.PrefetchScalarGridSpec(
            num_scalar_prefetch=1,
            grid=(b, H_B),
            in_specs=[pl.BlockSpec((None, None, t, 4 * DH_B), lambda bi, h, sel: (bi, h, 0, 0))] + slot_specs,
            out_specs=pl.BlockSpec((None, t, DH_B), lambda bi, h, sel: (bi, 0, h)),
        ),
        out_shape=jax.ShapeDtypeStruct((b, t, D_MODEL), F32),
        compiler_params=_params("parallel", "arbitrary"),
        name="moba_sample",
    )(sel_flat, new_rows, *([past_kv] * n_slots))


def _run_group(x, mods, mods_kv, mods_f, s_in, past, weights, paged):
    (norm_g, w_in_a, w_out_a, w_q_b, w_o_b, kv_norm_g, w_kv, final_g) = weights
    b, t, d = x.shape
    rows = b * t
    prompt = paged is None
    pos = past + jnp.arange(t, dtype=jnp.int32)
    if prompt:
        tm, tn, tm_norm, rpb = 1024, 1024, 512, t
        act_dtype = BF16
        expand = lambda v: v[:, None, :]
    else:
        tm, tn, tm_norm, rpb = rows, 1024, rows, 0
        act_dtype = F32
        expand = lambda v: jnp.repeat(v, t, axis=0)

    def project(x2, g, shift, scale, w, layer):
        if prompt:
            return _norm_matmul(x2, g[None, :], shift, scale, w, layer, rpb, tm, tn)
        h = _norm_mod(x2, g[None, :], shift, scale, rpb, act_dtype, tm_norm)
        return _matmul(h, w, layer, tm, tn)

    x2 = x.reshape(rows, d)
    new_s = None
    kv = k_new = v_new = k_hm = vt_hm = None
    for l in range(DEPTH):
        shift, scale, gate = (expand(mods[l][:, i * d:(i + 1) * d]) for i in range(3))
        if l < N_A:
            qkvg = project(x2, norm_g[l], shift, scale, w_in_a, l).reshape(b, t, -1)
            a, new_s = _retention(qkvg, s_in, l, pos, act_dtype, new_s)
            x2 = _matmul(a.reshape(rows, V_A), w_out_a, l, tm, tn, res=x2, gate=gate, rows_per_batch=rpb)
        else:
            lb = l - N_A
            qg = project(x2, norm_g[l], shift, scale, w_q_b, lb).reshape(b, t, -1)
            if prompt:
                a = _moba_prompt(qg, k_hm, vt_hm, act_dtype)
            else:
                past_kv, means = paged
                sel = _moba_select(qg, means, past)[:, :, :MOBA_TOPK].reshape(-1)
                a = _moba_sample(qg, kv.reshape(b, t, -1), past_kv, sel)
            x2 = _matmul(a.reshape(rows, d), w_o_b, lb, tm, tn, res=x2, gate=gate, rows_per_batch=rpb)
        if l == N_A - 1:
            kv_shift, kv_scale = (expand(mods_kv[:, i * d:(i + 1) * d]) for i in range(2))
            kv = project(x2, kv_norm_g, kv_shift, kv_scale, w_kv[None], 0)
            if prompt:
                k_new, v_new, k_hm, vt_hm = _split_heads(kv, b, t)
            else:
                k_new = kv[:, :d].reshape(b, t, H_B, DH_B)
                v_new = kv[:, d:].reshape(b, t, H_B, DH_B)
    f_shift, f_scale = (expand(mods_f[:, i * d:(i + 1) * d]) for i in range(2))
    y = _norm_mod(x2, final_g[None, :], f_shift, f_scale, rpb, F32, tm_norm)
    return y.reshape(b, t, d), new_s, k_new, v_new


def kernel(x_prompt, x_sample, state_ret, cache_k, cache_v, page_table, c_prompt, c_sample,
           norm_g, w_mod, b_mod, w_in_a, w_out_a, w_q_b, w_o_b,
           kv_norm_g, w_mod_kv, b_mod_kv, w_kv, final_g, w_mod_f, b_mod_f):
    bp = x_prompt.shape[0]
    bd, n_pages = page_table.shape
    n_pool, page, _, _ = cache_k.shape
    past_len = n_pages * page
    assert bp + bd <= MOD_ROWS

    c_all = jnp.concatenate([c_prompt, c_sample, jnp.zeros((MOD_ROWS - bp - bd, D_MODEL), F32)], axis=0)
    mods = _mod_matmul(c_all, w_mod, b_mod[:, None, :])
    mods_kv = _mod_matmul(c_all, w_mod_kv[None], b_mod_kv[None, None, :])[0]
    mods_f = _mod_matmul(c_all, w_mod_f[None], b_mod_f[None, None, :])[0]

    weights = (norm_g, w_in_a, w_out_a, w_q_b, w_o_b, kv_norm_g, w_kv, final_g)

    y_p, s_p, k_p, v_p = _run_group(
        x_prompt, mods[:, :bp], mods_kv[:bp], mods_f[:bp], None, 0, weights, None)

    pt_flat = page_table.reshape(-1)
    past_kv, means = _gather_cache(cache_k, cache_v, pt_flat, bd, n_pages)
    means = means.reshape(bd, -1, H_B * DH_B)
    y_s, s_s, k_s, v_s = _run_group(
        x_sample, mods[:, bp:bp + bd], mods_kv[bp:bp + bd], mods_f[bp:bp + bd], state_ret, past_len,
        weights, (past_kv, means))
    return (y_p, y_s, s_p, s_s, k_p, v_p, k_s, v_s)
```

```python
import functools
import math

import jax
import jax.numpy as jnp
from jax import lax
from jax.experimental import pallas as pl
from jax.experimental.pallas import tpu as pltpu

F32 = jnp.float32
BF16 = jnp.bfloat16

D_MODEL = 2048
DEPTH = 4
N_A = DEPTH // 2
H_A = 8
DK_A = D_MODEL // H_A
DV_A = 2 * DK_A
QK_A = H_A * DK_A
V_A = H_A * DV_A
RET_CHUNK = 256
H_B = 16
DH_B = D_MODEL // H_B
MOBA_BLOCK = 256
MOBA_TOPK = 3
ROPE_BASE = 10000.0
EPS = 1e-6
LOG2_E = math.log2(math.e)

VMEM_LIMIT_BYTES = 56 * 1024 * 1024
MOD_ROWS = 16


def _params(*sem):
    return pltpu.CompilerParams(dimension_semantics=sem, vmem_limit_bytes=VMEM_LIMIT_BYTES)


def _mod_kernel(c_ref, w_ref, b_ref, o_ref):
    acc = jnp.dot(c_ref[...].astype(BF16), w_ref[...].astype(BF16), preferred_element_type=F32)
    o_ref[...] = acc + b_ref[...]


def _mod_matmul(c, w, b, tn=1024):
    n_l, d, n = w.shape
    return pl.pallas_call(
        _mod_kernel,
        grid=(n_l, n // tn),
        in_specs=[
            pl.BlockSpec((MOD_ROWS, d), lambda l, j: (0, 0)),
            pl.BlockSpec((None, d, tn), lambda l, j: (l, 0, j)),
            pl.BlockSpec((None, 1, tn), lambda l, j: (l, 0, j)),
        ],
        out_specs=pl.BlockSpec((None, MOD_ROWS, tn), lambda l, j: (l, 0, j)),
        out_shape=jax.ShapeDtypeStruct((n_l, MOD_ROWS, n), F32),
        compiler_params=_params("parallel", "parallel"),
        name="mod_matmul",
    )(c, w, b)


def _norm_mod_kernel(x_ref, g_ref, sh_ref, sc_ref, o_ref):
    x = x_ref[...]
    r = lax.rsqrt(jnp.mean(x * x, axis=-1, keepdims=True) + EPS)
    y = (x * r) * g_ref[...]
    o_ref[...] = (y * (1.0 + sc_ref[...]) + sh_ref[...]).astype(o_ref.dtype)


def _norm_mod(x, g, shift, scale, rows_per_batch, out_dtype, tm):
    r, d = x.shape
    if rows_per_batch:
        tiles_per_b = rows_per_batch // tm
        mod_spec = pl.BlockSpec((None, 1, d), lambda i: (i // tiles_per_b, 0, 0))
    else:
        mod_spec = pl.BlockSpec((tm, d), lambda i: (i, 0))
    return pl.pallas_call(
        _norm_mod_kernel,
        grid=(r // tm,),
        in_specs=[
            pl.BlockSpec((tm, d), lambda i: (i, 0)),
            pl.BlockSpec((1, d), lambda i: (0, 0)),
            mod_spec,
            mod_spec,
        ],
        out_specs=pl.BlockSpec((tm, d), lambda i: (i, 0)),
        out_shape=jax.ShapeDtypeStruct((r, d), out_dtype),
        compiler_params=_params("parallel"),
        name="norm_mod",
    )(x, g, shift, scale)


def _matmul_kernel(a_ref, w_ref, o_ref):
    o_ref[...] = jnp.dot(
        a_ref[...].astype(BF16), w_ref[...].astype(BF16), preferred_element_type=F32
    ).astype(o_ref.dtype)


def _matmul_res_kernel(a_ref, w_ref, x_ref, gm_ref, o_ref):
    acc = jnp.dot(a_ref[...].astype(BF16), w_ref[...].astype(BF16), preferred_element_type=F32)
    o_ref[...] = x_ref[...] + gm_ref[...] * acc


MATMUL_W_TILE_ELEMS = 2 * 1024 * 1024


def _matmul(a, w, layer, tm, tn, res=None, gate=None, rows_per_batch=0):
    r, k = a.shape
    n = w.shape[2]
    tn = min(tn, MATMUL_W_TILE_ELEMS // k)
    in_specs = [
        pl.BlockSpec((tm, k), lambda i, j: (i, 0)),
        pl.BlockSpec((None, k, tn), lambda i, j: (layer, 0, j)),
    ]
    args = [a, w]
    kernel = _matmul_kernel
    if res is not None:
        kernel = _matmul_res_kernel
        in_specs.append(pl.BlockSpec((tm, tn), lambda i, j: (i, j)))
        if rows_per_batch:
            tiles_per_b = rows_per_batch // tm
            in_specs.append(pl.BlockSpec((None, 1, tn), lambda i, j: (i // tiles_per_b, 0, j)))
        else:
            in_specs.append(pl.BlockSpec((tm, tn), lambda i, j: (i, j)))
        args += [res, gate]
    return pl.pallas_call(
        kernel,
        grid=(r // tm, n // tn),
        in_specs=in_specs,
        out_specs=pl.BlockSpec((tm, tn), lambda i, j: (i, j)),
        out_shape=jax.ShapeDtypeStruct((r, n), F32),
        compiler_params=_params("parallel", "parallel"),
        name="matmul_res" if res is not None else "matmul",
    )(*args)


def _norm_matmul_kernel(x_ref, g_ref, sh_ref, sc_ref, w_ref, o_ref, h_scr):
    @pl.when(pl.program_id(1) == 0)
    def _():
        x = x_ref[...]
        r = lax.rsqrt(jnp.mean(x * x, axis=-1, keepdims=True) + EPS)
        y = (x * r) * g_ref[...]
        h_scr[...] = (y * (1.0 + sc_ref[...]) + sh_ref[...]).astype(BF16)

    o_ref[...] = jnp.dot(h_scr[...], w_ref[...].astype(BF16), preferred_element_type=F32)


def _norm_matmul(x, g, shift, scale, w, layer, rows_per_batch, tm, tn):
    r, k = x.shape
    n = w.shape[2]
    tn = min(tn, MATMUL_W_TILE_ELEMS // k)
    tiles_per_b = rows_per_batch // tm
    mod_spec = pl.BlockSpec((None, 1, k), lambda i, j: (i // tiles_per_b, 0, 0))
    return pl.pallas_call(
        _norm_matmul_kernel,
        grid=(r // tm, n // tn),
        in_specs=[
            pl.BlockSpec((tm, k), lambda i, j: (i, 0)),
            pl.BlockSpec((1, k), lambda i, j: (0, 0)),
            mod_spec,
            mod_spec,
            pl.BlockSpec((None, k, tn), lambda i, j: (layer, 0, j)),
        ],
        out_specs=pl.BlockSpec((tm, tn), lambda i, j: (i, j)),
        out_shape=jax.ShapeDtypeStruct((r, n), F32),
        scratch_shapes=[pltpu.VMEM((tm, k), BF16)],
        compiler_params=_params("parallel", "arbitrary"),
        name="norm_matmul",
    )(x, g, shift, scale, w)


def _split_heads_kernel(kv_ref, k_out_ref, v_out_ref, k_hm_ref, vt_hm_ref, *, tm):
    for h in range(H_B):
        kh = kv_ref[:, h * DH_B:(h + 1) * DH_B]
        vh = kv_ref[:, (H_B + h) * DH_B:(H_B + h + 1) * DH_B]
        k_out_ref[pl.ds(h, tm, stride=H_B), :] = kh
        v_out_ref[pl.ds(h, tm, stride=H_B), :] = vh
        k_hm_ref[h] = kh.astype(BF16)
        vt_hm_ref[h] = vh.T.astype(BF16)


def _split_heads(kv, b, t, tm=256):
    r = b * t
    tiles_per_b = t // tm
    kernel = functools.partial(_split_heads_kernel, tm=tm)
    rows_spec = pl.BlockSpec((tm * H_B, DH_B), lambda i: (i, 0))
    k_out, v_out, k_hm, vt_hm = pl.pallas_call(
        kernel,
        grid=(r // tm,),
        in_specs=[pl.BlockSpec((tm, 2 * D_MODEL), lambda i: (i, 0))],
        out_specs=[
            rows_spec,
            rows_spec,
            pl.BlockSpec((None, H_B, tm, DH_B), lambda i: (i // tiles_per_b, 0, i % tiles_per_b, 0)),
            pl.BlockSpec((None, H_B, DH_B, tm), lambda i: (i // tiles_per_b, 0, 0, i % tiles_per_b)),
        ],
        out_shape=[
            jax.ShapeDtypeStruct((r * H_B, DH_B), F32),
            jax.ShapeDtypeStruct((r * H_B, DH_B), F32),
            jax.ShapeDtypeStruct((b, H_B, t, DH_B), BF16),
            jax.ShapeDtypeStruct((b, H_B, DH_B, t), BF16),
        ],
        compiler_params=_params("parallel"),
        name="split_heads",
    )(kv)
    return k_out.reshape(b, t, H_B, DH_B), v_out.reshape(b, t, H_B, DH_B), k_hm, vt_hm


def _rope_tables(pos):
    half = DK_A // 2
    inv = 1.0 / (ROPE_BASE ** jnp.linspace(0.0, 1.0, half, dtype=F32))
    ang = pos.astype(F32)[:, None] * inv[None, :]
    return jnp.cos(ang), jnp.sin(ang)


def _retention_tables(chunk):
    log_g = jnp.log1p(-(2.0 ** (-5.0 - jnp.arange(H_A, dtype=F32))))
    i = jnp.arange(chunk, dtype=F32)
    diff = i[:, None] - i[None, :]
    decay = jnp.where(diff >= 0, jnp.exp(jnp.maximum(diff, 0.0)[None] * log_g[:, None, None]), 0.0)
    cross_scale = jnp.exp((i + 1.0)[:, None] * log_g[None, :]).T[:, :, None]
    wk = jnp.exp((chunk - 1.0 - i)[:, None] * log_g[None, :]).T[:, :, None]
    state_decay = jnp.exp(chunk * log_g)
    return decay, cross_scale, wk, state_decay


def _retention_chunk(q, k, v, g, cos, sin, decay, cross_scale, wk, state_decay, s):
    half = DK_A // 2

    def rot(x):
        x1 = x[:, :half]
        x2 = x[:, half:]
        return jnp.concatenate([x1 * cos - x2 * sin, x1 * sin + x2 * cos], axis=-1)

    q = rot(q)
    k = rot(k) * (DK_A ** -0.5)
    vb = v.astype(BF16)
    qb = q.astype(BF16)
    kb = k.astype(BF16)
    scores = lax.dot_general(qb, kb, (((1,), (1,)), ((), ())), preferred_element_type=F32) * decay
    inner = jnp.dot(scores.astype(BF16), vb, preferred_element_type=F32)
    cross = jnp.dot(qb, s.astype(BF16), preferred_element_type=F32) * cross_scale
    kw = (k * wk).astype(BF16)
    upd = lax.dot_general(kw, vb, (((0,), (0,)), ((), ())), preferred_element_type=F32)
    s_new = state_decay * s + upd
    o = inner + cross
    o = o * lax.rsqrt(jnp.mean(o * o, axis=-1, keepdims=True) + EPS)
    return g * jax.nn.sigmoid(g) * o, s_new


RET_HEADS_PER_STEP = 4
RET_CHUNKS_PER_STEP = 1


def _retention_kernel(sdec_ref, q_ref, k_ref, v_ref, g_ref, cos_ref, sin_ref, dec_ref, cs_ref,
                      wk_ref, *refs, chunk, heads, chunks, n_steps, zero_state):
    a_ref, sout_ref, s_scr = refs[-3:]
    hg = pl.program_id(1)
    c = pl.program_id(2)

    @pl.when(c == 0)
    def _():
        s_scr[...] = jnp.zeros(s_scr.shape, F32) if zero_state else refs[0][...]

    for hh in range(heads):
        s = s_scr[hh]
        kcols = slice(hh * DK_A, (hh + 1) * DK_A)
        vcols = slice(hh * DV_A, (hh + 1) * DV_A)
        for ci in range(chunks):
            rows = slice(ci * chunk, (ci + 1) * chunk)
            a, s = _retention_chunk(
                q_ref[rows, kcols], k_ref[rows, kcols], v_ref[rows, vcols], g_ref[rows, vcols],
                cos_ref[rows, :], sin_ref[rows, :], dec_ref[hh], cs_ref[hh], wk_ref[hh],
                sdec_ref[hg * heads + hh], s)
            a_ref[rows, vcols] = a.astype(a_ref.dtype)
        s_scr[hh] = s

        @pl.when(c == n_steps - 1)
        def _():
            sout_ref[hh] = s


def _retention(qkvg, s_in, layer, pos, out_dtype, s_out_prev):
    b, t, _ = qkvg.shape
    chunk = math.gcd(t, RET_CHUNK)
    n_chunks = t // chunk
    heads = RET_HEADS_PER_STEP
    chunks = math.gcd(n_chunks, RET_CHUNKS_PER_STEP)
    n_steps = n_chunks // chunks
    rows = chunk * chunks
    assert H_A % heads == 0
    cos, sin = _rope_tables(pos)
    decay, cross_scale, wk, state_decay = _retention_tables(chunk)
    k_off = QK_A // (heads * DK_A)
    v_off = 2 * QK_A // (heads * DV_A)
    g_off = (2 * QK_A + V_A) // (heads * DV_A)
    kernel = functools.partial(_retention_kernel, chunk=chunk, heads=heads, chunks=chunks, n_steps=n_steps,
                               zero_state=s_in is None)
    state_spec = pl.BlockSpec((None, None, heads, DK_A, DV_A), lambda bi, h, c: (layer, bi, h, 0, 0))
    state_args, state_specs, aliases = [], [], {}
    if s_in is not None:
        state_args.append(s_in)
        state_specs.append(state_spec)
    if s_out_prev is not None:
        aliases = {10 + len(state_args): 1}
        state_args.append(s_out_prev)
        state_specs.append(pl.BlockSpec(memory_space=pl.ANY))
    return pl.pallas_call(
        kernel,
        grid=(b, H_A // heads, n_steps),
        in_specs=[
            pl.BlockSpec(memory_space=pltpu.SMEM),
            pl.BlockSpec((None, rows, heads * DK_A), lambda bi, h, c: (bi, c, h)),
            pl.BlockSpec((None, rows, heads * DK_A), lambda bi, h, c: (bi, c, k_off + h)),
            pl.BlockSpec((None, rows, heads * DV_A), lambda bi, h, c: (bi, c, v_off + h)),
            pl.BlockSpec((None, rows, heads * DV_A), lambda bi, h, c: (bi, c, g_off + h)),
            pl.BlockSpec((rows, DK_A // 2), lambda bi, h, c: (c, 0)),
            pl.BlockSpec((rows, DK_A // 2), lambda bi, h, c: (c, 0)),
            pl.BlockSpec((heads, chunk, chunk), lambda bi, h, c: (h, 0, 0)),
            pl.BlockSpec((heads, chunk, 1), lambda bi, h, c: (h, 0, 0)),
            pl.BlockSpec((heads, chunk, 1), lambda bi, h, c: (h, 0, 0)),
        ] + state_specs,
        out_specs=[
            pl.BlockSpec((None, rows, heads * DV_A), lambda bi, h, c: (bi, c, h)),
            state_spec,
        ],
        out_shape=[
            jax.ShapeDtypeStruct((b, t, V_A), out_dtype),
            jax.ShapeDtypeStruct((N_A, b, H_A, DK_A, DV_A), F32),
        ],
        scratch_shapes=[pltpu.VMEM((heads, DK_A, DV_A), F32)],
        input_output_aliases=aliases,
        compiler_params=_params("parallel", "parallel", "arbitrary"),
        name="retention",
    )(state_decay, qkvg, qkvg, qkvg, qkvg, cos, sin, decay, cross_scale, wk, *state_args)


PROJ_RET_ROWS = 1024


def _proj_retention_kernel(sdec_ref, h_ref, wq_ref, wkey_ref, wv_ref, wg_ref, cos_ref, sin_ref, dec_ref,
                           cs_ref, kdecay_ref, *refs, chunk, n_steps):
    a_ref, sout_ref, qkvg_scr, s_scr = refs[-4:]
    head = pl.program_id(1)
    r = pl.program_id(2)

    @pl.when(r == 0)
    def _():
        s_scr[...] = jnp.zeros(s_scr.shape, F32)

    hb = h_ref[...]
    col = 0
    for w_ref in (wq_ref, wkey_ref, wv_ref, wg_ref):
        width = w_ref.shape[1]
        qkvg_scr[:, col:col + width] = jnp.dot(hb, w_ref[...].astype(BF16), preferred_element_type=F32)
        col += width

    s = s_scr[...]
    for ci in range(h_ref.shape[0] // chunk):
        rows = slice(ci * chunk, (ci + 1) * chunk)
        a, s = _retention_chunk(
            qkvg_scr[rows, 0:DK_A], qkvg_scr[rows, DK_A:2 * DK_A],
            qkvg_scr[rows, 2 * DK_A:2 * DK_A + DV_A], qkvg_scr[rows, 2 * DK_A + DV_A:],
            cos_ref[rows, :], sin_ref[rows, :], dec_ref[...], cs_ref[...], kdecay_ref[...],
            sdec_ref[head], s)
        a_ref[rows, :] = a.astype(a_ref.dtype)
    s_scr[...] = s

    @pl.when(r == n_steps - 1)
    def _():
        sout_ref[...] = s


def _proj_retention(h_act, w_in, layer, b, t, pos, out_dtype, s_out_prev):
    chunk = math.gcd(t, RET_CHUNK)
    rows = min(PROJ_RET_ROWS, t)
    assert t % rows == 0 and rows % chunk == 0
    n_steps = t // rows
    cos, sin = _rope_tables(pos)
    decay, cross_scale, wk, state_decay = _retention_tables(chunk)
    k_off = QK_A // DK_A
    v_off = 2 * QK_A // DV_A
    g_off = (2 * QK_A + V_A) // DV_A
    d = h_act.shape[1]
    kernel = functools.partial(_proj_retention_kernel, chunk=chunk, n_steps=n_steps)
    state_spec = pl.BlockSpec((None, None, None, DK_A, DV_A), lambda bi, h, r: (layer, bi, h, 0, 0))
    extra_args, extra_specs, aliases = [], [], {}
    if s_out_prev is not None:
        aliases = {11: 1}
        extra_args.append(s_out_prev)
        extra_specs.append(pl.BlockSpec(memory_space=pl.ANY))
    return pl.pallas_call(
        kernel,
        grid=(b, H_A, n_steps),
        in_specs=[
            pl.BlockSpec(memory_space=pltpu.SMEM),
            pl.BlockSpec((rows, d), lambda bi, h, r: (bi * n_steps + r, 0)),
            pl.BlockSpec((None, d, DK_A), lambda bi, h, r: (layer, 0, h)),
            pl.BlockSpec((None, d, DK_A), lambda bi, h, r: (layer, 0, k_off + h)),
            pl.BlockSpec((None, d, DV_A), lambda bi, h, r: (layer, 0, v_off + h)),
            pl.BlockSpec((None, d, DV_A), lambda bi, h, r: (layer, 0, g_off + h)),
            pl.BlockSpec((rows, DK_A // 2), lambda bi, h, r: (r, 0)),
            pl.BlockSpec((rows, DK_A // 2), lambda bi, h, r: (r, 0)),
            pl.BlockSpec((None, chunk, chunk), lambda bi, h, r: (h, 0, 0)),
            pl.BlockSpec((None, chunk, 1), lambda bi, h, r: (h, 0, 0)),
            pl.BlockSpec((None, chunk, 1), lambda bi, h, r: (h, 0, 0)),
        ] + extra_specs,
        out_specs=[
            pl.BlockSpec((None, rows, DV_A), lambda bi, h, r: (bi, r, h)),
            state_spec,
        ],
        out_shape=[
            jax.ShapeDtypeStruct((b, t, V_A), out_dtype),
            jax.ShapeDtypeStruct((N_A, b, H_A, DK_A, DV_A), F32),
        ],
        scratch_shapes=[
            pltpu.VMEM((rows, 2 * DK_A + 2 * DV_A), F32),
            pltpu.VMEM((DK_A, DV_A), F32),
        ],
        input_output_aliases=aliases,
        compiler_params=_params("parallel", "parallel", "arbitrary"),
        name="proj_retention",
    )(state_decay, h_act, w_in, w_in, w_in, w_in, cos, sin, decay, cross_scale, wk, *extra_args)


def _select_topk(gate, n_sel, axis=1):
    pos = lax.broadcasted_iota(jnp.int32, gate.shape, axis).astype(F32)
    picks = []
    for _ in range(n_sel):
        m = jnp.max(gate, axis=axis, keepdims=True)
        idx = jnp.min(jnp.where(gate == m, pos, float(gate.shape[axis])), axis=axis, keepdims=True)
        valid = m > -jnp.inf
        picks.append((idx, valid))
        gate = jnp.where((pos == idx) & valid, -jnp.inf, gate)
    return picks


MOBA_HEADS_PER_STEP = 2


def _moba_prompt_kernel(q_ref, g_ref, k_ref, vt_ref, a_ref, *, n_blocks, heads):
    for hh in range(heads):
        cols = slice(hh * DH_B, (hh + 1) * DH_B)
        _moba_prompt_head(q_ref, g_ref, k_ref.at[hh], vt_ref.at[hh], a_ref, cols, n_blocks)


def _moba_prompt_head(q_ref, g_ref, k_ref, vt_ref, a_ref, cols, n_blocks):
    bs = MOBA_BLOCK
    scale = DH_B ** -0.5
    n_sel = min(MOBA_TOPK, n_blocks)
    kb = k_ref[...]
    ones_rows = 16
    vt = jnp.concatenate([vt_ref[...], jnp.ones((ones_rows, kb.shape[0]), BF16)], axis=0)
    means_b = jnp.mean(kb.astype(F32).reshape(n_blocks, bs, DH_B), axis=1).astype(BF16)
    key = lax.broadcasted_iota(jnp.int32, (bs, bs), 0)
    qry = lax.broadcasted_iota(jnp.int32, (bs, bs), 1)
    causal = key <= qry

    for i in range(n_blocks):
        rows = slice(i * bs, (i + 1) * bs)
        q = q_ref[rows, cols]
        qs = (q * (scale * LOG2_E)).astype(BF16)
        picks = None
        if i > n_sel:
            gate = lax.dot_general(means_b, q.astype(BF16), (((1,), (1,)), ((), ())),
                                   preferred_element_type=F32)
            blk = lax.broadcasted_iota(jnp.int32, gate.shape, 0)
            picks = _select_topk(jnp.where(blk < i, gate, -jnp.inf), n_sel, axis=0)
        blocks = []
        for j in range(i + 1):
            sj = lax.dot_general(kb[j * bs:(j + 1) * bs, :], qs, (((1,), (1,)), ((), ())),
                                 preferred_element_type=F32)
            if j == i:
                sj = jnp.where(causal, sj, -jnp.inf)
            elif picks is not None:
                chosen = picks[0][0] == float(j)
                for idx, _ in picks[1:]:
                    chosen = chosen | (idx == float(j))
                sj = jnp.where(chosen, sj, -jnp.inf)
            blocks.append(sj)
        m = jnp.max(functools.reduce(jnp.maximum, blocks), axis=0, keepdims=True)
        acc = jnp.zeros((DH_B + ones_rows, bs), F32)
        for j, sj in enumerate(blocks):
            p = jnp.exp2(sj - m).astype(BF16)
            acc = acc + jnp.dot(vt[:, j * bs:(j + 1) * bs], p, preferred_element_type=F32)
        o = (acc[:DH_B, :] / acc[DH_B:DH_B + 1, :]).T
        g = g_ref[rows, cols]
        a_ref[rows, cols] = (g * jax.nn.sigmoid(g) * o).astype(a_ref.dtype)


def _moba_prompt(qg, k_hm, vt_hm, out_dtype):
    b, t, _ = qg.shape
    n_blocks = t // MOBA_BLOCK
    heads = MOBA_HEADS_PER_STEP
    assert n_blocks * MOBA_BLOCK == t and n_blocks <= 128 and H_B % heads == 0
    g_off = H_B // heads
    kernel = functools.partial(_moba_prompt_kernel, n_blocks=n_blocks, heads=heads)
    return pl.pallas_call(
        kernel,
        grid=(b, H_B // heads),
        in_specs=[
            pl.BlockSpec((None, t, heads * DH_B), lambda bi, h: (bi, 0, h)),
            pl.BlockSpec((None, t, heads * DH_B), lambda bi, h: (bi, 0, g_off + h)),
            pl.BlockSpec((None, heads, t, DH_B), lambda bi, h: (bi, h, 0, 0)),
            pl.BlockSpec((None, heads, DH_B, t), lambda bi, h: (bi, h, 0, 0)),
        ],
        out_specs=pl.BlockSpec((None, t, heads * DH_B), lambda bi, h: (bi, 0, h)),
        out_shape=jax.ShapeDtypeStruct((b, t, D_MODEL), out_dtype),
        compiler_params=_params("parallel", "parallel"),
        name="moba_prompt",
    )(qg, qg, k_hm, vt_hm)


HEAD_GROUP = 8


GATHER_PAGES_PER_STEP = 4


def _gather_cache_kernel(pt_ref, *refs, page, n_groups, pages_per_step, pages_per_block):
    n_slabs = pages_per_step * n_groups
    o_ref, m_ref = refs[2 * n_slabs], refs[2 * n_slabs + 1]
    for part in range(2):
        lanes = slice(part * DH_B, (part + 1) * DH_B)
        for pg in range(pages_per_step):
            for grp in range(n_groups):
                src = refs[part * n_slabs + pg * n_groups + grp]
                src = src.reshape(page * HEAD_GROUP, DH_B)
                for hg in range(HEAD_GROUP):
                    rows = src[pl.ds(hg, page, stride=HEAD_GROUP), :]
                    o_ref[grp * HEAD_GROUP + hg, pg * page:(pg + 1) * page, lanes] = rows.astype(BF16)
    for blk in range(pages_per_step // pages_per_block):
        for grp in range(n_groups):
            total = None
            for pg in range(blk * pages_per_block, (blk + 1) * pages_per_block):
                part_sum = jnp.sum(refs[pg * n_groups + grp][...], axis=0)
                total = part_sum if total is None else total + part_sum
            m_ref[blk, grp * HEAD_GROUP:(grp + 1) * HEAD_GROUP, :] = total / MOBA_BLOCK


def _gather_cache(cache_k, cache_v, page_table_flat, n_seq, n_pages):
    n_pool, page, n_h, dh = cache_k.shape
    pages_per_block = MOBA_BLOCK // page
    pps = GATHER_PAGES_PER_STEP
    assert n_h == H_B and dh == DH_B and n_h % HEAD_GROUP == 0
    assert pps % pages_per_block == 0 and n_pages % pps == 0
    n_blocks = n_pages // pages_per_block
    n_groups = n_h // HEAD_GROUP
    grouped = [c.reshape(n_pool, page, n_groups, HEAD_GROUP, dh) for c in (cache_k, cache_v)]

    def page_map(pg, grp):
        return lambda bi, j, pt: (pt[bi * n_pages + pps * j + pg], 0, grp, 0, 0)

    def slab_specs():
        return [pl.BlockSpec((None, page, None, HEAD_GROUP, dh), page_map(pg, grp))
                for pg in range(pps) for grp in range(n_groups)]

    n_slabs = pps * n_groups
    blocks_per_step = pps // pages_per_block
    kernel = functools.partial(_gather_cache_kernel, page=page, n_groups=n_groups, pages_per_step=pps,
                               pages_per_block=pages_per_block)
    return pl.pallas_call(
        kernel,
        grid_spec=pltpu.PrefetchScalarGridSpec(
            num_scalar_prefetch=1,
            grid=(n_seq, n_pages // pps),
            in_specs=slab_specs() + slab_specs(),
            out_specs=[
                pl.BlockSpec((None, n_h, pps * page, 2 * dh), lambda bi, j, pt: (bi, 0, j, 0)),
                pl.BlockSpec((None, blocks_per_step, n_h, dh), lambda bi, j, pt: (bi, j, 0, 0)),
            ],
        ),
        out_shape=[
            jax.ShapeDtypeStruct((n_seq, n_h, n_pages * page, 2 * dh), BF16),
            jax.ShapeDtypeStruct((n_seq, n_blocks, n_h, dh), F32),
        ],
        compiler_params=_params("parallel", "arbitrary"),
        name="gather_cache",
    )(page_table_flat, *([grouped[0]] * n_slabs), *([grouped[1]] * n_slabs))


def _moba_select_kernel(q_ref, means_ref, o_ref, *, t, n_blocks, past):
    q = q_ref[...]
    rows = t * H_B
    qrep = jnp.concatenate([jnp.broadcast_to(q[ti:ti + 1, :], (H_B, D_MODEL)) for ti in range(t)], axis=0)
    lane_h = lax.broadcasted_iota(jnp.int32, (rows, D_MODEL), 1) >> (DH_B.bit_length() - 1)
    row_h = lax.broadcasted_iota(jnp.int32, (rows, D_MODEL), 0) & (H_B - 1)
    qexp = jnp.where(lane_h == row_h, qrep, 0.0)
    means = jnp.concatenate([means_ref[...], jnp.zeros((128 - n_blocks, D_MODEL), F32)], axis=0)
    gate = lax.dot_general(qexp.astype(BF16), means.astype(BF16), (((1,), (1,)), ((), ())),
                           preferred_element_type=F32)
    lane = lax.broadcasted_iota(jnp.int32, gate.shape, 1)
    token = lax.broadcasted_iota(jnp.int32, gate.shape, 0) >> (H_B.bit_length() - 1)
    own = (past + token) >> (MOBA_BLOCK.bit_length() - 1)
    gate = jnp.where(lane < own, gate, -jnp.inf)
    out = jnp.zeros(gate.shape, F32)
    for r, (idx, _) in enumerate(_select_topk(gate, MOBA_TOPK)):
        out = jnp.where(lane == r, idx, out)
    o_ref[...] = out.astype(jnp.int32)


def _moba_select(qg, means, past):
    b, t, _ = qg.shape
    n_blocks = means.shape[1]
    assert MOBA_TOPK <= n_blocks <= 128
    kernel = functools.partial(_moba_select_kernel, t=t, n_blocks=n_blocks, past=past)
    return pl.pallas_call(
        kernel,
        grid=(b,),
        in_specs=[
            pl.BlockSpec((None, t, D_MODEL), lambda bi: (bi, 0, 0)),
            pl.BlockSpec((None, n_blocks, D_MODEL), lambda bi: (bi, 0, 0)),
        ],
        out_specs=pl.BlockSpec((None, t * H_B, 128), lambda bi: (bi, 0, 0)),
        out_shape=jax.ShapeDtypeStruct((b, t * H_B, 128), jnp.int32),
        compiler_params=_params("parallel"),
        name="moba_select",
    )(qg, means)


SAMPLE_HEADS_PER_STEP = 4


def _moba_sample_kernel(sel_ref, new_ref, *refs, t, n_slots, heads):
    a_ref = refs[heads * n_slots]
    for hh in range(heads):
        cols = slice(hh * DH_B, (hh + 1) * DH_B)
        _moba_sample_head(new_ref.at[hh], refs[hh * n_slots:(hh + 1) * n_slots], a_ref, cols, t, n_slots)


def _moba_sample_head(new_ref, kv_refs, a_ref, cols, t, n_slots):
    keys_per_token = (n_slots // t) * kv_refs[0].shape[0]

    q = new_ref[:, 0:DH_B]
    qb = q.astype(BF16)
    kall = jnp.concatenate([r[:, :DH_B] for r in kv_refs], axis=0)
    vall = jnp.concatenate([r[:, DH_B:] for r in kv_refs], axis=0)
    s = lax.dot_general(qb, kall, (((1,), (1,)), ((), ())), preferred_element_type=F32) * (DH_B ** -0.5)
    row = lax.broadcasted_iota(jnp.int32, s.shape, 0)
    col = lax.broadcasted_iota(jnp.int32, s.shape, 1)
    mine = (col >= row * keys_per_token) & (col < (row + 1) * keys_per_token)
    s = jnp.where(mine, s, -jnp.inf)

    kn = new_ref[:, 2 * DH_B:3 * DH_B]
    vn = new_ref[:, 3 * DH_B:4 * DH_B]
    rown = lax.broadcasted_iota(jnp.int32, (t, 1), 0)
    s_own = []
    for tj in range(t):
        sj = jnp.sum(q * kn[tj:tj + 1, :], axis=-1, keepdims=True) * (DH_B ** -0.5)
        s_own.append(jnp.where(rown >= tj, sj, -jnp.inf))
    m = jnp.max(s, axis=-1, keepdims=True)
    for sj in s_own:
        m = jnp.maximum(m, sj)
    p = jnp.exp(s - m)
    l = jnp.sum(p, axis=-1, keepdims=True)
    acc = jnp.dot(p.astype(BF16), vall, preferred_element_type=F32)
    for tj, sj in enumerate(s_own):
        pj = jnp.exp(sj - m)
        l = l + pj
        acc = acc + pj * vn[tj:tj + 1, :]
    o = acc / l
    g = new_ref[:, DH_B:2 * DH_B]
    a_ref[:, cols] = (g * jax.nn.sigmoid(g) * o).astype(a_ref.dtype)


def _moba_sample(qg, kv_new, past_kv, sel_flat):
    b, t, _ = qg.shape
    n_slots = t * MOBA_TOPK
    heads = SAMPLE_HEADS_PER_STEP
    assert H_B % heads == 0
    per_head = [a.reshape(b, t, H_B, DH_B) for a in
                (qg[:, :, :D_MODEL], qg[:, :, D_MODEL:], kv_new[:, :, :D_MODEL], kv_new[:, :, D_MODEL:])]
    new_rows = jnp.concatenate(per_head, axis=-1).transpose(0, 2, 1, 3)

    def slot_map(hh, ti, r):
        def index_map(bi, hg, sel):
            h = hg * heads + hh
            return (bi, h, sel[((bi * t + ti) * H_B + h) * MOBA_TOPK + r], 0)
        return index_map

    slot_specs = [pl.BlockSpec((None, None, MOBA_BLOCK, 2 * DH_B), slot_map(hh, ti, r))
                  for hh in range(heads) for ti in range(t) for r in range(MOBA_TOPK)]

    kernel = functools.partial(_moba_sample_kernel, t=t, n_slots=n_slots, heads=heads)
    return pl.pallas_call(
        kernel,
        grid_spec=pltpu.PrefetchScalarGridSpec(
            num_scalar_prefetch=1,
            grid=(b, H_B // heads),
            in_specs=[pl.BlockSpec((None, heads, t, 4 * DH_B), lambda bi, hg, sel: (bi, hg, 0, 0))] + slot_specs,
            out_specs=pl.BlockSpec((None, t, heads * DH_B), lambda bi, hg, sel: (bi, 0, hg)),
        ),
        out_shape=jax.ShapeDtypeStruct((b, t, D_MODEL), F32),
        compiler_params=_params("parallel", "arbitrary"),
        name="moba_sample",
    )(sel_flat, new_rows, *([past_kv] * len(slot_specs)))


def _run_group(x, mods, mods_kv, mods_f, s_in, past, weights, paged):
    (norm_g, w_in_a, w_out_a, w_q_b, w_o_b, kv_norm_g, w_kv, final_g) = weights
    b, t, d = x.shape
    rows = b * t
    prompt = paged is None
    pos = past + jnp.arange(t, dtype=jnp.int32)
    if prompt:
        tm, tn, tm_norm, rpb = 1024, 1024, 512, t
        act_dtype = BF16
        expand = lambda v: v[:, None, :]
    else:
        tm, tn, tm_norm, rpb = rows, 1024, rows, 0
        act_dtype = F32
        expand = lambda v: jnp.repeat(v, t, axis=0)

    def project(x2, g, shift, scale, w, layer):
        if prompt:
            return _norm_matmul(x2, g[None, :], shift, scale, w, layer, rpb, tm, tn)
        h = _norm_mod(x2, g[None, :], shift, scale, rpb, act_dtype, tm_norm)
        return _matmul(h, w, layer, tm, tn)

    x2 = x.reshape(rows, d)
    new_s = None
    kv = k_new = v_new = k_hm = vt_hm = None
    for l in range(DEPTH):
        shift, scale, gate = (expand(mods[l][:, i * d:(i + 1) * d]) for i in range(3))
        if l < N_A and prompt:
            h = _norm_mod(x2, norm_g[l][None, :], shift, scale, rpb, BF16, tm_norm)
            a, new_s = _proj_retention(h, w_in_a, l, b, t, pos, act_dtype, new_s)
            x2 = _matmul(a.reshape(rows, V_A), w_out_a, l, tm, tn, res=x2, gate=gate, rows_per_batch=rpb)
        elif l < N_A:
            qkvg = project(x2, norm_g[l], shift, scale, w_in_a, l).reshape(b, t, -1)
            a, new_s = _retention(qkvg, s_in, l, pos, act_dtype, new_s)
            x2 = _matmul(a.reshape(rows, V_A), w_out_a, l, tm, tn, res=x2, gate=gate, rows_per_batch=rpb)
        else:
            lb = l - N_A
            qg = project(x2, norm_g[l], shift, scale, w_q_b, lb).reshape(b, t, -1)
            if prompt:
                a = _moba_prompt(qg, k_hm, vt_hm, act_dtype)
            else:
                past_kv, means = paged
                sel = _moba_select(qg, means, past)[:, :, :MOBA_TOPK].reshape(-1)
                a = _moba_sample(qg, kv.reshape(b, t, -1), past_kv, sel)
            x2 = _matmul(a.reshape(rows, d), w_o_b, lb, tm, tn, res=x2, gate=gate, rows_per_batch=rpb)
        if l == N_A - 1:
            kv_shift, kv_scale = (expand(mods_kv[:, i * d:(i + 1) * d]) for i in range(2))
            kv = project(x2, kv_norm_g, kv_shift, kv_scale, w_kv[None], 0)
            if prompt:
                k_new, v_new, k_hm, vt_hm = _split_heads(kv, b, t)
            else:
                k_new = kv[:, :d].reshape(b, t, H_B, DH_B)
                v_new = kv[:, d:].reshape(b, t, H_B, DH_B)
    f_shift, f_scale = (expand(mods_f[:, i * d:(i + 1) * d]) for i in range(2))
    y = _norm_mod(x2, final_g[None, :], f_shift, f_scale, rpb, F32, tm_norm)
    return y.reshape(b, t, d), new_s, k_new, v_new


def kernel(x_prompt, x_sample, state_ret, cache_k, cache_v, page_table, c_prompt, c_sample,
           norm_g, w_mod, b_mod, w_in_a, w_out_a, w_q_b, w_o_b,
           kv_norm_g, w_mod_kv, b_mod_kv, w_kv, final_g, w_mod_f, b_mod_f):
    bp = x_prompt.shape[0]
    bd, n_pages = page_table.shape
    n_pool, page, _, _ = cache_k.shape
    past_len = n_pages * page
    assert bp + bd <= MOD_ROWS

    c_all = jnp.concatenate([c_prompt, c_sample, jnp.zeros((MOD_ROWS - bp - bd, D_MODEL), F32)], axis=0)
    mods = _mod_matmul(c_all, w_mod, b_mod[:, None, :])
    mods_kv = _mod_matmul(c_all, w_mod_kv[None], b_mod_kv[None, None, :])[0]
    mods_f = _mod_matmul(c_all, w_mod_f[None], b_mod_f[None, None, :])[0]

    weights = (norm_g, w_in_a, w_out_a, w_q_b, w_o_b, kv_norm_g, w_kv, final_g)

    y_p, s_p, k_p, v_p = _run_group(
        x_prompt, mods[:, :bp], mods_kv[:bp], mods_f[:bp], None, 0, weights, None)

    pt_flat = page_table.reshape(-1)
    past_kv, means = _gather_cache(cache_k, cache_v, pt_flat, bd, n_pages)
    means = means.reshape(bd, -1, H_B * DH_B)
    y_s, s_s, k_s, v_s = _run_group(
        x_sample, mods[:, bp:bp + bd], mods_kv[bp:bp + bd], mods_f[bp:bp + bd], state_ret, past_len,
        weights, (past_kv, means))
    return (y_p, y_s, s_p, s_s, k_p, v_p, k_s, v_s)
```

```python
import functools
import math

import jax
import jax.numpy as jnp
from jax import lax
from jax.experimental import pallas as pl
from jax.experimental.pallas import tpu as pltpu

F32 = jnp.float32
BF16 = jnp.bfloat16

D_MODEL = 2048
DEPTH = 4
N_A = DEPTH // 2
H_A = 8
DK_A = D_MODEL // H_A
DV_A = 2 * DK_A
QK_A = H_A * DK_A
V_A = H_A * DV_A
RET_CHUNK = 256
H_B = 16
DH_B = D_MODEL // H_B
MOBA_BLOCK = 256
MOBA_TOPK = 3
ROPE_BASE = 10000.0
EPS = 1e-6
LOG2_E = math.log2(math.e)

VMEM_LIMIT_BYTES = 56 * 1024 * 1024
MOD_ROWS = 16


def _params(*sem):
    return pltpu.CompilerParams(dimension_semantics=sem, vmem_limit_bytes=VMEM_LIMIT_BYTES)


def _mod_kernel(c_ref, w_ref, b_ref, o_ref):
    acc = jnp.dot(c_ref[...].astype(BF16), w_ref[...].astype(BF16), preferred_element_type=F32)
    o_ref[...] = acc + b_ref[...]


def _mod_matmul(c, w, b, tn=1024):
    n_l, d, n = w.shape
    return pl.pallas_call(
        _mod_kernel,
        grid=(n_l, n // tn),
        in_specs=[
            pl.BlockSpec((MOD_ROWS, d), lambda l, j: (0, 0)),
            pl.BlockSpec((None, d, tn), lambda l, j: (l, 0, j)),
            pl.BlockSpec((None, 1, tn), lambda l, j: (l, 0, j)),
        ],
        out_specs=pl.BlockSpec((None, MOD_ROWS, tn), lambda l, j: (l, 0, j)),
        out_shape=jax.ShapeDtypeStruct((n_l, MOD_ROWS, n), F32),
        compiler_params=_params("parallel", "parallel"),
        name="mod_matmul",
    )(c, w, b)


def _norm_mod_kernel(x_ref, g_ref, sh_ref, sc_ref, o_ref):
    x = x_ref[...]
    r = lax.rsqrt(jnp.mean(x * x, axis=-1, keepdims=True) + EPS)
    y = (x * r) * g_ref[...]
    o_ref[...] = (y * (1.0 + sc_ref[...]) + sh_ref[...]).astype(o_ref.dtype)


def _norm_mod(x, g, shift, scale, rows_per_batch, out_dtype, tm):
    r, d = x.shape
    if rows_per_batch:
        tiles_per_b = rows_per_batch // tm
        mod_spec = pl.BlockSpec((None, 1, d), lambda i: (i // tiles_per_b, 0, 0))
    else:
        mod_spec = pl.BlockSpec((tm, d), lambda i: (i, 0))
    return pl.pallas_call(
        _norm_mod_kernel,
        grid=(r // tm,),
        in_specs=[
            pl.BlockSpec((tm, d), lambda i: (i, 0)),
            pl.BlockSpec((1, d), lambda i: (0, 0)),
            mod_spec,
            mod_spec,
        ],
        out_specs=pl.BlockSpec((tm, d), lambda i: (i, 0)),
        out_shape=jax.ShapeDtypeStruct((r, d), out_dtype),
        compiler_params=_params("parallel"),
        name="norm_mod",
    )(x, g, shift, scale)


def _matmul_kernel(a_ref, w_ref, o_ref):
    o_ref[...] = jnp.dot(
        a_ref[...].astype(BF16), w_ref[...].astype(BF16), preferred_element_type=F32
    ).astype(o_ref.dtype)


def _matmul_res_kernel(a_ref, w_ref, x_ref, gm_ref, o_ref):
    acc = jnp.dot(a_ref[...].astype(BF16), w_ref[...].astype(BF16), preferred_element_type=F32)
    o_ref[...] = x_ref[...] + gm_ref[...] * acc


MATMUL_W_TILE_ELEMS = 2 * 1024 * 1024


def _matmul(a, w, layer, tm, tn, res=None, gate=None, rows_per_batch=0):
    r, k = a.shape
    n = w.shape[2]
    tn = min(tn, MATMUL_W_TILE_ELEMS // k)
    in_specs = [
        pl.BlockSpec((tm, k), lambda i, j: (i, 0)),
        pl.BlockSpec((None, k, tn), lambda i, j: (layer, 0, j)),
    ]
    args = [a, w]
    kernel = _matmul_kernel
    if res is not None:
        kernel = _matmul_res_kernel
        in_specs.append(pl.BlockSpec((tm, tn), lambda i, j: (i, j)))
        if rows_per_batch:
            tiles_per_b = rows_per_batch // tm
            in_specs.append(pl.BlockSpec((None, 1, tn), lambda i, j: (i // tiles_per_b, 0, j)))
        else:
            in_specs.append(pl.BlockSpec((tm, tn), lambda i, j: (i, j)))
        args += [res, gate]
    return pl.pallas_call(
        kernel,
        grid=(r // tm, n // tn),
        in_specs=in_specs,
        out_specs=pl.BlockSpec((tm, tn), lambda i, j: (i, j)),
        out_shape=jax.ShapeDtypeStruct((r, n), F32),
        compiler_params=_params("parallel", "parallel"),
        name="matmul_res" if res is not None else "matmul",
    )(*args)


def _norm_matmul_kernel(x_ref, g_ref, sh_ref, sc_ref, w_ref, o_ref, h_scr):
    @pl.when(pl.program_id(1) == 0)
    def _():
        x = x_ref[...]
        r = lax.rsqrt(jnp.mean(x * x, axis=-1, keepdims=True) + EPS)
        y = (x * r) * g_ref[...]
        h_scr[...] = (y * (1.0 + sc_ref[...]) + sh_ref[...]).astype(BF16)

    o_ref[...] = jnp.dot(h_scr[...], w_ref[...].astype(BF16), preferred_element_type=F32)


def _norm_matmul(x, g, shift, scale, w, layer, rows_per_batch, tm, tn):
    r, k = x.shape
    n = w.shape[2]
    tn = min(tn, MATMUL_W_TILE_ELEMS // k)
    tiles_per_b = rows_per_batch // tm
    mod_spec = pl.BlockSpec((None, 1, k), lambda i, j: (i // tiles_per_b, 0, 0))
    return pl.pallas_call(
        _norm_matmul_kernel,
        grid=(r // tm, n // tn),
        in_specs=[
            pl.BlockSpec((tm, k), lambda i, j: (i, 0)),
            pl.BlockSpec((1, k), lambda i, j: (0, 0)),
            mod_spec,
            mod_spec,
            pl.BlockSpec((None, k, tn), lambda i, j: (layer, 0, j)),
        ],
        out_specs=pl.BlockSpec((tm, tn), lambda i, j: (i, j)),
        out_shape=jax.ShapeDtypeStruct((r, n), F32),
        scratch_shapes=[pltpu.VMEM((tm, k), BF16)],
        compiler_params=_params("parallel", "arbitrary"),
        name="norm_matmul",
    )(x, g, shift, scale, w)


def _split_heads_kernel(kv_ref, k_out_ref, v_out_ref, k_hm_ref, vt_hm_ref, *, tm):
    for h in range(H_B):
        kh = kv_ref[:, h * DH_B:(h + 1) * DH_B]
        vh = kv_ref[:, (H_B + h) * DH_B:(H_B + h + 1) * DH_B]
        k_out_ref[pl.ds(h, tm, stride=H_B), :] = kh
        v_out_ref[pl.ds(h, tm, stride=H_B), :] = vh
        k_hm_ref[h] = kh.astype(BF16)
        vt_hm_ref[h] = vh.T.astype(BF16)


def _split_heads(kv, b, t, tm=256):
    r = b * t
    tiles_per_b = t // tm
    kernel = functools.partial(_split_heads_kernel, tm=tm)
    rows_spec = pl.BlockSpec((tm * H_B, DH_B), lambda i: (i, 0))
    k_out, v_out, k_hm, vt_hm = pl.pallas_call(
        kernel,
        grid=(r // tm,),
        in_specs=[pl.BlockSpec((tm, 2 * D_MODEL), lambda i: (i, 0))],
        out_specs=[
            rows_spec,
            rows_spec,
            pl.BlockSpec((None, H_B, tm, DH_B), lambda i: (i // tiles_per_b, 0, i % tiles_per_b, 0)),
            pl.BlockSpec((None, H_B, DH_B, tm), lambda i: (i // tiles_per_b, 0, 0, i % tiles_per_b)),
        ],
        out_shape=[
            jax.ShapeDtypeStruct((r * H_B, DH_B), F32),
            jax.ShapeDtypeStruct((r * H_B, DH_B), F32),
            jax.ShapeDtypeStruct((b, H_B, t, DH_B), BF16),
            jax.ShapeDtypeStruct((b, H_B, DH_B, t), BF16),
        ],
        compiler_params=_params("parallel"),
        name="split_heads",
    )(kv)
    return k_out.reshape(b, t, H_B, DH_B), v_out.reshape(b, t, H_B, DH_B), k_hm, vt_hm


def _rope_tables(pos):
    half = DK_A // 2
    inv = 1.0 / (ROPE_BASE ** jnp.linspace(0.0, 1.0, half, dtype=F32))
    ang = pos.astype(F32)[:, None] * inv[None, :]
    return jnp.cos(ang), jnp.sin(ang)


def _retention_tables(chunk):
    log_g = jnp.log1p(-(2.0 ** (-5.0 - jnp.arange(H_A, dtype=F32))))
    i = jnp.arange(chunk, dtype=F32)
    diff = i[:, None] - i[None, :]
    decay = jnp.where(diff >= 0, jnp.exp(jnp.maximum(diff, 0.0)[None] * log_g[:, None, None]), 0.0)
    cross_scale = jnp.exp((i + 1.0)[:, None] * log_g[None, :]).T[:, :, None]
    wk = jnp.exp((chunk - 1.0 - i)[:, None] * log_g[None, :]).T[:, :, None]
    state_decay = jnp.exp(chunk * log_g)
    return decay, cross_scale, wk, state_decay


def _retention_chunk(q, k, v, g, cos, sin, decay, cross_scale, wk, state_decay, s):
    half = DK_A // 2

    def rot(x):
        x1 = x[:, :half]
        x2 = x[:, half:]
        return jnp.concatenate([x1 * cos - x2 * sin, x1 * sin + x2 * cos], axis=-1)

    q = rot(q)
    k = rot(k) * (DK_A ** -0.5)
    vb = v.astype(BF16)
    qb = q.astype(BF16)
    kb = k.astype(BF16)
    scores = lax.dot_general(qb, kb, (((1,), (1,)), ((), ())), preferred_element_type=F32) * decay
    inner = jnp.dot(scores.astype(BF16), vb, preferred_element_type=F32)
    cross = jnp.dot(qb, s.astype(BF16), preferred_element_type=F32) * cross_scale
    kw = (k * wk).astype(BF16)
    upd = lax.dot_general(kw, vb, (((0,), (0,)), ((), ())), preferred_element_type=F32)
    s_new = state_decay * s + upd
    o = inner + cross
    o = o * lax.rsqrt(jnp.mean(o * o, axis=-1, keepdims=True) + EPS)
    return g * jax.nn.sigmoid(g) * o, s_new


RET_HEADS_PER_STEP = 4
RET_CHUNKS_PER_STEP = 1


def _retention_kernel(sdec_ref, q_ref, k_ref, v_ref, g_ref, cos_ref, sin_ref, dec_ref, cs_ref,
                      wk_ref, *refs, chunk, heads, chunks, n_steps, zero_state, layer, init_stack):
    a_ref, sout_ref, s_scr = refs[-3:]
    hg = pl.program_id(1)
    c = pl.program_id(2)

    @pl.when(c == 0)
    def _():
        s_scr[...] = jnp.zeros(s_scr.shape, F32) if zero_state else refs[0][...]

    for hh in range(heads):
        s = s_scr[hh]
        kcols = slice(hh * DK_A, (hh + 1) * DK_A)
        vcols = slice(hh * DV_A, (hh + 1) * DV_A)
        for ci in range(chunks):
            rows = slice(ci * chunk, (ci + 1) * chunk)
            a, s = _retention_chunk(
                q_ref[rows, kcols], k_ref[rows, kcols], v_ref[rows, vcols], g_ref[rows, vcols],
                cos_ref[rows, :], sin_ref[rows, :], dec_ref[hh], cs_ref[hh], wk_ref[hh],
                sdec_ref[hg * heads + hh], s)
            a_ref[rows, vcols] = a.astype(a_ref.dtype)
        s_scr[hh] = s

        @pl.when(c == n_steps - 1)
        def _():
            _store_state(sout_ref, hh, s, layer, init_stack)


def _store_state(sout_ref, idx, s, layer, init_stack):
    if not init_stack:
        sout_ref[idx] = s
        return
    for slab in range(sout_ref.shape[0]):
        sout_ref[slab, idx] = s if slab == layer else jnp.zeros(s.shape, s.dtype)


def _retention(qkvg, s_in, layer, pos, out_dtype, s_out_prev):
    b, t, _ = qkvg.shape
    chunk = math.gcd(t, RET_CHUNK)
    n_chunks = t // chunk
    heads = RET_HEADS_PER_STEP
    chunks = math.gcd(n_chunks, RET_CHUNKS_PER_STEP)
    n_steps = n_chunks // chunks
    rows = chunk * chunks
    assert H_A % heads == 0
    cos, sin = _rope_tables(pos)
    decay, cross_scale, wk, state_decay = _retention_tables(chunk)
    k_off = QK_A // (heads * DK_A)
    v_off = 2 * QK_A // (heads * DV_A)
    g_off = (2 * QK_A + V_A) // (heads * DV_A)
    init_stack = s_out_prev is None
    kernel = functools.partial(_retention_kernel, chunk=chunk, heads=heads, chunks=chunks, n_steps=n_steps,
                               zero_state=s_in is None, layer=layer, init_stack=init_stack)
    state_spec = pl.BlockSpec((None, None, heads, DK_A, DV_A), lambda bi, h, c: (layer, bi, h, 0, 0))
    state_args, state_specs, aliases = [], [], {}
    if s_in is not None:
        state_args.append(s_in)
        state_specs.append(state_spec)
    if init_stack:
        out_state_spec = pl.BlockSpec((N_A, None, heads, DK_A, DV_A), lambda bi, h, c: (0, bi, h, 0, 0))
    else:
        out_state_spec = state_spec
        aliases = {10 + len(state_args): 1}
        state_args.append(s_out_prev)
        state_specs.append(pl.BlockSpec(memory_space=pl.ANY))
    return pl.pallas_call(
        kernel,
        grid=(b, H_A // heads, n_steps),
        in_specs=[
            pl.BlockSpec(memory_space=pltpu.SMEM),
            pl.BlockSpec((None, rows, heads * DK_A), lambda bi, h, c: (bi, c, h)),
            pl.BlockSpec((None, rows, heads * DK_A), lambda bi, h, c: (bi, c, k_off + h)),
            pl.BlockSpec((None, rows, heads * DV_A), lambda bi, h, c: (bi, c, v_off + h)),
            pl.BlockSpec((None, rows, heads * DV_A), lambda bi, h, c: (bi, c, g_off + h)),
            pl.BlockSpec((rows, DK_A // 2), lambda bi, h, c: (c, 0)),
            pl.BlockSpec((rows, DK_A // 2), lambda bi, h, c: (c, 0)),
            pl.BlockSpec((heads, chunk, chunk), lambda bi, h, c: (h, 0, 0)),
            pl.BlockSpec((heads, chunk, 1), lambda bi, h, c: (h, 0, 0)),
            pl.BlockSpec((heads, chunk, 1), lambda bi, h, c: (h, 0, 0)),
        ] + state_specs,
        out_specs=[
            pl.BlockSpec((None, rows, heads * DV_A), lambda bi, h, c: (bi, c, h)),
            out_state_spec,
        ],
        out_shape=[
            jax.ShapeDtypeStruct((b, t, V_A), out_dtype),
            jax.ShapeDtypeStruct((N_A, b, H_A, DK_A, DV_A), F32),
        ],
        scratch_shapes=[pltpu.VMEM((heads, DK_A, DV_A), F32)],
        input_output_aliases=aliases,
        compiler_params=_params("parallel", "parallel", "arbitrary"),
        name="retention",
    )(state_decay, qkvg, qkvg, qkvg, qkvg, cos, sin, decay, cross_scale, wk, *state_args)


PROJ_RET_ROWS = 1024


def _proj_retention_kernel(sdec_ref, h_ref, wq_ref, wkey_ref, wv_ref, wg_ref, cos_ref, sin_ref, dec_ref,
                           cs_ref, kdecay_ref, *refs, chunk, n_steps, layer, init_stack):
    a_ref, sout_ref, qkvg_scr, s_scr = refs[-4:]
    head = pl.program_id(1)
    r = pl.program_id(2)

    @pl.when(r == 0)
    def _():
        s_scr[...] = jnp.zeros(s_scr.shape, F32)

    hb = h_ref[...]
    col = 0
    for w_ref in (wq_ref, wkey_ref, wv_ref, wg_ref):
        width = w_ref.shape[1]
        qkvg_scr[:, col:col + width] = jnp.dot(hb, w_ref[...].astype(BF16), preferred_element_type=F32)
        col += width

    s = s_scr[...]
    for ci in range(h_ref.shape[0] // chunk):
        rows = slice(ci * chunk, (ci + 1) * chunk)
        a, s = _retention_chunk(
            qkvg_scr[rows, 0:DK_A], qkvg_scr[rows, DK_A:2 * DK_A],
            qkvg_scr[rows, 2 * DK_A:2 * DK_A + DV_A], qkvg_scr[rows, 2 * DK_A + DV_A:],
            cos_ref[rows, :], sin_ref[rows, :], dec_ref[...], cs_ref[...], kdecay_ref[...],
            sdec_ref[head], s)
        a_ref[rows, :] = a.astype(a_ref.dtype)
    s_scr[...] = s

    @pl.when(r == n_steps - 1)
    def _():
        _store_state(sout_ref, 0, s, layer, init_stack)


def _proj_retention(h_act, w_in, layer, b, t, pos, out_dtype, s_out_prev):
    chunk = math.gcd(t, RET_CHUNK)
    rows = min(PROJ_RET_ROWS, t)
    assert t % rows == 0 and rows % chunk == 0
    n_steps = t // rows
    cos, sin = _rope_tables(pos)
    decay, cross_scale, wk, state_decay = _retention_tables(chunk)
    k_off = QK_A // DK_A
    v_off = 2 * QK_A // DV_A
    g_off = (2 * QK_A + V_A) // DV_A
    d = h_act.shape[1]
    init_stack = s_out_prev is None
    kernel = functools.partial(_proj_retention_kernel, chunk=chunk, n_steps=n_steps, layer=layer,
                               init_stack=init_stack)
    extra_args, extra_specs, aliases = [], [], {}
    if init_stack:
        state_spec = pl.BlockSpec((N_A, None, 1, DK_A, DV_A), lambda bi, h, r: (0, bi, h, 0, 0))
    else:
        state_spec = pl.BlockSpec((None, None, 1, DK_A, DV_A), lambda bi, h, r: (layer, bi, h, 0, 0))
        aliases = {11: 1}
        extra_args.append(s_out_prev)
        extra_specs.append(pl.BlockSpec(memory_space=pl.ANY))
    return pl.pallas_call(
        kernel,
        grid=(b, H_A, n_steps),
        in_specs=[
            pl.BlockSpec(memory_space=pltpu.SMEM),
            pl.BlockSpec((rows, d), lambda bi, h, r: (bi * n_steps + r, 0)),
            pl.BlockSpec((None, d, DK_A), lambda bi, h, r: (layer, 0, h)),
            pl.BlockSpec((None, d, DK_A), lambda bi, h, r: (layer, 0, k_off + h)),
            pl.BlockSpec((None, d, DV_A), lambda bi, h, r: (layer, 0, v_off + h)),
            pl.BlockSpec((None, d, DV_A), lambda bi, h, r: (layer, 0, g_off + h)),
            pl.BlockSpec((rows, DK_A // 2), lambda bi, h, r: (r, 0)),
            pl.BlockSpec((rows, DK_A // 2), lambda bi, h, r: (r, 0)),
            pl.BlockSpec((None, chunk, chunk), lambda bi, h, r: (h, 0, 0)),
            pl.BlockSpec((None, chunk, 1), lambda bi, h, r: (h, 0, 0)),
            pl.BlockSpec((None, chunk, 1), lambda bi, h, r: (h, 0, 0)),
        ] + extra_specs,
        out_specs=[
            pl.BlockSpec((None, rows, DV_A), lambda bi, h, r: (bi, r, h)),
            state_spec,
        ],
        out_shape=[
            jax.ShapeDtypeStruct((b, t, V_A), out_dtype),
            jax.ShapeDtypeStruct((N_A, b, H_A, DK_A, DV_A), F32),
        ],
        scratch_shapes=[
            pltpu.VMEM((rows, 2 * DK_A + 2 * DV_A), F32),
            pltpu.VMEM((DK_A, DV_A), F32),
        ],
        input_output_aliases=aliases,
        compiler_params=_params("parallel", "parallel", "arbitrary"),
        name="proj_retention",
    )(state_decay, h_act, w_in, w_in, w_in, w_in, cos, sin, decay, cross_scale, wk, *extra_args)


def _select_topk(gate, n_sel, axis=1):
    pos = lax.broadcasted_iota(jnp.int32, gate.shape, axis).astype(F32)
    picks = []
    for _ in range(n_sel):
        m = jnp.max(gate, axis=axis, keepdims=True)
        idx = jnp.min(jnp.where(gate == m, pos, float(gate.shape[axis])), axis=axis, keepdims=True)
        valid = m > -jnp.inf
        picks.append((idx, valid))
        gate = jnp.where((pos == idx) & valid, -jnp.inf, gate)
    return picks


MOBA_HEADS_PER_STEP = 2


def _moba_prompt_kernel(h_ref, wq_ref, wg_ref, k_ref, vt_ref, a_ref, q_scr, g_scr, *, n_blocks, heads):
    hb = h_ref[...]
    q_scr[...] = jnp.dot(hb, wq_ref[...].astype(BF16), preferred_element_type=F32)
    g_scr[...] = jnp.dot(hb, wg_ref[...].astype(BF16), preferred_element_type=F32)
    for hh in range(heads):
        cols = slice(hh * DH_B, (hh + 1) * DH_B)
        _moba_prompt_head(q_scr, g_scr, k_ref.at[hh], vt_ref.at[hh], a_ref, cols, n_blocks)


def _moba_prompt_head(q_ref, g_ref, k_ref, vt_ref, a_ref, cols, n_blocks):
    bs = MOBA_BLOCK
    scale = DH_B ** -0.5
    n_sel = min(MOBA_TOPK, n_blocks)
    kb = k_ref[...]
    ones_rows = 16
    vt = jnp.concatenate([vt_ref[...], jnp.ones((ones_rows, kb.shape[0]), BF16)], axis=0)
    means_b = jnp.mean(kb.astype(F32).reshape(n_blocks, bs, DH_B), axis=1).astype(BF16)
    key = lax.broadcasted_iota(jnp.int32, (bs, bs), 0)
    qry = lax.broadcasted_iota(jnp.int32, (bs, bs), 1)
    causal = key <= qry

    for i in range(n_blocks):
        rows = slice(i * bs, (i + 1) * bs)
        q = q_ref[rows, cols]
        qs = (q * (scale * LOG2_E)).astype(BF16)
        picks = None
        if i > n_sel:
            gate = lax.dot_general(means_b, q.astype(BF16), (((1,), (1,)), ((), ())),
                                   preferred_element_type=F32)
            blk = lax.broadcasted_iota(jnp.int32, gate.shape, 0)
            picks = _select_topk(jnp.where(blk < i, gate, -jnp.inf), n_sel, axis=0)
        blocks = []
        for j in range(i + 1):
            sj = lax.dot_general(kb[j * bs:(j + 1) * bs, :], qs, (((1,), (1,)), ((), ())),
                                 preferred_element_type=F32)
            if j == i:
                sj = jnp.where(causal, sj, -jnp.inf)
            elif picks is not None:
                chosen = picks[0][0] == float(j)
                for idx, _ in picks[1:]:
                    chosen = chosen | (idx == float(j))
                sj = jnp.where(chosen, sj, -jnp.inf)
            blocks.append(sj)
        m = jnp.max(functools.reduce(jnp.maximum, blocks), axis=0, keepdims=True)
        acc = jnp.zeros((DH_B + ones_rows, bs), F32)
        for j, sj in enumerate(blocks):
            p = jnp.exp2(sj - m).astype(BF16)
            acc = acc + jnp.dot(vt[:, j * bs:(j + 1) * bs], p, preferred_element_type=F32)
        o = (acc[:DH_B, :] / acc[DH_B:DH_B + 1, :]).T
        g = g_ref[rows, cols]
        a_ref[rows, cols] = (g * jax.nn.sigmoid(g) * o).astype(a_ref.dtype)


def _moba_prompt(h_act, w_q, layer, b, t, k_hm, vt_hm, out_dtype):
    d = h_act.shape[1]
    n_blocks = t // MOBA_BLOCK
    heads = MOBA_HEADS_PER_STEP
    assert n_blocks * MOBA_BLOCK == t and n_blocks <= 128 and H_B % heads == 0
    g_off = H_B // heads
    kernel = functools.partial(_moba_prompt_kernel, n_blocks=n_blocks, heads=heads)
    return pl.pallas_call(
        kernel,
        grid=(b, H_B // heads),
        in_specs=[
            pl.BlockSpec((t, d), lambda bi, h: (bi, 0)),
            pl.BlockSpec((None, d, heads * DH_B), lambda bi, h: (layer, 0, h)),
            pl.BlockSpec((None, d, heads * DH_B), lambda bi, h: (layer, 0, g_off + h)),
            pl.BlockSpec((None, heads, t, DH_B), lambda bi, h: (bi, h, 0, 0)),
            pl.BlockSpec((None, heads, DH_B, t), lambda bi, h: (bi, h, 0, 0)),
        ],
        out_specs=pl.BlockSpec((None, t, heads * DH_B), lambda bi, h: (bi, 0, h)),
        out_shape=jax.ShapeDtypeStruct((b, t, D_MODEL), out_dtype),
        scratch_shapes=[pltpu.VMEM((t, heads * DH_B), F32), pltpu.VMEM((t, heads * DH_B), F32)],
        compiler_params=_params("parallel", "arbitrary"),
        name="moba_prompt",
    )(h_act, w_q, w_q, k_hm, vt_hm)


HEAD_GROUP = 8


GATHER_PAGES_PER_STEP = 4


def _gather_cache_kernel(pt_ref, *refs, page, n_groups, pages_per_step, pages_per_block):
    n_slabs = pages_per_step * n_groups
    o_ref, m_ref = refs[2 * n_slabs], refs[2 * n_slabs + 1]
    for part in range(2):
        lanes = slice(part * DH_B, (part + 1) * DH_B)
        for pg in range(pages_per_step):
            for grp in range(n_groups):
                src = refs[part * n_slabs + pg * n_groups + grp]
                src = src.reshape(page * HEAD_GROUP, DH_B)
                for hg in range(HEAD_GROUP):
                    rows = src[pl.ds(hg, page, stride=HEAD_GROUP), :]
                    o_ref[grp * HEAD_GROUP + hg, pg * page:(pg + 1) * page, lanes] = rows.astype(BF16)
    for blk in range(pages_per_step // pages_per_block):
        for grp in range(n_groups):
            total = None
            for pg in range(blk * pages_per_block, (blk + 1) * pages_per_block):
                part_sum = jnp.sum(refs[pg * n_groups + grp][...], axis=0)
                total = part_sum if total is None else total + part_sum
            m_ref[blk, grp * HEAD_GROUP:(grp + 1) * HEAD_GROUP, :] = total / MOBA_BLOCK


def _gather_cache(cache_k, cache_v, page_table_flat, n_seq, n_pages):
    n_pool, page, n_h, dh = cache_k.shape
    pages_per_block = MOBA_BLOCK // page
    pps = GATHER_PAGES_PER_STEP
    assert n_h == H_B and dh == DH_B and n_h % HEAD_GROUP == 0
    assert pps % pages_per_block == 0 and n_pages % pps == 0
    n_blocks = n_pages // pages_per_block
    n_groups = n_h // HEAD_GROUP
    grouped = [c.reshape(n_pool, page, n_groups, HEAD_GROUP, dh) for c in (cache_k, cache_v)]

    def page_map(pg, grp):
        return lambda bi, j, pt: (pt[bi * n_pages + pps * j + pg], 0, grp, 0, 0)

    def slab_specs():
        return [pl.BlockSpec((None, page, None, HEAD_GROUP, dh), page_map(pg, grp))
                for pg in range(pps) for grp in range(n_groups)]

    n_slabs = pps * n_groups
    blocks_per_step = pps // pages_per_block
    kernel = functools.partial(_gather_cache_kernel, page=page, n_groups=n_groups, pages_per_step=pps,
                               pages_per_block=pages_per_block)
    return pl.pallas_call(
        kernel,
        grid_spec=pltpu.PrefetchScalarGridSpec(
            num_scalar_prefetch=1,
            grid=(n_seq, n_pages // pps),
            in_specs=slab_specs() + slab_specs(),
            out_specs=[
                pl.BlockSpec((None, n_h, pps * page, 2 * dh), lambda bi, j, pt: (bi, 0, j, 0)),
                pl.BlockSpec((None, blocks_per_step, n_h, dh), lambda bi, j, pt: (bi, j, 0, 0)),
            ],
        ),
        out_shape=[
            jax.ShapeDtypeStruct((n_seq, n_h, n_pages * page, 2 * dh), BF16),
            jax.ShapeDtypeStruct((n_seq, n_blocks, n_h, dh), F32),
        ],
        compiler_params=_params("parallel", "arbitrary"),
        name="gather_cache",
    )(page_table_flat, *([grouped[0]] * n_slabs), *([grouped[1]] * n_slabs))


def _moba_select_kernel(q_ref, means_ref, o_ref, *, t, n_blocks, past):
    q = q_ref[...]
    rows = t * H_B
    qrep = jnp.concatenate([jnp.broadcast_to(q[ti:ti + 1, :], (H_B, D_MODEL)) for ti in range(t)], axis=0)
    lane_h = lax.broadcasted_iota(jnp.int32, (rows, D_MODEL), 1) >> (DH_B.bit_length() - 1)
    row_h = lax.broadcasted_iota(jnp.int32, (rows, D_MODEL), 0) & (H_B - 1)
    qexp = jnp.where(lane_h == row_h, qrep, 0.0)
    means = jnp.concatenate([means_ref[...], jnp.zeros((128 - n_blocks, D_MODEL), F32)], axis=0)
    gate = lax.dot_general(qexp.astype(BF16), means.astype(BF16), (((1,), (1,)), ((), ())),
                           preferred_element_type=F32)
    lane = lax.broadcasted_iota(jnp.int32, gate.shape, 1)
    token = lax.broadcasted_iota(jnp.int32, gate.shape, 0) >> (H_B.bit_length() - 1)
    own = (past + token) >> (MOBA_BLOCK.bit_length() - 1)
    gate = jnp.where(lane < own, gate, -jnp.inf)
    out = jnp.zeros(gate.shape, F32)
    for r, (idx, _) in enumerate(_select_topk(gate, MOBA_TOPK)):
        out = jnp.where(lane == r, idx, out)
    o_ref[...] = out.astype(jnp.int32)


def _moba_select(qg, means, past):
    b, t, _ = qg.shape
    n_blocks = means.shape[1]
    assert MOBA_TOPK <= n_blocks <= 128
    kernel = functools.partial(_moba_select_kernel, t=t, n_blocks=n_blocks, past=past)
    return pl.pallas_call(
        kernel,
        grid=(b,),
        in_specs=[
            pl.BlockSpec((None, t, D_MODEL), lambda bi: (bi, 0, 0)),
            pl.BlockSpec((None, n_blocks, D_MODEL), lambda bi: (bi, 0, 0)),
        ],
        out_specs=pl.BlockSpec((None, t * H_B, 128), lambda bi: (bi, 0, 0)),
        out_shape=jax.ShapeDtypeStruct((b, t * H_B, 128), jnp.int32),
        compiler_params=_params("parallel"),
        name="moba_select",
    )(qg, means)


SAMPLE_HEADS_PER_STEP = 4


def _moba_sample_kernel(sel_ref, new_ref, *refs, t, n_slots, heads):
    a_ref = refs[heads * n_slots]
    for hh in range(heads):
        cols = slice(hh * DH_B, (hh + 1) * DH_B)
        _moba_sample_head(new_ref.at[hh], refs[hh * n_slots:(hh + 1) * n_slots], a_ref, cols, t, n_slots)


def _moba_sample_head(new_ref, kv_refs, a_ref, cols, t, n_slots):
    keys_per_token = (n_slots // t) * kv_refs[0].shape[0]

    q = new_ref[:, 0:DH_B]
    qb = q.astype(BF16)
    kall = jnp.concatenate([r[:, :DH_B] for r in kv_refs], axis=0)
    vall = jnp.concatenate([r[:, DH_B:] for r in kv_refs], axis=0)
    s = lax.dot_general(qb, kall, (((1,), (1,)), ((), ())), preferred_element_type=F32) * (DH_B ** -0.5)
    row = lax.broadcasted_iota(jnp.int32, s.shape, 0)
    col = lax.broadcasted_iota(jnp.int32, s.shape, 1)
    mine = (col >= row * keys_per_token) & (col < (row + 1) * keys_per_token)
    s = jnp.where(mine, s, -jnp.inf)

    kn = new_ref[:, 2 * DH_B:3 * DH_B]
    vn = new_ref[:, 3 * DH_B:4 * DH_B]
    rown = lax.broadcasted_iota(jnp.int32, (t, 1), 0)
    s_own = []
    for tj in range(t):
        sj = jnp.sum(q * kn[tj:tj + 1, :], axis=-1, keepdims=True) * (DH_B ** -0.5)
        s_own.append(jnp.where(rown >= tj, sj, -jnp.inf))
    m = jnp.max(s, axis=-1, keepdims=True)
    for sj in s_own:
        m = jnp.maximum(m, sj)
    p = jnp.exp(s - m)
    l = jnp.sum(p, axis=-1, keepdims=True)
    acc = jnp.dot(p.astype(BF16), vall, preferred_element_type=F32)
    for tj, sj in enumerate(s_own):
        pj = jnp.exp(sj - m)
        l = l + pj
        acc = acc + pj * vn[tj:tj + 1, :]
    o = acc / l
    g = new_ref[:, DH_B:2 * DH_B]
    a_ref[:, cols] = (g * jax.nn.sigmoid(g) * o).astype(a_ref.dtype)


def _moba_sample(qg, kv_new, past_kv, sel_flat):
    b, t, _ = qg.shape
    n_slots = t * MOBA_TOPK
    heads = SAMPLE_HEADS_PER_STEP
    assert H_B % heads == 0
    per_head = [a.reshape(b, t, H_B, DH_B) for a in
                (qg[:, :, :D_MODEL], qg[:, :, D_MODEL:], kv_new[:, :, :D_MODEL], kv_new[:, :, D_MODEL:])]
    new_rows = jnp.concatenate(per_head, axis=-1).transpose(0, 2, 1, 3)

    def slot_map(hh, ti, r):
        def index_map(bi, hg, sel):
            h = hg * heads + hh
            return (bi, h, sel[((bi * t + ti) * H_B + h) * MOBA_TOPK + r], 0)
        return index_map

    slot_specs = [pl.BlockSpec((None, None, MOBA_BLOCK, 2 * DH_B), slot_map(hh, ti, r))
                  for hh in range(heads) for ti in range(t) for r in range(MOBA_TOPK)]

    kernel = functools.partial(_moba_sample_kernel, t=t, n_slots=n_slots, heads=heads)
    return pl.pallas_call(
        kernel,
        grid_spec=pltpu.PrefetchScalarGridSpec(
            num_scalar_prefetch=1,
            grid=(b, H_B // heads),
            in_specs=[pl.BlockSpec((None, heads, t, 4 * DH_B), lambda bi, hg, sel: (bi, hg, 0, 0))] + slot_specs,
            out_specs=pl.BlockSpec((None, t, heads * DH_B), lambda bi, hg, sel: (bi, 0, hg)),
        ),
        out_shape=jax.ShapeDtypeStruct((b, t, D_MODEL), F32),
        compiler_params=_params("parallel", "arbitrary"),
        name="moba_sample",
    )(sel_flat, new_rows, *([past_kv] * len(slot_specs)))


def _run_group(x, mods, mods_kv, mods_f, s_in, past, weights, paged):
    (norm_g, w_in_a, w_out_a, w_q_b, w_o_b, kv_norm_g, w_kv, final_g) = weights
    b, t, d = x.shape
    rows = b * t
    prompt = paged is None
    pos = past + jnp.arange(t, dtype=jnp.int32)
    if prompt:
        tm, tn, tm_norm, rpb = 1024, 1024, 512, t
        act_dtype = BF16
        expand = lambda v: v[:, None, :]
    else:
        tm, tn, tm_norm, rpb = rows, 1024, rows, 0
        act_dtype = F32
        expand = lambda v: jnp.repeat(v, t, axis=0)

    def project(x2, g, shift, scale, w, layer):
        if prompt:
            return _norm_matmul(x2, g[None, :], shift, scale, w, layer, rpb, tm, tn)
        h = _norm_mod(x2, g[None, :], shift, scale, rpb, act_dtype, tm_norm)
        return _matmul(h, w, layer, tm, tn)

    x2 = x.reshape(rows, d)
    new_s = None
    kv = k_new = v_new = k_hm = vt_hm = None
    for l in range(DEPTH):
        shift, scale, gate = (expand(mods[l][:, i * d:(i + 1) * d]) for i in range(3))
        if l < N_A and prompt:
            h = _norm_mod(x2, norm_g[l][None, :], shift, scale, rpb, BF16, tm_norm)
            a, new_s = _proj_retention(h, w_in_a, l, b, t, pos, act_dtype, new_s)
            x2 = _matmul(a.reshape(rows, V_A), w_out_a, l, tm, tn, res=x2, gate=gate, rows_per_batch=rpb)
        elif l < N_A:
            qkvg = project(x2, norm_g[l], shift, scale, w_in_a, l).reshape(b, t, -1)
            a, new_s = _retention(qkvg, s_in, l, pos, act_dtype, new_s)
            x2 = _matmul(a.reshape(rows, V_A), w_out_a, l, tm, tn, res=x2, gate=gate, rows_per_batch=rpb)
        else:
            lb = l - N_A
            if prompt:
                h = _norm_mod(x2, norm_g[l][None, :], shift, scale, rpb, BF16, tm_norm)
                a = _moba_prompt(h, w_q_b, lb, b, t, k_hm, vt_hm, act_dtype)
            else:
                qg = project(x2, norm_g[l], shift, scale, w_q_b, lb).reshape(b, t, -1)
                past_kv, means = paged
                sel = _moba_select(qg, means, past)[:, :, :MOBA_TOPK].reshape(-1)
                a = _moba_sample(qg, kv.reshape(b, t, -1), past_kv, sel)
            x2 = _matmul(a.reshape(rows, d), w_o_b, lb, tm, tn, res=x2, gate=gate, rows_per_batch=rpb)
        if l == N_A - 1:
            kv_shift, kv_scale = (expand(mods_kv[:, i * d:(i + 1) * d]) for i in range(2))
            kv = project(x2, kv_norm_g, kv_shift, kv_scale, w_kv[None], 0)
            if prompt:
                k_new, v_new, k_hm, vt_hm = _split_heads(kv, b, t)
            else:
                k_new = kv[:, :d].reshape(b, t, H_B, DH_B)
                v_new = kv[:, d:].reshape(b, t, H_B, DH_B)
    f_shift, f_scale = (expand(mods_f[:, i * d:(i + 1) * d]) for i in range(2))
    y = _norm_mod(x2, final_g[None, :], f_shift, f_scale, rpb, F32, tm_norm)
    return y.reshape(b, t, d), new_s, k_new, v_new


def kernel(x_prompt, x_sample, state_ret, cache_k, cache_v, page_table, c_prompt, c_sample,
           norm_g, w_mod, b_mod, w_in_a, w_out_a, w_q_b, w_o_b,
           kv_norm_g, w_mod_kv, b_mod_kv, w_kv, final_g, w_mod_f, b_mod_f):
    bp = x_prompt.shape[0]
    bd, n_pages = page_table.shape
    n_pool, page, _, _ = cache_k.shape
    past_len = n_pages * page
    assert bp + bd <= MOD_ROWS

    c_all = jnp.concatenate([c_prompt, c_sample, jnp.zeros((MOD_ROWS - bp - bd, D_MODEL), F32)], axis=0)
    mods = _mod_matmul(c_all, w_mod, b_mod[:, None, :])
    mods_kv = _mod_matmul(c_all, w_mod_kv[None], b_mod_kv[None, None, :])[0]
    mods_f = _mod_matmul(c_all, w_mod_f[None], b_mod_f[None, None, :])[0]

    weights = (norm_g, w_in_a, w_out_a, w_q_b, w_o_b, kv_norm_g, w_kv, final_g)

    y_p, s_p, k_p, v_p = _run_group(
        x_prompt, mods[:, :bp], mods_kv[:bp], mods_f[:bp], None, 0, weights, None)

    pt_flat = page_table.reshape(-1)
    past_kv, means = _gather_cache(cache_k, cache_v, pt_flat, bd, n_pages)
    means = means.reshape(bd, -1, H_B * DH_B)
    y_s, s_s, k_s, v_s = _run_group(
        x_sample, mods[:, bp:bp + bd], mods_kv[bp:bp + bd], mods_f[bp:bp + bd], state_ret, past_len,
        weights, (past_kv, means))
    return (y_p, y_s, s_p, s_s, k_p, v_p, k_s, v_s)
```

```python
import functools
import math

import jax
import jax.numpy as jnp
from jax import lax
from jax.experimental import pallas as pl
from jax.experimental.pallas import tpu as pltpu

F32 = jnp.float32
BF16 = jnp.bfloat16

D_MODEL = 2048
DEPTH = 4
N_A = DEPTH // 2
H_A = 8
DK_A = D_MODEL // H_A
DV_A = 2 * DK_A
QK_A = H_A * DK_A
V_A = H_A * DV_A
RET_CHUNK = 256
H_B = 16
DH_B = D_MODEL // H_B
MOBA_BLOCK = 256
MOBA_TOPK = 3
ROPE_BASE = 10000.0
EPS = 1e-6
LOG2_E = math.log2(math.e)

VMEM_LIMIT_BYTES = 56 * 1024 * 1024
MOD_ROWS = 16


def _params(*sem):
    return pltpu.CompilerParams(dimension_semantics=sem, vmem_limit_bytes=VMEM_LIMIT_BYTES)


def _mod_kernel(c_ref, w_ref, b_ref, o_ref):
    acc = jnp.dot(c_ref[...].astype(BF16), w_ref[...].astype(BF16), preferred_element_type=F32)
    o_ref[...] = acc + b_ref[...]


def _mod_matmul(c, w, b, tn=1024):
    n_l, d, n = w.shape
    return pl.pallas_call(
        _mod_kernel,
        grid=(n_l, n // tn),
        in_specs=[
            pl.BlockSpec((MOD_ROWS, d), lambda l, j: (0, 0)),
            pl.BlockSpec((None, d, tn), lambda l, j: (l, 0, j)),
            pl.BlockSpec((None, 1, tn), lambda l, j: (l, 0, j)),
        ],
        out_specs=pl.BlockSpec((None, MOD_ROWS, tn), lambda l, j: (l, 0, j)),
        out_shape=jax.ShapeDtypeStruct((n_l, MOD_ROWS, n), F32),
        compiler_params=_params("parallel", "parallel"),
        name="mod_matmul",
    )(c, w, b)


def _norm_mod_kernel(x_ref, g_ref, sh_ref, sc_ref, o_ref):
    x = x_ref[...]
    r = lax.rsqrt(jnp.mean(x * x, axis=-1, keepdims=True) + EPS)
    y = (x * r) * g_ref[...]
    o_ref[...] = (y * (1.0 + sc_ref[...]) + sh_ref[...]).astype(o_ref.dtype)


def _norm_mod(x, g, shift, scale, rows_per_batch, out_dtype, tm):
    r, d = x.shape
    if rows_per_batch:
        tiles_per_b = rows_per_batch // tm
        mod_spec = pl.BlockSpec((None, 1, d), lambda i: (i // tiles_per_b, 0, 0))
    else:
        mod_spec = pl.BlockSpec((tm, d), lambda i: (i, 0))
    return pl.pallas_call(
        _norm_mod_kernel,
        grid=(r // tm,),
        in_specs=[
            pl.BlockSpec((tm, d), lambda i: (i, 0)),
            pl.BlockSpec((1, d), lambda i: (0, 0)),
            mod_spec,
            mod_spec,
        ],
        out_specs=pl.BlockSpec((tm, d), lambda i: (i, 0)),
        out_shape=jax.ShapeDtypeStruct((r, d), out_dtype),
        compiler_params=_params("parallel"),
        name="norm_mod",
    )(x, g, shift, scale)


def _matmul_kernel(a_ref, w_ref, o_ref):
    o_ref[...] = jnp.dot(
        a_ref[...].astype(BF16), w_ref[...].astype(BF16), preferred_element_type=F32
    ).astype(o_ref.dtype)


def _matmul_res_kernel(a_ref, w_ref, x_ref, gm_ref, o_ref):
    acc = jnp.dot(a_ref[...].astype(BF16), w_ref[...].astype(BF16), preferred_element_type=F32)
    o_ref[...] = x_ref[...] + gm_ref[...] * acc


MATMUL_W_TILE_ELEMS = 2 * 1024 * 1024
MATMUL_A_TILE_ELEMS = 4 * 1024 * 1024


def _matmul(a, w, layer, tm, tn, res=None, gate=None, rows_per_batch=0):
    r, k = a.shape
    n = w.shape[2]
    tm = min(tm, MATMUL_A_TILE_ELEMS // k)
    tn = min(tn, MATMUL_W_TILE_ELEMS // k, MATMUL_W_TILE_ELEMS * 1024 // (tm * k))
    in_specs = [
        pl.BlockSpec((tm, k), lambda i, j: (i, 0)),
        pl.BlockSpec((None, k, tn), lambda i, j: (layer, 0, j)),
    ]
    args = [a, w]
    kernel = _matmul_kernel
    if res is not None:
        kernel = _matmul_res_kernel
        in_specs.append(pl.BlockSpec((tm, tn), lambda i, j: (i, j)))
        if rows_per_batch:
            tiles_per_b = rows_per_batch // tm
            in_specs.append(pl.BlockSpec((None, 1, tn), lambda i, j: (i // tiles_per_b, 0, j)))
        else:
            in_specs.append(pl.BlockSpec((tm, tn), lambda i, j: (i, j)))
        args += [res, gate]
    return pl.pallas_call(
        kernel,
        grid=(r // tm, n // tn),
        in_specs=in_specs,
        out_specs=pl.BlockSpec((tm, tn), lambda i, j: (i, j)),
        out_shape=jax.ShapeDtypeStruct((r, n), F32),
        compiler_params=_params("parallel", "parallel"),
        name="matmul_res" if res is not None else "matmul",
    )(*args)


def _norm_matmul_kernel(x_ref, g_ref, sh_ref, sc_ref, w_ref, o_ref, h_scr):
    @pl.when(pl.program_id(1) == 0)
    def _():
        x = x_ref[...]
        r = lax.rsqrt(jnp.mean(x * x, axis=-1, keepdims=True) + EPS)
        y = (x * r) * g_ref[...]
        h_scr[...] = (y * (1.0 + sc_ref[...]) + sh_ref[...]).astype(BF16)

    o_ref[...] = jnp.dot(h_scr[...], w_ref[...].astype(BF16), preferred_element_type=F32)


def _norm_matmul(x, g, shift, scale, w, layer, rows_per_batch, tm, tn):
    r, k = x.shape
    n = w.shape[2]
    tn = min(tn, MATMUL_W_TILE_ELEMS // k)
    tiles_per_b = rows_per_batch // tm
    mod_spec = pl.BlockSpec((None, 1, k), lambda i, j: (i // tiles_per_b, 0, 0))
    return pl.pallas_call(
        _norm_matmul_kernel,
        grid=(r // tm, n // tn),
        in_specs=[
            pl.BlockSpec((tm, k), lambda i, j: (i, 0)),
            pl.BlockSpec((1, k), lambda i, j: (0, 0)),
            mod_spec,
            mod_spec,
            pl.BlockSpec((None, k, tn), lambda i, j: (layer, 0, j)),
        ],
        out_specs=pl.BlockSpec((tm, tn), lambda i, j: (i, j)),
        out_shape=jax.ShapeDtypeStruct((r, n), F32),
        scratch_shapes=[pltpu.VMEM((tm, k), BF16)],
        compiler_params=_params("parallel", "arbitrary"),
        name="norm_matmul",
    )(x, g, shift, scale, w)


def _split_heads_kernel(kv_ref, k_out_ref, v_out_ref, k_hm_ref, vt_hm_ref, *, tm):
    for h in range(H_B):
        kh = kv_ref[:, h * DH_B:(h + 1) * DH_B]
        vh = kv_ref[:, (H_B + h) * DH_B:(H_B + h + 1) * DH_B]
        k_out_ref[pl.ds(h, tm, stride=H_B), :] = kh
        v_out_ref[pl.ds(h, tm, stride=H_B), :] = vh
        k_hm_ref[h] = kh.astype(BF16)
        vt_hm_ref[h] = vh.T.astype(BF16)


def _split_heads(kv, b, t, tm=256):
    r = b * t
    tiles_per_b = t // tm
    kernel = functools.partial(_split_heads_kernel, tm=tm)
    rows_spec = pl.BlockSpec((tm * H_B, DH_B), lambda i: (i, 0))
    k_out, v_out, k_hm, vt_hm = pl.pallas_call(
        kernel,
        grid=(r // tm,),
        in_specs=[pl.BlockSpec((tm, 2 * D_MODEL), lambda i: (i, 0))],
        out_specs=[
            rows_spec,
            rows_spec,
            pl.BlockSpec((None, H_B, tm, DH_B), lambda i: (i // tiles_per_b, 0, i % tiles_per_b, 0)),
            pl.BlockSpec((None, H_B, DH_B, tm), lambda i: (i // tiles_per_b, 0, 0, i % tiles_per_b)),
        ],
        out_shape=[
            jax.ShapeDtypeStruct((r * H_B, DH_B), F32),
            jax.ShapeDtypeStruct((r * H_B, DH_B), F32),
            jax.ShapeDtypeStruct((b, H_B, t, DH_B), BF16),
            jax.ShapeDtypeStruct((b, H_B, DH_B, t), BF16),
        ],
        compiler_params=_params("parallel"),
        name="split_heads",
    )(kv)
    return k_out.reshape(b, t, H_B, DH_B), v_out.reshape(b, t, H_B, DH_B), k_hm, vt_hm


def _rope_tables(pos):
    half = DK_A // 2
    inv = 1.0 / (ROPE_BASE ** jnp.linspace(0.0, 1.0, half, dtype=F32))
    ang = pos.astype(F32)[:, None] * inv[None, :]
    return jnp.cos(ang), jnp.sin(ang)


def _retention_tables(chunk):
    log_g = jnp.log1p(-(2.0 ** (-5.0 - jnp.arange(H_A, dtype=F32))))
    i = jnp.arange(chunk, dtype=F32)
    diff = i[:, None] - i[None, :]
    decay = jnp.where(diff >= 0, jnp.exp(jnp.maximum(diff, 0.0)[None] * log_g[:, None, None]), 0.0)
    cross_scale = jnp.exp((i + 1.0)[:, None] * log_g[None, :]).T[:, :, None]
    wk = jnp.exp((chunk - 1.0 - i)[:, None] * log_g[None, :]).T[:, :, None]
    state_decay = jnp.exp(chunk * log_g)
    return decay, cross_scale, wk, state_decay


def _retention_chunk(q, k, v, g, cos, sin, decay, cross_scale, wk, state_decay, s):
    half = DK_A // 2

    def rot(x):
        x1 = x[:, :half]
        x2 = x[:, half:]
        return jnp.concatenate([x1 * cos - x2 * sin, x1 * sin + x2 * cos], axis=-1)

    q = rot(q)
    k = rot(k) * (DK_A ** -0.5)
    vb = v.astype(BF16)
    qb = q.astype(BF16)
    kb = k.astype(BF16)
    scores = lax.dot_general(qb, kb, (((1,), (1,)), ((), ())), preferred_element_type=F32) * decay
    inner = jnp.dot(scores.astype(BF16), vb, preferred_element_type=F32)
    cross = jnp.dot(qb, s.astype(BF16), preferred_element_type=F32) * cross_scale
    kw = (k * wk).astype(BF16)
    upd = lax.dot_general(kw, vb, (((0,), (0,)), ((), ())), preferred_element_type=F32)
    s_new = state_decay * s + upd
    o = inner + cross
    o = o * lax.rsqrt(jnp.mean(o * o, axis=-1, keepdims=True) + EPS)
    return g * jax.nn.sigmoid(g) * o, s_new


RET_HEADS_PER_STEP = 4
RET_CHUNKS_PER_STEP = 1


def _retention_kernel(sdec_ref, q_ref, k_ref, v_ref, g_ref, cos_ref, sin_ref, dec_ref, cs_ref,
                      wk_ref, *refs, chunk, heads, chunks, n_steps, zero_state, layer, init_stack):
    a_ref, sout_ref, s_scr = refs[-3:]
    hg = pl.program_id(1)
    c = pl.program_id(2)

    @pl.when(c == 0)
    def _():
        s_scr[...] = jnp.zeros(s_scr.shape, F32) if zero_state else refs[0][...]

    for hh in range(heads):
        s = s_scr[hh]
        kcols = slice(hh * DK_A, (hh + 1) * DK_A)
        vcols = slice(hh * DV_A, (hh + 1) * DV_A)
        for ci in range(chunks):
            rows = slice(ci * chunk, (ci + 1) * chunk)
            a, s = _retention_chunk(
                q_ref[rows, kcols], k_ref[rows, kcols], v_ref[rows, vcols], g_ref[rows, vcols],
                cos_ref[rows, :], sin_ref[rows, :], dec_ref[hh], cs_ref[hh], wk_ref[hh],
                sdec_ref[hg * heads + hh], s)
            a_ref[rows, vcols] = a.astype(a_ref.dtype)
        s_scr[hh] = s

        @pl.when(c == n_steps - 1)
        def _():
            _store_state(sout_ref, hh, s, layer, init_stack)


def _store_state(sout_ref, idx, s, layer, init_stack):
    if not init_stack:
        sout_ref[idx] = s
        return
    for slab in range(sout_ref.shape[0]):
        sout_ref[slab, idx] = s if slab == layer else jnp.zeros(s.shape, s.dtype)


def _retention(qkvg, s_in, layer, pos, out_dtype, s_out_prev):
    b, t, _ = qkvg.shape
    chunk = math.gcd(t, RET_CHUNK)
    n_chunks = t // chunk
    heads = RET_HEADS_PER_STEP
    chunks = math.gcd(n_chunks, RET_CHUNKS_PER_STEP)
    n_steps = n_chunks // chunks
    rows = chunk * chunks
    assert H_A % heads == 0
    cos, sin = _rope_tables(pos)
    decay, cross_scale, wk, state_decay = _retention_tables(chunk)
    k_off = QK_A // (heads * DK_A)
    v_off = 2 * QK_A // (heads * DV_A)
    g_off = (2 * QK_A + V_A) // (heads * DV_A)
    init_stack = s_out_prev is None
    kernel = functools.partial(_retention_kernel, chunk=chunk, heads=heads, chunks=chunks, n_steps=n_steps,
                               zero_state=s_in is None, layer=layer, init_stack=init_stack)
    state_spec = pl.BlockSpec((None, None, heads, DK_A, DV_A), lambda bi, h, c: (layer, bi, h, 0, 0))
    state_args, state_specs, aliases = [], [], {}
    if s_in is not None:
        state_args.append(s_in)
        state_specs.append(state_spec)
    if init_stack:
        out_state_spec = pl.BlockSpec((N_A, None, heads, DK_A, DV_A), lambda bi, h, c: (0, bi, h, 0, 0))
    else:
        out_state_spec = state_spec
        aliases = {10 + len(state_args): 1}
        state_args.append(s_out_prev)
        state_specs.append(pl.BlockSpec(memory_space=pl.ANY))
    return pl.pallas_call(
        kernel,
        grid=(b, H_A // heads, n_steps),
        in_specs=[
            pl.BlockSpec(memory_space=pltpu.SMEM),
            pl.BlockSpec((None, rows, heads * DK_A), lambda bi, h, c: (bi, c, h)),
            pl.BlockSpec((None, rows, heads * DK_A), lambda bi, h, c: (bi, c, k_off + h)),
            pl.BlockSpec((None, rows, heads * DV_A), lambda bi, h, c: (bi, c, v_off + h)),
            pl.BlockSpec((None, rows, heads * DV_A), lambda bi, h, c: (bi, c, g_off + h)),
            pl.BlockSpec((rows, DK_A // 2), lambda bi, h, c: (c, 0)),
            pl.BlockSpec((rows, DK_A // 2), lambda bi, h, c: (c, 0)),
            pl.BlockSpec((heads, chunk, chunk), lambda bi, h, c: (h, 0, 0)),
            pl.BlockSpec((heads, chunk, 1), lambda bi, h, c: (h, 0, 0)),
            pl.BlockSpec((heads, chunk, 1), lambda bi, h, c: (h, 0, 0)),
        ] + state_specs,
        out_specs=[
            pl.BlockSpec((None, rows, heads * DV_A), lambda bi, h, c: (bi, c, h)),
            out_state_spec,
        ],
        out_shape=[
            jax.ShapeDtypeStruct((b, t, V_A), out_dtype),
            jax.ShapeDtypeStruct((N_A, b, H_A, DK_A, DV_A), F32),
        ],
        scratch_shapes=[pltpu.VMEM((heads, DK_A, DV_A), F32)],
        input_output_aliases=aliases,
        compiler_params=_params("parallel", "parallel", "arbitrary"),
        name="retention",
    )(state_decay, qkvg, qkvg, qkvg, qkvg, cos, sin, decay, cross_scale, wk, *state_args)


PROJ_RET_ROWS = 1024


def _proj_retention_kernel(sdec_ref, h_ref, wq_ref, wkey_ref, wv_ref, wg_ref, cos_ref, sin_ref, dec_ref,
                           cs_ref, kdecay_ref, *refs, chunk, n_steps, layer, init_stack):
    a_ref, sout_ref, qkvg_scr, s_scr = refs[-4:]
    head = pl.program_id(1)
    r = pl.program_id(2)

    @pl.when(r == 0)
    def _():
        s_scr[...] = jnp.zeros(s_scr.shape, F32)

    hb = h_ref[...]
    col = 0
    for w_ref in (wq_ref, wkey_ref, wv_ref, wg_ref):
        width = w_ref.shape[1]
        qkvg_scr[:, col:col + width] = jnp.dot(hb, w_ref[...].astype(BF16), preferred_element_type=F32)
        col += width

    s = s_scr[...]
    for ci in range(h_ref.shape[0] // chunk):
        rows = slice(ci * chunk, (ci + 1) * chunk)
        a, s = _retention_chunk(
            qkvg_scr[rows, 0:DK_A], qkvg_scr[rows, DK_A:2 * DK_A],
            qkvg_scr[rows, 2 * DK_A:2 * DK_A + DV_A], qkvg_scr[rows, 2 * DK_A + DV_A:],
            cos_ref[rows, :], sin_ref[rows, :], dec_ref[...], cs_ref[...], kdecay_ref[...],
            sdec_ref[head], s)
        a_ref[rows, :] = a.astype(a_ref.dtype)
    s_scr[...] = s

    @pl.when(r == n_steps - 1)
    def _():
        _store_state(sout_ref, 0, s, layer, init_stack)


def _proj_retention(h_act, w_in, layer, b, t, pos, out_dtype, s_out_prev):
    chunk = math.gcd(t, RET_CHUNK)
    rows = min(PROJ_RET_ROWS, t)
    assert t % rows == 0 and rows % chunk == 0
    n_steps = t // rows
    cos, sin = _rope_tables(pos)
    decay, cross_scale, wk, state_decay = _retention_tables(chunk)
    k_off = QK_A // DK_A
    v_off = 2 * QK_A // DV_A
    g_off = (2 * QK_A + V_A) // DV_A
    d = h_act.shape[1]
    init_stack = s_out_prev is None
    kernel = functools.partial(_proj_retention_kernel, chunk=chunk, n_steps=n_steps, layer=layer,
                               init_stack=init_stack)
    extra_args, extra_specs, aliases = [], [], {}
    if init_stack:
        state_spec = pl.BlockSpec((N_A, None, 1, DK_A, DV_A), lambda bi, h, r: (0, bi, h, 0, 0))
    else:
        state_spec = pl.BlockSpec((None, None, 1, DK_A, DV_A), lambda bi, h, r: (layer, bi, h, 0, 0))
        aliases = {11: 1}
        extra_args.append(s_out_prev)
        extra_specs.append(pl.BlockSpec(memory_space=pl.ANY))
    return pl.pallas_call(
        kernel,
        grid=(b, H_A, n_steps),
        in_specs=[
            pl.BlockSpec(memory_space=pltpu.SMEM),
            pl.BlockSpec((rows, d), lambda bi, h, r: (bi * n_steps + r, 0)),
            pl.BlockSpec((None, d, DK_A), lambda bi, h, r: (layer, 0, h)),
            pl.BlockSpec((None, d, DK_A), lambda bi, h, r: (layer, 0, k_off + h)),
            pl.BlockSpec((None, d, DV_A), lambda bi, h, r: (layer, 0, v_off + h)),
            pl.BlockSpec((None, d, DV_A), lambda bi, h, r: (layer, 0, g_off + h)),
            pl.BlockSpec((rows, DK_A // 2), lambda bi, h, r: (r, 0)),
            pl.BlockSpec((rows, DK_A // 2), lambda bi, h, r: (r, 0)),
            pl.BlockSpec((None, chunk, chunk), lambda bi, h, r: (h, 0, 0)),
            pl.BlockSpec((None, chunk, 1), lambda bi, h, r: (h, 0, 0)),
            pl.BlockSpec((None, chunk, 1), lambda bi, h, r: (h, 0, 0)),
        ] + extra_specs,
        out_specs=[
            pl.BlockSpec((None, rows, DV_A), lambda bi, h, r: (bi, r, h)),
            state_spec,
        ],
        out_shape=[
            jax.ShapeDtypeStruct((b, t, V_A), out_dtype),
            jax.ShapeDtypeStruct((N_A, b, H_A, DK_A, DV_A), F32),
        ],
        scratch_shapes=[
            pltpu.VMEM((rows, 2 * DK_A + 2 * DV_A), F32),
            pltpu.VMEM((DK_A, DV_A), F32),
        ],
        input_output_aliases=aliases,
        compiler_params=_params("parallel", "parallel", "arbitrary"),
        name="proj_retention",
    )(state_decay, h_act, w_in, w_in, w_in, w_in, cos, sin, decay, cross_scale, wk, *extra_args)


def _select_topk(gate, n_sel, axis=1):
    pos = lax.broadcasted_iota(jnp.int32, gate.shape, axis).astype(F32)
    picks = []
    for _ in range(n_sel):
        m = jnp.max(gate, axis=axis, keepdims=True)
        idx = jnp.min(jnp.where(gate == m, pos, float(gate.shape[axis])), axis=axis, keepdims=True)
        valid = m > -jnp.inf
        picks.append((idx, valid))
        gate = jnp.where((pos == idx) & valid, -jnp.inf, gate)
    return picks


MOBA_HEADS_PER_STEP = 2
MOBA_PROJ_HEADS = 2


def _moba_prompt_kernel(h_ref, wq_ref, wg_ref, k_ref, vt_ref, a_ref, q_scr, g_scr, *, n_blocks, heads):
    hb = h_ref[...]
    for h0 in range(0, heads, MOBA_PROJ_HEADS):
        cols = slice(h0 * DH_B, (h0 + MOBA_PROJ_HEADS) * DH_B)
        q_scr[:, cols] = jnp.dot(hb, wq_ref[:, cols].astype(BF16), preferred_element_type=F32)
        g_scr[:, cols] = jnp.dot(hb, wg_ref[:, cols].astype(BF16), preferred_element_type=F32)
    for hh in range(heads):
        cols = slice(hh * DH_B, (hh + 1) * DH_B)
        _moba_prompt_head(q_scr, g_scr, k_ref.at[hh], vt_ref.at[hh], a_ref, cols, n_blocks)


def _moba_prompt_head(q_ref, g_ref, k_ref, vt_ref, a_ref, cols, n_blocks):
    bs = MOBA_BLOCK
    scale = DH_B ** -0.5
    n_sel = min(MOBA_TOPK, n_blocks)
    kb = k_ref[...]
    ones_rows = 16
    vt = jnp.concatenate([vt_ref[...], jnp.ones((ones_rows, kb.shape[0]), BF16)], axis=0)
    means_b = jnp.mean(kb.astype(F32).reshape(n_blocks, bs, DH_B), axis=1).astype(BF16)
    key = lax.broadcasted_iota(jnp.int32, (bs, bs), 0)
    qry = lax.broadcasted_iota(jnp.int32, (bs, bs), 1)
    causal = key <= qry

    for i in range(n_blocks):
        rows = slice(i * bs, (i + 1) * bs)
        q = q_ref[rows, cols]
        qs = (q * (scale * LOG2_E)).astype(BF16)
        picks = None
        if i > n_sel:
            gate = lax.dot_general(means_b, q.astype(BF16), (((1,), (1,)), ((), ())),
                                   preferred_element_type=F32)
            blk = lax.broadcasted_iota(jnp.int32, gate.shape, 0)
            picks = _select_topk(jnp.where(blk < i, gate, -jnp.inf), n_sel, axis=0)
        blocks = []
        for j in range(i + 1):
            sj = lax.dot_general(kb[j * bs:(j + 1) * bs, :], qs, (((1,), (1,)), ((), ())),
                                 preferred_element_type=F32)
            if j == i:
                sj = jnp.where(causal, sj, -jnp.inf)
            elif picks is not None:
                chosen = picks[0][0] == float(j)
                for idx, _ in picks[1:]:
                    chosen = chosen | (idx == float(j))
                sj = jnp.where(chosen, sj, -jnp.inf)
            blocks.append(sj)
        m = jnp.max(functools.reduce(jnp.maximum, blocks), axis=0, keepdims=True)
        acc = jnp.zeros((DH_B + ones_rows, bs), F32)
        for j, sj in enumerate(blocks):
            p = jnp.exp2(sj - m).astype(BF16)
            acc = acc + jnp.dot(vt[:, j * bs:(j + 1) * bs], p, preferred_element_type=F32)
        o = (acc[:DH_B, :] / acc[DH_B:DH_B + 1, :]).T
        g = g_ref[rows, cols]
        a_ref[rows, cols] = (g * jax.nn.sigmoid(g) * o).astype(a_ref.dtype)


def _moba_prompt(h_act, w_q, layer, b, t, k_hm, vt_hm, out_dtype):
    d = h_act.shape[1]
    n_blocks = t // MOBA_BLOCK
    heads = MOBA_HEADS_PER_STEP
    assert n_blocks * MOBA_BLOCK == t and n_blocks <= 128 and H_B % heads == 0
    g_off = H_B // heads
    kernel = functools.partial(_moba_prompt_kernel, n_blocks=n_blocks, heads=heads)
    return pl.pallas_call(
        kernel,
        grid=(b, H_B // heads),
        in_specs=[
            pl.BlockSpec((t, d), lambda bi, h: (bi, 0)),
            pl.BlockSpec((None, d, heads * DH_B), lambda bi, h: (layer, 0, h)),
            pl.BlockSpec((None, d, heads * DH_B), lambda bi, h: (layer, 0, g_off + h)),
            pl.BlockSpec((None, heads, t, DH_B), lambda bi, h: (bi, h, 0, 0)),
            pl.BlockSpec((None, heads, DH_B, t), lambda bi, h: (bi, h, 0, 0)),
        ],
        out_specs=pl.BlockSpec((None, t, heads * DH_B), lambda bi, h: (bi, 0, h)),
        out_shape=jax.ShapeDtypeStruct((b, t, D_MODEL), out_dtype),
        scratch_shapes=[pltpu.VMEM((t, heads * DH_B), F32), pltpu.VMEM((t, heads * DH_B), F32)],
        compiler_params=_params("parallel", "arbitrary"),
        name="moba_prompt",
    )(h_act, w_q, w_q, k_hm, vt_hm)


HEAD_GROUP = 8


GATHER_PAGES_PER_STEP = 4


def _gather_cache_kernel(pt_ref, *refs, page, n_groups, pages_per_step, pages_per_block):
    n_slabs = pages_per_step * n_groups
    o_ref, m_ref = refs[2 * n_slabs], refs[2 * n_slabs + 1]
    for part in range(2):
        lanes = slice(part * DH_B, (part + 1) * DH_B)
        for pg in range(pages_per_step):
            for grp in range(n_groups):
                src = refs[part * n_slabs + pg * n_groups + grp]
                src = src.reshape(page * HEAD_GROUP, DH_B)
                for hg in range(HEAD_GROUP):
                    rows = src[pl.ds(hg, page, stride=HEAD_GROUP), :]
                    o_ref[grp * HEAD_GROUP + hg, pg * page:(pg + 1) * page, lanes] = rows.astype(BF16)
    for blk in range(pages_per_step // pages_per_block):
        for grp in range(n_groups):
            total = None
            for pg in range(blk * pages_per_block, (blk + 1) * pages_per_block):
                part_sum = jnp.sum(refs[pg * n_groups + grp][...], axis=0)
                total = part_sum if total is None else total + part_sum
            m_ref[blk, grp * HEAD_GROUP:(grp + 1) * HEAD_GROUP, :] = total / MOBA_BLOCK


def _gather_cache(cache_k, cache_v, page_table_flat, n_seq, n_pages):
    n_pool, page, n_h, dh = cache_k.shape
    pages_per_block = MOBA_BLOCK // page
    pps = GATHER_PAGES_PER_STEP
    assert n_h == H_B and dh == DH_B and n_h % HEAD_GROUP == 0
    assert pps % pages_per_block == 0 and n_pages % pps == 0
    n_blocks = n_pages // pages_per_block
    n_groups = n_h // HEAD_GROUP
    grouped = [c.reshape(n_pool, page, n_groups, HEAD_GROUP, dh) for c in (cache_k, cache_v)]

    def page_map(pg, grp):
        return lambda bi, j, pt: (pt[bi * n_pages + pps * j + pg], 0, grp, 0, 0)

    def slab_specs():
        return [pl.BlockSpec((None, page, None, HEAD_GROUP, dh), page_map(pg, grp))
                for pg in range(pps) for grp in range(n_groups)]

    n_slabs = pps * n_groups
    blocks_per_step = pps // pages_per_block
    kernel = functools.partial(_gather_cache_kernel, page=page, n_groups=n_groups, pages_per_step=pps,
                               pages_per_block=pages_per_block)
    return pl.pallas_call(
        kernel,
        grid_spec=pltpu.PrefetchScalarGridSpec(
            num_scalar_prefetch=1,
            grid=(n_seq, n_pages // pps),
            in_specs=slab_specs() + slab_specs(),
            out_specs=[
                pl.BlockSpec((None, n_h, pps * page, 2 * dh), lambda bi, j, pt: (bi, 0, j, 0)),
                pl.BlockSpec((None, blocks_per_step, n_h, dh), lambda bi, j, pt: (bi, j, 0, 0)),
            ],
        ),
        out_shape=[
            jax.ShapeDtypeStruct((n_seq, n_h, n_pages * page, 2 * dh), BF16),
            jax.ShapeDtypeStruct((n_seq, n_blocks, n_h, dh), F32),
        ],
        compiler_params=_params("parallel", "arbitrary"),
        name="gather_cache",
    )(page_table_flat, *([grouped[0]] * n_slabs), *([grouped[1]] * n_slabs))


def _moba_select_kernel(q_ref, means_ref, o_ref, *, t, n_blocks, past):
    q = q_ref[...]
    rows = t * H_B
    qrep = jnp.concatenate([jnp.broadcast_to(q[ti:ti + 1, :], (H_B, D_MODEL)) for ti in range(t)], axis=0)
    lane_h = lax.broadcasted_iota(jnp.int32, (rows, D_MODEL), 1) >> (DH_B.bit_length() - 1)
    row_h = lax.broadcasted_iota(jnp.int32, (rows, D_MODEL), 0) & (H_B - 1)
    qexp = jnp.where(lane_h == row_h, qrep, 0.0)
    means = jnp.concatenate([means_ref[...], jnp.zeros((128 - n_blocks, D_MODEL), F32)], axis=0)
    gate = lax.dot_general(qexp.astype(BF16), means.astype(BF16), (((1,), (1,)), ((), ())),
                           preferred_element_type=F32)
    lane = lax.broadcasted_iota(jnp.int32, gate.shape, 1)
    token = lax.broadcasted_iota(jnp.int32, gate.shape, 0) >> (H_B.bit_length() - 1)
    own = (past + token) >> (MOBA_BLOCK.bit_length() - 1)
    gate = jnp.where(lane < own, gate, -jnp.inf)
    out = jnp.zeros(gate.shape, F32)
    for r, (idx, _) in enumerate(_select_topk(gate, MOBA_TOPK)):
        out = jnp.where(lane == r, idx, out)
    o_ref[...] = out.astype(jnp.int32)


def _moba_select(qg, means, past):
    b, t, _ = qg.shape
    n_blocks = means.shape[1]
    assert MOBA_TOPK <= n_blocks <= 128
    kernel = functools.partial(_moba_select_kernel, t=t, n_blocks=n_blocks, past=past)
    return pl.pallas_call(
        kernel,
        grid=(b,),
        in_specs=[
            pl.BlockSpec((None, t, D_MODEL), lambda bi: (bi, 0, 0)),
            pl.BlockSpec((None, n_blocks, D_MODEL), lambda bi: (bi, 0, 0)),
        ],
        out_specs=pl.BlockSpec((None, t * H_B, 128), lambda bi: (bi, 0, 0)),
        out_shape=jax.ShapeDtypeStruct((b, t * H_B, 128), jnp.int32),
        compiler_params=_params("parallel"),
        name="moba_select",
    )(qg, means)


SAMPLE_HEADS_PER_STEP = 4


def _moba_sample_kernel(sel_ref, new_ref, *refs, t, n_slots, heads):
    a_ref = refs[heads * n_slots]
    for hh in range(heads):
        cols = slice(hh * DH_B, (hh + 1) * DH_B)
        _moba_sample_head(new_ref.at[hh], refs[hh * n_slots:(hh + 1) * n_slots], a_ref, cols, t, n_slots)


def _moba_sample_head(new_ref, kv_refs, a_ref, cols, t, n_slots):
    keys_per_token = (n_slots // t) * kv_refs[0].shape[0]

    q = new_ref[:, 0:DH_B]
    qb = q.astype(BF16)
    kall = jnp.concatenate([r[:, :DH_B] for r in kv_refs], axis=0)
    vall = jnp.concatenate([r[:, DH_B:] for r in kv_refs], axis=0)
    s = lax.dot_general(qb, kall, (((1,), (1,)), ((), ())), preferred_element_type=F32) * (DH_B ** -0.5)
    row = lax.broadcasted_iota(jnp.int32, s.shape, 0)
    col = lax.broadcasted_iota(jnp.int32, s.shape, 1)
    mine = (col >= row * keys_per_token) & (col < (row + 1) * keys_per_token)
    s = jnp.where(mine, s, -jnp.inf)

    kn = new_ref[:, 2 * DH_B:3 * DH_B]
    vn = new_ref[:, 3 * DH_B:4 * DH_B]
    rown = lax.broadcasted_iota(jnp.int32, (t, 1), 0)
    s_own = []
    for tj in range(t):
        sj = jnp.sum(q * kn[tj:tj + 1, :], axis=-1, keepdims=True) * (DH_B ** -0.5)
        s_own.append(jnp.where(rown >= tj, sj, -jnp.inf))
    m = jnp.max(s, axis=-1, keepdims=True)
    for sj in s_own:
        m = jnp.maximum(m, sj)
    p = jnp.exp(s - m)
    l = jnp.sum(p, axis=-1, keepdims=True)
    acc = jnp.dot(p.astype(BF16), vall, preferred_element_type=F32)
    for tj, sj in enumerate(s_own):
        pj = jnp.exp(sj - m)
        l = l + pj
        acc = acc + pj * vn[tj:tj + 1, :]
    o = acc / l
    g = new_ref[:, DH_B:2 * DH_B]
    a_ref[:, cols] = (g * jax.nn.sigmoid(g) * o).astype(a_ref.dtype)


def _moba_sample(qg, kv_new, past_kv, sel_flat):
    b, t, _ = qg.shape
    n_slots = t * MOBA_TOPK
    heads = SAMPLE_HEADS_PER_STEP
    assert H_B % heads == 0
    per_head = [a.reshape(b, t, H_B, DH_B) for a in
                (qg[:, :, :D_MODEL], qg[:, :, D_MODEL:], kv_new[:, :, :D_MODEL], kv_new[:, :, D_MODEL:])]
    new_rows = jnp.concatenate(per_head, axis=-1).transpose(0, 2, 1, 3)

    def slot_map(hh, ti, r):
        def index_map(bi, hg, sel):
            h = hg * heads + hh
            return (bi, h, sel[((bi * t + ti) * H_B + h) * MOBA_TOPK + r], 0)
        return index_map

    slot_specs = [pl.BlockSpec((None, None, MOBA_BLOCK, 2 * DH_B), slot_map(hh, ti, r))
                  for hh in range(heads) for ti in range(t) for r in range(MOBA_TOPK)]

    kernel = functools.partial(_moba_sample_kernel, t=t, n_slots=n_slots, heads=heads)
    return pl.pallas_call(
        kernel,
        grid_spec=pltpu.PrefetchScalarGridSpec(
            num_scalar_prefetch=1,
            grid=(b, H_B // heads),
            in_specs=[pl.BlockSpec((None, heads, t, 4 * DH_B), lambda bi, hg, sel: (bi, hg, 0, 0))] + slot_specs,
            out_specs=pl.BlockSpec((None, t, heads * DH_B), lambda bi, hg, sel: (bi, 0, hg)),
        ),
        out_shape=jax.ShapeDtypeStruct((b, t, D_MODEL), F32),
        compiler_params=_params("parallel", "arbitrary"),
        name="moba_sample",
    )(sel_flat, new_rows, *([past_kv] * len(slot_specs)))


def _run_group(x, mods, mods_kv, mods_f, s_in, past, weights, paged):
    (norm_g, w_in_a, w_out_a, w_q_b, w_o_b, kv_norm_g, w_kv, final_g) = weights
    b, t, d = x.shape
    rows = b * t
    prompt = paged is None
    pos = past + jnp.arange(t, dtype=jnp.int32)
    if prompt:
        tm, tn, tm_norm, rpb = 1024, 1024, 512, t
        tm_res = 2048
        act_dtype = BF16
        expand = lambda v: v[:, None, :]
    else:
        tm, tn, tm_norm, rpb = rows, 1024, rows, 0
        tm_res = rows
        act_dtype = F32
        expand = lambda v: jnp.repeat(v, t, axis=0)

    def project(x2, g, shift, scale, w, layer):
        if prompt:
            return _norm_matmul(x2, g[None, :], shift, scale, w, layer, rpb, tm, tn)
        h = _norm_mod(x2, g[None, :], shift, scale, rpb, act_dtype, tm_norm)
        return _matmul(h, w, layer, tm, tn)

    x2 = x.reshape(rows, d)
    new_s = None
    kv = k_new = v_new = k_hm = vt_hm = None
    for l in range(DEPTH):
        shift, scale, gate = (expand(mods[l][:, i * d:(i + 1) * d]) for i in range(3))
        if l < N_A and prompt:
            h = _norm_mod(x2, norm_g[l][None, :], shift, scale, rpb, BF16, tm_norm)
            a, new_s = _proj_retention(h, w_in_a, l, b, t, pos, act_dtype, new_s)
            x2 = _matmul(a.reshape(rows, V_A), w_out_a, l, tm_res, tn, res=x2, gate=gate, rows_per_batch=rpb)
        elif l < N_A:
            qkvg = project(x2, norm_g[l], shift, scale, w_in_a, l).reshape(b, t, -1)
            a, new_s = _retention(qkvg, s_in, l, pos, act_dtype, new_s)
            x2 = _matmul(a.reshape(rows, V_A), w_out_a, l, tm_res, tn, res=x2, gate=gate, rows_per_batch=rpb)
        else:
            lb = l - N_A
            if prompt:
                h = _norm_mod(x2, norm_g[l][None, :], shift, scale, rpb, BF16, tm_norm)
                a = _moba_prompt(h, w_q_b, lb, b, t, k_hm, vt_hm, act_dtype)
            else:
                qg = project(x2, norm_g[l], shift, scale, w_q_b, lb).reshape(b, t, -1)
                past_kv, means = paged
                sel = _moba_select(qg, means, past)[:, :, :MOBA_TOPK].reshape(-1)
                a = _moba_sample(qg, kv.reshape(b, t, -1), past_kv, sel)
            x2 = _matmul(a.reshape(rows, d), w_o_b, lb, tm_res, tn, res=x2, gate=gate, rows_per_batch=rpb)
        if l == N_A - 1:
            kv_shift, kv_scale = (expand(mods_kv[:, i * d:(i + 1) * d]) for i in range(2))
            kv = project(x2, kv_norm_g, kv_shift, kv_scale, w_kv[None], 0)
            if prompt:
                k_new, v_new, k_hm, vt_hm = _split_heads(kv, b, t)
            else:
                k_new = kv[:, :d].reshape(b, t, H_B, DH_B)
                v_new = kv[:, d:].reshape(b, t, H_B, DH_B)
    f_shift, f_scale = (expand(mods_f[:, i * d:(i + 1) * d]) for i in range(2))
    y = _norm_mod(x2, final_g[None, :], f_shift, f_scale, rpb, F32, tm_norm)
    return y.reshape(b, t, d), new_s, k_new, v_new


def kernel(x_prompt, x_sample, state_ret, cache_k, cache_v, page_table, c_prompt, c_sample,
           norm_g, w_mod, b_mod, w_in_a, w_out_a, w_q_b, w_o_b,
           kv_norm_g, w_mod_kv, b_mod_kv, w_kv, final_g, w_mod_f, b_mod_f):
    bp = x_prompt.shape[0]
    bd, n_pages = page_table.shape
    n_pool, page, _, _ = cache_k.shape
    past_len = n_pages * page
    assert bp + bd <= MOD_ROWS

    c_all = jnp.concatenate([c_prompt, c_sample, jnp.zeros((MOD_ROWS - bp - bd, D_MODEL), F32)], axis=0)
    mods = _mod_matmul(c_all, w_mod, b_mod[:, None, :])
    mods_kv = _mod_matmul(c_all, w_mod_kv[None], b_mod_kv[None, None, :])[0]
    mods_f = _mod_matmul(c_all, w_mod_f[None], b_mod_f[None, None, :])[0]

    weights = (norm_g, w_in_a, w_out_a, w_q_b, w_o_b, kv_norm_g, w_kv, final_g)

    y_p, s_p, k_p, v_p = _run_group(
        x_prompt, mods[:, :bp], mods_kv[:bp], mods_f[:bp], None, 0, weights, None)

    pt_flat = page_table.reshape(-1)
    past_kv, means = _gather_cache(cache_k, cache_v, pt_flat, bd, n_pages)
    means = means.reshape(bd, -1, H_B * DH_B)
    y_s, s_s, k_s, v_s = _run_group(
        x_sample, mods[:, bp:bp + bd], mods_kv[bp:bp + bd], mods_f[bp:bp + bd], state_ret, past_len,
        weights, (past_kv, means))
    return (y_p, y_s, s_p, s_s, k_p, v_p, k_s, v_s)
```

```python
import functools
import math

import jax
import jax.numpy as jnp
from jax import lax
from jax.experimental import pallas as pl
from jax.experimental.pallas import tpu as pltpu

F32 = jnp.float32
BF16 = jnp.bfloat16

D_MODEL = 2048
DEPTH = 4
N_A = DEPTH // 2
H_A = 8
DK_A = D_MODEL // H_A
DV_A = 2 * DK_A
QK_A = H_A * DK_A
V_A = H_A * DV_A
RET_CHUNK = 256
H_B = 16
DH_B = D_MODEL // H_B
MOBA_BLOCK = 256
MOBA_TOPK = 3
ROPE_BASE = 10000.0
EPS = 1e-6
LOG2_E = math.log2(math.e)

VMEM_LIMIT_BYTES = 56 * 1024 * 1024
MOD_ROWS = 16


def _params(*sem):
    return pltpu.CompilerParams(dimension_semantics=sem, vmem_limit_bytes=VMEM_LIMIT_BYTES)


def _mod_kernel(c_ref, w_ref, b_ref, o_ref):
    acc = jnp.dot(c_ref[...].astype(BF16), w_ref[...].astype(BF16), preferred_element_type=F32)
    o_ref[...] = acc + b_ref[...]


def _mod_matmul(c, w, b, tn=1024):
    n_l, d, n = w.shape
    return pl.pallas_call(
        _mod_kernel,
        grid=(n_l, n // tn),
        in_specs=[
            pl.BlockSpec((MOD_ROWS, d), lambda l, j: (0, 0)),
            pl.BlockSpec((None, d, tn), lambda l, j: (l, 0, j)),
            pl.BlockSpec((None, 1, tn), lambda l, j: (l, 0, j)),
        ],
        out_specs=pl.BlockSpec((None, MOD_ROWS, tn), lambda l, j: (l, 0, j)),
        out_shape=jax.ShapeDtypeStruct((n_l, MOD_ROWS, n), F32),
        compiler_params=_params("parallel", "parallel"),
        name="mod_matmul",
    )(c, w, b)


def _norm_mod_kernel(x_ref, g_ref, sh_ref, sc_ref, o_ref):
    x = x_ref[...]
    r = lax.rsqrt(jnp.mean(x * x, axis=-1, keepdims=True) + EPS)
    y = (x * r) * g_ref[...]
    o_ref[...] = (y * (1.0 + sc_ref[...]) + sh_ref[...]).astype(o_ref.dtype)


def _norm_mod(x, g, shift, scale, rows_per_batch, out_dtype, tm):
    r, d = x.shape
    if rows_per_batch:
        tiles_per_b = rows_per_batch // tm
        mod_spec = pl.BlockSpec((None, 1, d), lambda i: (i // tiles_per_b, 0, 0))
    else:
        mod_spec = pl.BlockSpec((tm, d), lambda i: (i, 0))
    return pl.pallas_call(
        _norm_mod_kernel,
        grid=(r // tm,),
        in_specs=[
            pl.BlockSpec((tm, d), lambda i: (i, 0)),
            pl.BlockSpec((1, d), lambda i: (0, 0)),
            mod_spec,
            mod_spec,
        ],
        out_specs=pl.BlockSpec((tm, d), lambda i: (i, 0)),
        out_shape=jax.ShapeDtypeStruct((r, d), out_dtype),
        compiler_params=_params("parallel"),
        name="norm_mod",
    )(x, g, shift, scale)


def _matmul_kernel(a_ref, w_ref, o_ref):
    o_ref[...] = jnp.dot(
        a_ref[...].astype(BF16), w_ref[...].astype(BF16), preferred_element_type=F32
    ).astype(o_ref.dtype)


def _matmul_res_kernel(a_ref, w_ref, x_ref, gm_ref, o_ref):
    acc = jnp.dot(a_ref[...].astype(BF16), w_ref[...].astype(BF16), preferred_element_type=F32)
    o_ref[...] = x_ref[...] + gm_ref[...] * acc


MATMUL_W_TILE_ELEMS = 2 * 1024 * 1024
MATMUL_A_TILE_ELEMS = 4 * 1024 * 1024


def _matmul(a, w, layer, tm, tn, res=None, gate=None, rows_per_batch=0):
    r, k = a.shape
    n = w.shape[2]
    tm = min(tm, MATMUL_A_TILE_ELEMS // k)
    tn = min(tn, MATMUL_W_TILE_ELEMS // k, MATMUL_W_TILE_ELEMS * 1024 // (tm * k))
    in_specs = [
        pl.BlockSpec((tm, k), lambda i, j: (i, 0)),
        pl.BlockSpec((None, k, tn), lambda i, j: (layer, 0, j)),
    ]
    args = [a, w]
    kernel = _matmul_kernel
    if res is not None:
        kernel = _matmul_res_kernel
        in_specs.append(pl.BlockSpec((tm, tn), lambda i, j: (i, j)))
        if rows_per_batch:
            tiles_per_b = rows_per_batch // tm
            in_specs.append(pl.BlockSpec((None, 1, tn), lambda i, j: (i // tiles_per_b, 0, j)))
        else:
            in_specs.append(pl.BlockSpec((tm, tn), lambda i, j: (i, j)))
        args += [res, gate]
    return pl.pallas_call(
        kernel,
        grid=(r // tm, n // tn),
        in_specs=in_specs,
        out_specs=pl.BlockSpec((tm, tn), lambda i, j: (i, j)),
        out_shape=jax.ShapeDtypeStruct((r, n), F32),
        compiler_params=_params("parallel", "parallel"),
        name="matmul_res" if res is not None else "matmul",
    )(*args)


def _cast_kernel(x_ref, o_ref):
    o_ref[...] = x_ref[...].astype(o_ref.dtype)


def _cast_bf16(w, rows=512):
    n_l, k, n = w.shape
    spec = pl.BlockSpec((None, rows, n), lambda l, i: (l, i, 0))
    return pl.pallas_call(
        _cast_kernel,
        grid=(n_l, k // rows),
        in_specs=[spec],
        out_specs=spec,
        out_shape=jax.ShapeDtypeStruct(w.shape, BF16),
        compiler_params=_params("parallel", "parallel"),
        name="cast_bf16",
    )(w)


def _matmul_res_norm_kernel(a_ref, w_ref, x_ref, gm_ref, g_ref, sh_ref, sc_ref, *out_refs):
    acc = jnp.dot(a_ref[...].astype(BF16), w_ref[...], preferred_element_type=F32)
    x = x_ref[...] + gm_ref[...] * acc
    if len(out_refs) == 2:
        out_refs[0][...] = x
    r = lax.rsqrt(jnp.mean(x * x, axis=-1, keepdims=True) + EPS)
    y = (x * r) * g_ref[...]
    out_refs[-1][...] = (y * (1.0 + sc_ref[...]) + sh_ref[...]).astype(out_refs[-1].dtype)


def _matmul_res_norm(a, w_bf16, layer, res, gate, g, shift, scale, rows_per_batch, tm, norm_dtype, keep_x):
    r, k = a.shape
    n = w_bf16.shape[2]
    tiles_per_b = rows_per_batch // tm
    row_spec = pl.BlockSpec((tm, n), lambda i: (i, 0))
    mod_spec = pl.BlockSpec((None, 1, n), lambda i: (i // tiles_per_b, 0, 0))
    out_specs = [row_spec, row_spec] if keep_x else [row_spec]
    out_shape = [jax.ShapeDtypeStruct((r, n), F32)] if keep_x else []
    out_shape.append(jax.ShapeDtypeStruct((r, n), norm_dtype))
    outs = pl.pallas_call(
        _matmul_res_norm_kernel,
        grid=(r // tm,),
        in_specs=[
            pl.BlockSpec((tm, k), lambda i: (i, 0)),
            pl.BlockSpec((None, k, n), lambda i: (layer, 0, 0), pipeline_mode=pl.Buffered(1)),
            row_spec,
            mod_spec,
            pl.BlockSpec((1, n), lambda i: (0, 0)),
            mod_spec,
            mod_spec,
        ],
        out_specs=out_specs,
        out_shape=out_shape,
        compiler_params=_params("parallel"),
        name="matmul_res_norm",
    )(a, w_bf16, res, gate, g, shift, scale)
    return (outs[0], outs[1]) if keep_x else (None, outs[0])


def _norm_matmul_kernel(x_ref, g_ref, sh_ref, sc_ref, w_ref, o_ref, h_scr):
    @pl.when(pl.program_id(1) == 0)
    def _():
        x = x_ref[...]
        r = lax.rsqrt(jnp.mean(x * x, axis=-1, keepdims=True) + EPS)
        y = (x * r) * g_ref[...]
        h_scr[...] = (y * (1.0 + sc_ref[...]) + sh_ref[...]).astype(BF16)

    o_ref[...] = jnp.dot(h_scr[...], w_ref[...].astype(BF16), preferred_element_type=F32)


def _norm_matmul(x, g, shift, scale, w, layer, rows_per_batch, tm, tn):
    r, k = x.shape
    n = w.shape[2]
    tn = min(tn, MATMUL_W_TILE_ELEMS // k)
    tiles_per_b = rows_per_batch // tm
    mod_spec = pl.BlockSpec((None, 1, k), lambda i, j: (i // tiles_per_b, 0, 0))
    return pl.pallas_call(
        _norm_matmul_kernel,
        grid=(r // tm, n // tn),
        in_specs=[
            pl.BlockSpec((tm, k), lambda i, j: (i, 0)),
            pl.BlockSpec((1, k), lambda i, j: (0, 0)),
            mod_spec,
            mod_spec,
            pl.BlockSpec((None, k, tn), lambda i, j: (layer, 0, j)),
        ],
        out_specs=pl.BlockSpec((tm, tn), lambda i, j: (i, j)),
        out_shape=jax.ShapeDtypeStruct((r, n), F32),
        scratch_shapes=[pltpu.VMEM((tm, k), BF16)],
        compiler_params=_params("parallel", "arbitrary"),
        name="norm_matmul",
    )(x, g, shift, scale, w)


def _split_heads_kernel(kv_ref, k_out_ref, v_out_ref, k_hm_ref, vt_hm_ref, *, tm):
    for h in range(H_B):
        kh = kv_ref[:, h * DH_B:(h + 1) * DH_B]
        vh = kv_ref[:, (H_B + h) * DH_B:(H_B + h + 1) * DH_B]
        k_out_ref[pl.ds(h, tm, stride=H_B), :] = kh
        v_out_ref[pl.ds(h, tm, stride=H_B), :] = vh
        k_hm_ref[h] = kh.astype(BF16)
        vt_hm_ref[h] = vh.T.astype(BF16)


def _split_heads(kv, b, t, tm=256):
    r = b * t
    tiles_per_b = t // tm
    kernel = functools.partial(_split_heads_kernel, tm=tm)
    rows_spec = pl.BlockSpec((tm * H_B, DH_B), lambda i: (i, 0))
    k_out, v_out, k_hm, vt_hm = pl.pallas_call(
        kernel,
        grid=(r // tm,),
        in_specs=[pl.BlockSpec((tm, 2 * D_MODEL), lambda i: (i, 0))],
        out_specs=[
            rows_spec,
            rows_spec,
            pl.BlockSpec((None, H_B, tm, DH_B), lambda i: (i // tiles_per_b, 0, i % tiles_per_b, 0)),
            pl.BlockSpec((None, H_B, DH_B, tm), lambda i: (i // tiles_per_b, 0, 0, i % tiles_per_b)),
        ],
        out_shape=[
            jax.ShapeDtypeStruct((r * H_B, DH_B), F32),
            jax.ShapeDtypeStruct((r * H_B, DH_B), F32),
            jax.ShapeDtypeStruct((b, H_B, t, DH_B), BF16),
            jax.ShapeDtypeStruct((b, H_B, DH_B, t), BF16),
        ],
        compiler_params=_params("parallel"),
        name="split_heads",
    )(kv)
    return k_out.reshape(b, t, H_B, DH_B), v_out.reshape(b, t, H_B, DH_B), k_hm, vt_hm


def _rope_tables(pos):
    half = DK_A // 2
    inv = 1.0 / (ROPE_BASE ** jnp.linspace(0.0, 1.0, half, dtype=F32))
    ang = pos.astype(F32)[:, None] * inv[None, :]
    return jnp.cos(ang), jnp.sin(ang)


def _retention_tables(chunk):
    log_g = jnp.log1p(-(2.0 ** (-5.0 - jnp.arange(H_A, dtype=F32))))
    i = jnp.arange(chunk, dtype=F32)
    diff = i[:, None] - i[None, :]
    decay = jnp.where(diff >= 0, jnp.exp(jnp.maximum(diff, 0.0)[None] * log_g[:, None, None]), 0.0)
    cross_scale = jnp.exp((i + 1.0)[:, None] * log_g[None, :]).T[:, :, None]
    wk = jnp.exp((chunk - 1.0 - i)[:, None] * log_g[None, :]).T[:, :, None]
    state_decay = jnp.exp(chunk * log_g)
    return decay, cross_scale, wk, state_decay


def _retention_chunk(q, k, v, g, cos, sin, decay, cross_scale, wk, state_decay, s):
    half = DK_A // 2

    def rot(x):
        x1 = x[:, :half]
        x2 = x[:, half:]
        return jnp.concatenate([x1 * cos - x2 * sin, x1 * sin + x2 * cos], axis=-1)

    q = rot(q)
    k = rot(k) * (DK_A ** -0.5)
    vb = v.astype(BF16)
    qb = q.astype(BF16)
    kb = k.astype(BF16)
    scores = lax.dot_general(qb, kb, (((1,), (1,)), ((), ())), preferred_element_type=F32) * decay
    inner = jnp.dot(scores.astype(BF16), vb, preferred_element_type=F32)
    cross = jnp.dot(qb, s.astype(BF16), preferred_element_type=F32) * cross_scale
    kw = (k * wk).astype(BF16)
    upd = lax.dot_general(kw, vb, (((0,), (0,)), ((), ())), preferred_element_type=F32)
    s_new = state_decay * s + upd
    o = inner + cross
    o = o * lax.rsqrt(jnp.mean(o * o, axis=-1, keepdims=True) + EPS)
    return g * jax.nn.sigmoid(g) * o, s_new


RET_HEADS_PER_STEP = 4
RET_CHUNKS_PER_STEP = 1


def _retention_kernel(sdec_ref, q_ref, k_ref, v_ref, g_ref, cos_ref, sin_ref, dec_ref, cs_ref,
                      wk_ref, *refs, chunk, heads, chunks, n_steps, zero_state, layer, init_stack):
    a_ref, sout_ref, s_scr = refs[-3:]
    hg = pl.program_id(1)
    c = pl.program_id(2)

    @pl.when(c == 0)
    def _():
        s_scr[...] = jnp.zeros(s_scr.shape, F32) if zero_state else refs[0][...]

    for hh in range(heads):
        s = s_scr[hh]
        kcols = slice(hh * DK_A, (hh + 1) * DK_A)
        vcols = slice(hh * DV_A, (hh + 1) * DV_A)
        for ci in range(chunks):
            rows = slice(ci * chunk, (ci + 1) * chunk)
            a, s = _retention_chunk(
                q_ref[rows, kcols], k_ref[rows, kcols], v_ref[rows, vcols], g_ref[rows, vcols],
                cos_ref[rows, :], sin_ref[rows, :], dec_ref[hh], cs_ref[hh], wk_ref[hh],
                sdec_ref[hg * heads + hh], s)
            a_ref[rows, vcols] = a.astype(a_ref.dtype)
        s_scr[hh] = s

        @pl.when(c == n_steps - 1)
        def _():
            _store_state(sout_ref, hh, s, layer, init_stack)


def _store_state(sout_ref, idx, s, layer, init_stack):
    if not init_stack:
        sout_ref[idx] = s
        return
    for slab in range(sout_ref.shape[0]):
        sout_ref[slab, idx] = s if slab == layer else jnp.zeros(s.shape, s.dtype)


def _retention(qkvg, s_in, layer, pos, out_dtype, s_out_prev):
    b, t, _ = qkvg.shape
    chunk = math.gcd(t, RET_CHUNK)
    n_chunks = t // chunk
    heads = RET_HEADS_PER_STEP
    chunks = math.gcd(n_chunks, RET_CHUNKS_PER_STEP)
    n_steps = n_chunks // chunks
    rows = chunk * chunks
    assert H_A % heads == 0
    cos, sin = _rope_tables(pos)
    decay, cross_scale, wk, state_decay = _retention_tables(chunk)
    k_off = QK_A // (heads * DK_A)
    v_off = 2 * QK_A // (heads * DV_A)
    g_off = (2 * QK_A + V_A) // (heads * DV_A)
    init_stack = s_out_prev is None
    kernel = functools.partial(_retention_kernel, chunk=chunk, heads=heads, chunks=chunks, n_steps=n_steps,
                               zero_state=s_in is None, layer=layer, init_stack=init_stack)
    state_spec = pl.BlockSpec((None, None, heads, DK_A, DV_A), lambda bi, h, c: (layer, bi, h, 0, 0))
    state_args, state_specs, aliases = [], [], {}
    if s_in is not None:
        state_args.append(s_in)
        state_specs.append(state_spec)
    if init_stack:
        out_state_spec = pl.BlockSpec((N_A, None, heads, DK_A, DV_A), lambda bi, h, c: (0, bi, h, 0, 0))
    else:
        out_state_spec = state_spec
        aliases = {10 + len(state_args): 1}
        state_args.append(s_out_prev)
        state_specs.append(pl.BlockSpec(memory_space=pl.ANY))
    return pl.pallas_call(
        kernel,
        grid=(b, H_A // heads, n_steps),
        in_specs=[
            pl.BlockSpec(memory_space=pltpu.SMEM),
            pl.BlockSpec((None, rows, heads * DK_A), lambda bi, h, c: (bi, c, h)),
            pl.BlockSpec((None, rows, heads * DK_A), lambda bi, h, c: (bi, c, k_off + h)),
            pl.BlockSpec((None, rows, heads * DV_A), lambda bi, h, c: (bi, c, v_off + h)),
            pl.BlockSpec((None, rows, heads * DV_A), lambda bi, h, c: (bi, c, g_off + h)),
            pl.BlockSpec((rows, DK_A // 2), lambda bi, h, c: (c, 0)),
            pl.BlockSpec((rows, DK_A // 2), lambda bi, h, c: (c, 0)),
            pl.BlockSpec((heads, chunk, chunk), lambda bi, h, c: (h, 0, 0)),
            pl.BlockSpec((heads, chunk, 1), lambda bi, h, c: (h, 0, 0)),
            pl.BlockSpec((heads, chunk, 1), lambda bi, h, c: (h, 0, 0)),
        ] + state_specs,
        out_specs=[
            pl.BlockSpec((None, rows, heads * DV_A), lambda bi, h, c: (bi, c, h)),
            out_state_spec,
        ],
        out_shape=[
            jax.ShapeDtypeStruct((b, t, V_A), out_dtype),
            jax.ShapeDtypeStruct((N_A, b, H_A, DK_A, DV_A), F32),
        ],
        scratch_shapes=[pltpu.VMEM((heads, DK_A, DV_A), F32)],
        input_output_aliases=aliases,
        compiler_params=_params("parallel", "parallel", "arbitrary"),
        name="retention",
    )(state_decay, qkvg, qkvg, qkvg, qkvg, cos, sin, decay, cross_scale, wk, *state_args)


PROJ_RET_ROWS = 1024


def _proj_retention_kernel(sdec_ref, h_ref, wq_ref, wkey_ref, wv_ref, wg_ref, cos_ref, sin_ref, dec_ref,
                           cs_ref, kdecay_ref, *refs, chunk, n_steps, layer, init_stack):
    a_ref, sout_ref, qkvg_scr, s_scr = refs[-4:]
    head = pl.program_id(1)
    r = pl.program_id(2)

    @pl.when(r == 0)
    def _():
        s_scr[...] = jnp.zeros(s_scr.shape, F32)

    hb = h_ref[...]
    col = 0
    for w_ref in (wq_ref, wkey_ref, wv_ref, wg_ref):
        width = w_ref.shape[1]
        qkvg_scr[:, col:col + width] = jnp.dot(hb, w_ref[...].astype(BF16), preferred_element_type=F32)
        col += width

    s = s_scr[...]
    for ci in range(h_ref.shape[0] // chunk):
        rows = slice(ci * chunk, (ci + 1) * chunk)
        a, s = _retention_chunk(
            qkvg_scr[rows, 0:DK_A], qkvg_scr[rows, DK_A:2 * DK_A],
            qkvg_scr[rows, 2 * DK_A:2 * DK_A + DV_A], qkvg_scr[rows, 2 * DK_A + DV_A:],
            cos_ref[rows, :], sin_ref[rows, :], dec_ref[...], cs_ref[...], kdecay_ref[...],
            sdec_ref[head], s)
        a_ref[rows, :] = a.astype(a_ref.dtype)
    s_scr[...] = s

    @pl.when(r == n_steps - 1)
    def _():
        _store_state(sout_ref, 0, s, layer, init_stack)


def _proj_retention(h_act, w_in, layer, b, t, pos, out_dtype, s_out_prev):
    chunk = math.gcd(t, RET_CHUNK)
    rows = min(PROJ_RET_ROWS, t)
    assert t % rows == 0 and rows % chunk == 0
    n_steps = t // rows
    cos, sin = _rope_tables(pos)
    decay, cross_scale, wk, state_decay = _retention_tables(chunk)
    k_off = QK_A // DK_A
    v_off = 2 * QK_A // DV_A
    g_off = (2 * QK_A + V_A) // DV_A
    d = h_act.shape[1]
    init_stack = s_out_prev is None
    kernel = functools.partial(_proj_retention_kernel, chunk=chunk, n_steps=n_steps, layer=layer,
                               init_stack=init_stack)
    extra_args, extra_specs, aliases = [], [], {}
    if init_stack:
        state_spec = pl.BlockSpec((N_A, None, 1, DK_A, DV_A), lambda bi, h, r: (0, bi, h, 0, 0))
    else:
        state_spec = pl.BlockSpec((None, None, 1, DK_A, DV_A), lambda bi, h, r: (layer, bi, h, 0, 0))
        aliases = {11: 1}
        extra_args.append(s_out_prev)
        extra_specs.append(pl.BlockSpec(memory_space=pl.ANY))
    return pl.pallas_call(
        kernel,
        grid=(b, H_A, n_steps),
        in_specs=[
            pl.BlockSpec(memory_space=pltpu.SMEM),
            pl.BlockSpec((rows, d), lambda bi, h, r: (bi * n_steps + r, 0)),
            pl.BlockSpec((None, d, DK_A), lambda bi, h, r: (layer, 0, h)),
            pl.BlockSpec((None, d, DK_A), lambda bi, h, r: (layer, 0, k_off + h)),
            pl.BlockSpec((None, d, DV_A), lambda bi, h, r: (layer, 0, v_off + h)),
            pl.BlockSpec((None, d, DV_A), lambda bi, h, r: (layer, 0, g_off + h)),
            pl.BlockSpec((rows, DK_A // 2), lambda bi, h, r: (r, 0)),
            pl.BlockSpec((rows, DK_A // 2), lambda bi, h, r: (r, 0)),
            pl.BlockSpec((None, chunk, chunk), lambda bi, h, r: (h, 0, 0)),
            pl.BlockSpec((None, chunk, 1), lambda bi, h, r: (h, 0, 0)),
            pl.BlockSpec((None, chunk, 1), lambda bi, h, r: (h, 0, 0)),
        ] + extra_specs,
        out_specs=[
            pl.BlockSpec((None, rows, DV_A), lambda bi, h, r: (bi, r, h)),
            state_spec,
        ],
        out_shape=[
            jax.ShapeDtypeStruct((b, t, V_A), out_dtype),
            jax.ShapeDtypeStruct((N_A, b, H_A, DK_A, DV_A), F32),
        ],
        scratch_shapes=[
            pltpu.VMEM((rows, 2 * DK_A + 2 * DV_A), F32),
            pltpu.VMEM((DK_A, DV_A), F32),
        ],
        input_output_aliases=aliases,
        compiler_params=_params("parallel", "parallel", "arbitrary"),
        name="proj_retention",
    )(state_decay, h_act, w_in, w_in, w_in, w_in, cos, sin, decay, cross_scale, wk, *extra_args)


def _select_topk(gate, n_sel, axis=1):
    pos = lax.broadcasted_iota(jnp.int32, gate.shape, axis).astype(F32)
    picks = []
    for _ in range(n_sel):
        m = jnp.max(gate, axis=axis, keepdims=True)
        idx = jnp.min(jnp.where(gate == m, pos, float(gate.shape[axis])), axis=axis, keepdims=True)
        valid = m > -jnp.inf
        picks.append((idx, valid))
        gate = jnp.where((pos == idx) & valid, -jnp.inf, gate)
    return picks


MOBA_HEADS_PER_STEP = 2
MOBA_PROJ_HEADS = 2


def _moba_prompt_kernel(h_ref, wq_ref, wg_ref, k_ref, vt_ref, a_ref, q_scr, g_scr, *, n_blocks, heads):
    hb = h_ref[...]
    for h0 in range(0, heads, MOBA_PROJ_HEADS):
        cols = slice(h0 * DH_B, (h0 + MOBA_PROJ_HEADS) * DH_B)
        q_scr[:, cols] = jnp.dot(hb, wq_ref[:, cols].astype(BF16), preferred_element_type=F32)
        g_scr[:, cols] = jnp.dot(hb, wg_ref[:, cols].astype(BF16), preferred_element_type=F32)
    for hh in range(heads):
        cols = slice(hh * DH_B, (hh + 1) * DH_B)
        _moba_prompt_head(q_scr, g_scr, k_ref.at[hh], vt_ref.at[hh], a_ref, cols, n_blocks)


def _moba_prompt_head(q_ref, g_ref, k_ref, vt_ref, a_ref, cols, n_blocks):
    bs = MOBA_BLOCK
    scale = DH_B ** -0.5
    n_sel = min(MOBA_TOPK, n_blocks)
    kb = k_ref[...]
    ones_rows = 16
    vt = jnp.concatenate([vt_ref[...], jnp.ones((ones_rows, kb.shape[0]), BF16)], axis=0)
    means_b = jnp.mean(kb.astype(F32).reshape(n_blocks, bs, DH_B), axis=1).astype(BF16)
    key = lax.broadcasted_iota(jnp.int32, (bs, bs), 0)
    qry = lax.broadcasted_iota(jnp.int32, (bs, bs), 1)
    causal = key <= qry

    for i in range(n_blocks):
        rows = slice(i * bs, (i + 1) * bs)
        q = q_ref[rows, cols]
        qs = (q * (scale * LOG2_E)).astype(BF16)
        picks = None
        if i > n_sel:
            gate = lax.dot_general(means_b, q.astype(BF16), (((1,), (1,)), ((), ())),
                                   preferred_element_type=F32)
            blk = lax.broadcasted_iota(jnp.int32, gate.shape, 0)
            picks = _select_topk(jnp.where(blk < i, gate, -jnp.inf), n_sel, axis=0)
        blocks = []
        for j in range(i + 1):
            sj = lax.dot_general(kb[j * bs:(j + 1) * bs, :], qs, (((1,), (1,)), ((), ())),
                                 preferred_element_type=F32)
            if j == i:
                sj = jnp.where(causal, sj, -jnp.inf)
            elif picks is not None:
                chosen = picks[0][0] == float(j)
                for idx, _ in picks[1:]:
                    chosen = chosen | (idx == float(j))
                sj = jnp.where(chosen, sj, -jnp.inf)
            blocks.append(sj)
        m = jnp.max(functools.reduce(jnp.maximum, blocks), axis=0, keepdims=True)
        acc = jnp.zeros((DH_B + ones_rows, bs), F32)
        for j, sj in enumerate(blocks):
            p = jnp.exp2(sj - m).astype(BF16)
            acc = acc + jnp.dot(vt[:, j * bs:(j + 1) * bs], p, preferred_element_type=F32)
        o = (acc[:DH_B, :] / acc[DH_B:DH_B + 1, :]).T
        g = g_ref[rows, cols]
        a_ref[rows, cols] = (g * jax.nn.sigmoid(g) * o).astype(a_ref.dtype)


def _moba_prompt(h_act, w_q, layer, b, t, k_hm, vt_hm, out_dtype):
    d = h_act.shape[1]
    n_blocks = t // MOBA_BLOCK
    heads = MOBA_HEADS_PER_STEP
    assert n_blocks * MOBA_BLOCK == t and n_blocks <= 128 and H_B % heads == 0
    g_off = H_B // heads
    kernel = functools.partial(_moba_prompt_kernel, n_blocks=n_blocks, heads=heads)
    return pl.pallas_call(
        kernel,
        grid=(b, H_B // heads),
        in_specs=[
            pl.BlockSpec((t, d), lambda bi, h: (bi, 0)),
            pl.BlockSpec((None, d, heads * DH_B), lambda bi, h: (layer, 0, h)),
            pl.BlockSpec((None, d, heads * DH_B), lambda bi, h: (layer, 0, g_off + h)),
            pl.BlockSpec((None, heads, t, DH_B), lambda bi, h: (bi, h, 0, 0)),
            pl.BlockSpec((None, heads, DH_B, t), lambda bi, h: (bi, h, 0, 0)),
        ],
        out_specs=pl.BlockSpec((None, t, heads * DH_B), lambda bi, h: (bi, 0, h)),
        out_shape=jax.ShapeDtypeStruct((b, t, D_MODEL), out_dtype),
        scratch_shapes=[pltpu.VMEM((t, heads * DH_B), F32), pltpu.VMEM((t, heads * DH_B), F32)],
        compiler_params=_params("parallel", "arbitrary"),
        name="moba_prompt",
    )(h_act, w_q, w_q, k_hm, vt_hm)


HEAD_GROUP = 8


GATHER_PAGES_PER_STEP = 4


def _gather_cache_kernel(pt_ref, *refs, page, n_groups, pages_per_step, pages_per_block):
    n_slabs = pages_per_step * n_groups
    o_ref, m_ref = refs[2 * n_slabs], refs[2 * n_slabs + 1]
    for part in range(2):
        lanes = slice(part * DH_B, (part + 1) * DH_B)
        for pg in range(pages_per_step):
            for grp in range(n_groups):
                src = refs[part * n_slabs + pg * n_groups + grp]
                src = src.reshape(page * HEAD_GROUP, DH_B)
                for hg in range(HEAD_GROUP):
                    rows = src[pl.ds(hg, page, stride=HEAD_GROUP), :]
                    o_ref[grp * HEAD_GROUP + hg, pg * page:(pg + 1) * page, lanes] = rows.astype(BF16)
    for blk in range(pages_per_step // pages_per_block):
        for grp in range(n_groups):
            total = None
            for pg in range(blk * pages_per_block, (blk + 1) * pages_per_block):
                part_sum = jnp.sum(refs[pg * n_groups + grp][...], axis=0)
                total = part_sum if total is None else total + part_sum
            m_ref[blk, grp * HEAD_GROUP:(grp + 1) * HEAD_GROUP, :] = total / MOBA_BLOCK


def _gather_cache(cache_k, cache_v, page_table_flat, n_seq, n_pages):
    n_pool, page, n_h, dh = cache_k.shape
    pages_per_block = MOBA_BLOCK // page
    pps = GATHER_PAGES_PER_STEP
    assert n_h == H_B and dh == DH_B and n_h % HEAD_GROUP == 0
    assert pps % pages_per_block == 0 and n_pages % pps == 0
    n_blocks = n_pages // pages_per_block
    n_groups = n_h // HEAD_GROUP
    grouped = [c.reshape(n_pool, page, n_groups, HEAD_GROUP, dh) for c in (cache_k, cache_v)]

    def page_map(pg, grp):
        return lambda bi, j, pt: (pt[bi * n_pages + pps * j + pg], 0, grp, 0, 0)

    def slab_specs():
        return [pl.BlockSpec((None, page, None, HEAD_GROUP, dh), page_map(pg, grp))
                for pg in range(pps) for grp in range(n_groups)]

    n_slabs = pps * n_groups
    blocks_per_step = pps // pages_per_block
    kernel = functools.partial(_gather_cache_kernel, page=page, n_groups=n_groups, pages_per_step=pps,
                               pages_per_block=pages_per_block)
    return pl.pallas_call(
        kernel,
        grid_spec=pltpu.PrefetchScalarGridSpec(
            num_scalar_prefetch=1,
            grid=(n_seq, n_pages // pps),
            in_specs=slab_specs() + slab_specs(),
            out_specs=[
                pl.BlockSpec((None, n_h, pps * page, 2 * dh), lambda bi, j, pt: (bi, 0, j, 0)),
                pl.BlockSpec((None, blocks_per_step, n_h, dh), lambda bi, j, pt: (bi, j, 0, 0)),
            ],
        ),
        out_shape=[
            jax.ShapeDtypeStruct((n_seq, n_h, n_pages * page, 2 * dh), BF16),
            jax.ShapeDtypeStruct((n_seq, n_blocks, n_h, dh), F32),
        ],
        compiler_params=_params("parallel", "arbitrary"),
        name="gather_cache",
    )(page_table_flat, *([grouped[0]] * n_slabs), *([grouped[1]] * n_slabs))


def _moba_select_kernel(q_ref, means_ref, o_ref, *, t, n_blocks, past):
    q = q_ref[...]
    rows = t * H_B
    qrep = jnp.concatenate([jnp.broadcast_to(q[ti:ti + 1, :], (H_B, D_MODEL)) for ti in range(t)], axis=0)
    lane_h = lax.broadcasted_iota(jnp.int32, (rows, D_MODEL), 1) >> (DH_B.bit_length() - 1)
    row_h = lax.broadcasted_iota(jnp.int32, (rows, D_MODEL), 0) & (H_B - 1)
    qexp = jnp.where(lane_h == row_h, qrep, 0.0)
    means = jnp.concatenate([means_ref[...], jnp.zeros((128 - n_blocks, D_MODEL), F32)], axis=0)
    gate = lax.dot_general(qexp.astype(BF16), means.astype(BF16), (((1,), (1,)), ((), ())),
                           preferred_element_type=F32)
    lane = lax.broadcasted_iota(jnp.int32, gate.shape, 1)
    token = lax.broadcasted_iota(jnp.int32, gate.shape, 0) >> (H_B.bit_length() - 1)
    own = (past + token) >> (MOBA_BLOCK.bit_length() - 1)
    gate = jnp.where(lane < own, gate, -jnp.inf)
    out = jnp.zeros(gate.shape, F32)
    for r, (idx, _) in enumerate(_select_topk(gate, MOBA_TOPK)):
        out = jnp.where(lane == r, idx, out)
    o_ref[...] = out.astype(jnp.int32)


def _moba_select(qg, means, past):
    b, t, _ = qg.shape
    n_blocks = means.shape[1]
    assert MOBA_TOPK <= n_blocks <= 128
    kernel = functools.partial(_moba_select_kernel, t=t, n_blocks=n_blocks, past=past)
    return pl.pallas_call(
        kernel,
        grid=(b,),
        in_specs=[
            pl.BlockSpec((None, t, D_MODEL), lambda bi: (bi, 0, 0)),
            pl.BlockSpec((None, n_blocks, D_MODEL), lambda bi: (bi, 0, 0)),
        ],
        out_specs=pl.BlockSpec((None, t * H_B, 128), lambda bi: (bi, 0, 0)),
        out_shape=jax.ShapeDtypeStruct((b, t * H_B, 128), jnp.int32),
        compiler_params=_params("parallel"),
        name="moba_select",
    )(qg, means)


SAMPLE_HEADS_PER_STEP = 4


def _moba_sample_kernel(sel_ref, new_ref, *refs, t, n_slots, heads):
    a_ref = refs[heads * n_slots]
    for hh in range(heads):
        cols = slice(hh * DH_B, (hh + 1) * DH_B)
        _moba_sample_head(new_ref.at[hh], refs[hh * n_slots:(hh + 1) * n_slots], a_ref, cols, t, n_slots)


def _moba_sample_head(new_ref, kv_refs, a_ref, cols, t, n_slots):
    keys_per_token = (n_slots // t) * kv_refs[0].shape[0]

    q = new_ref[:, 0:DH_B]
    qb = q.astype(BF16)
    kall = jnp.concatenate([r[:, :DH_B] for r in kv_refs], axis=0)
    vall = jnp.concatenate([r[:, DH_B:] for r in kv_refs], axis=0)
    s = lax.dot_general(qb, kall, (((1,), (1,)), ((), ())), preferred_element_type=F32) * (DH_B ** -0.5)
    row = lax.broadcasted_iota(jnp.int32, s.shape, 0)
    col = lax.broadcasted_iota(jnp.int32, s.shape, 1)
    mine = (col >= row * keys_per_token) & (col < (row + 1) * keys_per_token)
    s = jnp.where(mine, s, -jnp.inf)

    kn = new_ref[:, 2 * DH_B:3 * DH_B]
    vn = new_ref[:, 3 * DH_B:4 * DH_B]
    rown = lax.broadcasted_iota(jnp.int32, (t, 1), 0)
    s_own = []
    for tj in range(t):
        sj = jnp.sum(q * kn[tj:tj + 1, :], axis=-1, keepdims=True) * (DH_B ** -0.5)
        s_own.append(jnp.where(rown >= tj, sj, -jnp.inf))
    m = jnp.max(s, axis=-1, keepdims=True)
    for sj in s_own:
        m = jnp.maximum(m, sj)
    p = jnp.exp(s - m)
    l = jnp.sum(p, axis=-1, keepdims=True)
    acc = jnp.dot(p.astype(BF16), vall, preferred_element_type=F32)
    for tj, sj in enumerate(s_own):
        pj = jnp.exp(sj - m)
        l = l + pj
        acc = acc + pj * vn[tj:tj + 1, :]
    o = acc / l
    g = new_ref[:, DH_B:2 * DH_B]
    a_ref[:, cols] = (g * jax.nn.sigmoid(g) * o).astype(a_ref.dtype)


def _moba_sample(qg, kv_new, past_kv, sel_flat):
    b, t, _ = qg.shape
    n_slots = t * MOBA_TOPK
    heads = SAMPLE_HEADS_PER_STEP
    assert H_B % heads == 0
    per_head = [a.reshape(b, t, H_B, DH_B) for a in
                (qg[:, :, :D_MODEL], qg[:, :, D_MODEL:], kv_new[:, :, :D_MODEL], kv_new[:, :, D_MODEL:])]
    new_rows = jnp.concatenate(per_head, axis=-1).transpose(0, 2, 1, 3)

    def slot_map(hh, ti, r):
        def index_map(bi, hg, sel):
            h = hg * heads + hh
            return (bi, h, sel[((bi * t + ti) * H_B + h) * MOBA_TOPK + r], 0)
        return index_map

    slot_specs = [pl.BlockSpec((None, None, MOBA_BLOCK, 2 * DH_B), slot_map(hh, ti, r))
                  for hh in range(heads) for ti in range(t) for r in range(MOBA_TOPK)]

    kernel = functools.partial(_moba_sample_kernel, t=t, n_slots=n_slots, heads=heads)
    return pl.pallas_call(
        kernel,
        grid_spec=pltpu.PrefetchScalarGridSpec(
            num_scalar_prefetch=1,
            grid=(b, H_B // heads),
            in_specs=[pl.BlockSpec((None, heads, t, 4 * DH_B), lambda bi, hg, sel: (bi, hg, 0, 0))] + slot_specs,
            out_specs=pl.BlockSpec((None, t, heads * DH_B), lambda bi, hg, sel: (bi, 0, hg)),
        ),
        out_shape=jax.ShapeDtypeStruct((b, t, D_MODEL), F32),
        compiler_params=_params("parallel", "arbitrary"),
        name="moba_sample",
    )(sel_flat, new_rows, *([past_kv] * len(slot_specs)))


PROMPT_ROWS = 1024
PROMPT_NORM_ROWS = 512
PROMPT_RES_ROWS = 512
SAMPLE_COLS = 1024


def _split_mod(v, n_parts, d, expand):
    return tuple(expand(v[:, i * d:(i + 1) * d]) for i in range(n_parts))


def _run_prompt(x, mods, mods_kv, mods_f, weights):
    (norm_g, w_in_a, w_out_a, w_q_b, w_o_b, kv_norm_g, w_kv, final_g) = weights
    b, t, d = x.shape
    rows = b * t
    pos = jnp.arange(t, dtype=jnp.int32)
    per_batch = lambda v: v[:, None, :]
    x2 = x.reshape(rows, d)
    shift, scale, _ = _split_mod(mods[0], 3, d, per_batch)
    h = _norm_mod(x2, norm_g[0][None, :], shift, scale, t, BF16, PROMPT_NORM_ROWS)
    new_s = k_new = v_new = k_hm = vt_hm = None
    for l in range(DEPTH):
        gate = _split_mod(mods[l], 3, d, per_batch)[2]
        if l < N_A:
            a, new_s = _proj_retention(h, w_in_a, l, b, t, pos, BF16, new_s)
            a, w_res, w_layer = a.reshape(rows, V_A), w_out_a, l
        else:
            a = _moba_prompt(h, w_q_b, l - N_A, b, t, k_hm, vt_hm, BF16)
            a, w_res, w_layer = a.reshape(rows, d), w_o_b, l - N_A
        last = l == DEPTH - 1
        if last:
            g_next = final_g
            sh_next, sc_next = _split_mod(mods_f, 2, d, per_batch)
        else:
            g_next = norm_g[l + 1]
            sh_next, sc_next, _ = _split_mod(mods[l + 1], 3, d, per_batch)
        x2, h = _matmul_res_norm(a, w_res, w_layer, x2, gate, g_next[None, :], sh_next, sc_next, t,
                                 PROMPT_RES_ROWS, F32 if last else BF16, keep_x=not last)
        if l == N_A - 1:
            kv_shift, kv_scale = _split_mod(mods_kv, 2, d, per_batch)
            kv = _norm_matmul(x2, kv_norm_g[None, :], kv_shift, kv_scale, w_kv[None], 0, t,
                              PROMPT_ROWS, PROMPT_ROWS)
            k_new, v_new, k_hm, vt_hm = _split_heads(kv, b, t)
    return h.reshape(b, t, d), new_s, k_new, v_new


def _run_sample(x, mods, mods_kv, mods_f, s_in, past, weights, past_kv, means):
    (norm_g, w_in_a, w_out_a, w_q_b, w_o_b, kv_norm_g, w_kv, final_g) = weights
    b, t, d = x.shape
    rows = b * t
    pos = past + jnp.arange(t, dtype=jnp.int32)
    per_row = lambda v: jnp.repeat(v, t, axis=0)

    def project(x2, g, shift, scale, w, layer):
        h = _norm_mod(x2, g[None, :], shift, scale, 0, F32, rows)
        return _matmul(h, w, layer, rows, SAMPLE_COLS)

    x2 = x.reshape(rows, d)
    new_s = kv = None
    for l in range(DEPTH):
        shift, scale, gate = _split_mod(mods[l], 3, d, per_row)
        if l < N_A:
            qkvg = project(x2, norm_g[l], shift, scale, w_in_a, l).reshape(b, t, -1)
            a, new_s = _retention(qkvg, s_in, l, pos, F32, new_s)
            x2 = _matmul(a.reshape(rows, V_A), w_out_a, l, rows, SAMPLE_COLS, res=x2, gate=gate)
        else:
            lb = l - N_A
            qg = project(x2, norm_g[l], shift, scale, w_q_b, lb).reshape(b, t, -1)
            sel = _moba_select(qg, means, past)[:, :, :MOBA_TOPK].reshape(-1)
            a = _moba_sample(qg, kv.reshape(b, t, -1), past_kv, sel)
            x2 = _matmul(a.reshape(rows, d), w_o_b, lb, rows, SAMPLE_COLS, res=x2, gate=gate)
        if l == N_A - 1:
            kv_shift, kv_scale = _split_mod(mods_kv, 2, d, per_row)
            kv = project(x2, kv_norm_g, kv_shift, kv_scale, w_kv[None], 0)
    f_shift, f_scale = _split_mod(mods_f, 2, d, per_row)
    y = _norm_mod(x2, final_g[None, :], f_shift, f_scale, 0, F32, rows)
    k_new = kv[:, :d].reshape(b, t, H_B, DH_B)
    v_new = kv[:, d:].reshape(b, t, H_B, DH_B)
    return y.reshape(b, t, d), new_s, k_new, v_new


def kernel(x_prompt, x_sample, state_ret, cache_k, cache_v, page_table, c_prompt, c_sample,
           norm_g, w_mod, b_mod, w_in_a, w_out_a, w_q_b, w_o_b,
           kv_norm_g, w_mod_kv, b_mod_kv, w_kv, final_g, w_mod_f, b_mod_f):
    bp = x_prompt.shape[0]
    bd, n_pages = page_table.shape
    n_pool, page, _, _ = cache_k.shape
    past_len = n_pages * page
    assert bp + bd <= MOD_ROWS

    c_all = jnp.concatenate([c_prompt, c_sample, jnp.zeros((MOD_ROWS - bp - bd, D_MODEL), F32)], axis=0)
    mods = _mod_matmul(c_all, w_mod, b_mod[:, None, :])
    mods_kv = _mod_matmul(c_all, w_mod_kv[None], b_mod_kv[None, None, :])[0]
    mods_f = _mod_matmul(c_all, w_mod_f[None], b_mod_f[None, None, :])[0]

    weights = (norm_g, w_in_a, _cast_bf16(w_out_a), w_q_b, _cast_bf16(w_o_b), kv_norm_g, w_kv, final_g)

    y_p, s_p, k_p, v_p = _run_prompt(x_prompt, mods[:, :bp], mods_kv[:bp], mods_f[:bp], weights)

    pt_flat = page_table.reshape(-1)
    past_kv, means = _gather_cache(cache_k, cache_v, pt_flat, bd, n_pages)
    means = means.reshape(bd, -1, H_B * DH_B)
    y_s, s_s, k_s, v_s = _run_sample(
        x_sample, mods[:, bp:bp + bd], mods_kv[bp:bp + bd], mods_f[bp:bp + bd], state_ret, past_len,
        weights, past_kv, means)
    return (y_p, y_s, s_p, s_s, k_p, v_p, k_s, v_s)
```

```python
import functools
import math

import jax
import jax.numpy as jnp
from jax import lax
from jax.experimental import pallas as pl
from jax.experimental.pallas import tpu as pltpu

F32 = jnp.float32
BF16 = jnp.bfloat16

D_MODEL = 2048
DEPTH = 4
N_A = DEPTH // 2
H_A = 8
DK_A = D_MODEL // H_A
DV_A = 2 * DK_A
QK_A = H_A * DK_A
V_A = H_A * DV_A
RET_CHUNK = 256
H_B = 16
DH_B = D_MODEL // H_B
MOBA_BLOCK = 256
MOBA_TOPK = 3
ROPE_BASE = 10000.0
EPS = 1e-6
LOG2_E = math.log2(math.e)

VMEM_LIMIT_BYTES = 56 * 1024 * 1024
MOD_ROWS = 16


def _params(*sem):
    return pltpu.CompilerParams(dimension_semantics=sem, vmem_limit_bytes=VMEM_LIMIT_BYTES)


def _mod_kernel(c_ref, w_ref, b_ref, o_ref):
    acc = jnp.dot(c_ref[...].astype(BF16), w_ref[...].astype(BF16), preferred_element_type=F32)
    o_ref[...] = acc + b_ref[...]


def _mod_matmul(c, w, b, tn=1024):
    n_l, d, n = w.shape
    return pl.pallas_call(
        _mod_kernel,
        grid=(n_l, n // tn),
        in_specs=[
            pl.BlockSpec((MOD_ROWS, d), lambda l, j: (0, 0)),
            pl.BlockSpec((None, d, tn), lambda l, j: (l, 0, j)),
            pl.BlockSpec((None, 1, tn), lambda l, j: (l, 0, j)),
        ],
        out_specs=pl.BlockSpec((None, MOD_ROWS, tn), lambda l, j: (l, 0, j)),
        out_shape=jax.ShapeDtypeStruct((n_l, MOD_ROWS, n), F32),
        compiler_params=_params("parallel", "parallel"),
        name="mod_matmul",
    )(c, w, b)


def _norm_mod_kernel(x_ref, g_ref, sh_ref, sc_ref, o_ref):
    x = x_ref[...]
    r = lax.rsqrt(jnp.mean(x * x, axis=-1, keepdims=True) + EPS)
    y = (x * r) * g_ref[...]
    o_ref[...] = (y * (1.0 + sc_ref[...]) + sh_ref[...]).astype(o_ref.dtype)


def _norm_mod(x, g, shift, scale, rows_per_batch, out_dtype, tm):
    r, d = x.shape
    if rows_per_batch:
        tiles_per_b = rows_per_batch // tm
        mod_spec = pl.BlockSpec((None, 1, d), lambda i: (i // tiles_per_b, 0, 0))
    else:
        mod_spec = pl.BlockSpec((tm, d), lambda i: (i, 0))
    return pl.pallas_call(
        _norm_mod_kernel,
        grid=(r // tm,),
        in_specs=[
            pl.BlockSpec((tm, d), lambda i: (i, 0)),
            pl.BlockSpec((1, d), lambda i: (0, 0)),
            mod_spec,
            mod_spec,
        ],
        out_specs=pl.BlockSpec((tm, d), lambda i: (i, 0)),
        out_shape=jax.ShapeDtypeStruct((r, d), out_dtype),
        compiler_params=_params("parallel"),
        name="norm_mod",
    )(x, g, shift, scale)


def _matmul_kernel(a_ref, w_ref, o_ref):
    o_ref[...] = jnp.dot(
        a_ref[...].astype(BF16), w_ref[...].astype(BF16), preferred_element_type=F32
    ).astype(o_ref.dtype)


def _matmul_res_kernel(a_ref, w_ref, x_ref, gm_ref, o_ref):
    acc = jnp.dot(a_ref[...].astype(BF16), w_ref[...].astype(BF16), preferred_element_type=F32)
    o_ref[...] = x_ref[...] + gm_ref[...] * acc


MATMUL_W_TILE_ELEMS = 2 * 1024 * 1024
MATMUL_A_TILE_ELEMS = 4 * 1024 * 1024


def _matmul(a, w, layer, tm, tn, res=None, gate=None, rows_per_batch=0):
    r, k = a.shape
    n = w.shape[2]
    tm = min(tm, MATMUL_A_TILE_ELEMS // k)
    tn = min(tn, MATMUL_W_TILE_ELEMS // k, MATMUL_W_TILE_ELEMS * 1024 // (tm * k))
    in_specs = [
        pl.BlockSpec((tm, k), lambda i, j: (i, 0)),
        pl.BlockSpec((None, k, tn), lambda i, j: (layer, 0, j)),
    ]
    args = [a, w]
    kernel = _matmul_kernel
    if res is not None:
        kernel = _matmul_res_kernel
        in_specs.append(pl.BlockSpec((tm, tn), lambda i, j: (i, j)))
        if rows_per_batch:
            tiles_per_b = rows_per_batch // tm
            in_specs.append(pl.BlockSpec((None, 1, tn), lambda i, j: (i // tiles_per_b, 0, j)))
        else:
            in_specs.append(pl.BlockSpec((tm, tn), lambda i, j: (i, j)))
        args += [res, gate]
    return pl.pallas_call(
        kernel,
        grid=(r // tm, n // tn),
        in_specs=in_specs,
        out_specs=pl.BlockSpec((tm, tn), lambda i, j: (i, j)),
        out_shape=jax.ShapeDtypeStruct((r, n), F32),
        compiler_params=_params("parallel", "parallel"),
        name="matmul_res" if res is not None else "matmul",
    )(*args)


def _cast_kernel(x_ref, o_ref):
    o_ref[...] = x_ref[...].astype(o_ref.dtype)


def _cast_bf16(w, rows=512):
    n_l, k, n = w.shape
    spec = pl.BlockSpec((None, rows, n), lambda l, i: (l, i, 0))
    return pl.pallas_call(
        _cast_kernel,
        grid=(n_l, k // rows),
        in_specs=[spec],
        out_specs=spec,
        out_shape=jax.ShapeDtypeStruct(w.shape, BF16),
        compiler_params=_params("parallel", "parallel"),
        name="cast_bf16",
    )(w)


def _matmul_res_norm_kernel(a_ref, w_ref, x_ref, gm_ref, g_ref, sh_ref, sc_ref, *out_refs):
    acc = jnp.dot(a_ref[...].astype(BF16), w_ref[...], preferred_element_type=F32)
    x = x_ref[...] + gm_ref[...] * acc
    if len(out_refs) == 2:
        out_refs[0][...] = x
    r = lax.rsqrt(jnp.mean(x * x, axis=-1, keepdims=True) + EPS)
    y = (x * r) * g_ref[...]
    out_refs[-1][...] = (y * (1.0 + sc_ref[...]) + sh_ref[...]).astype(out_refs[-1].dtype)


def _matmul_res_norm(a, w_bf16, layer, res, gate, g, shift, scale, rows_per_batch, tm, norm_dtype, keep_x):
    r, k = a.shape
    n = w_bf16.shape[2]
    tiles_per_b = rows_per_batch // tm
    row_spec = pl.BlockSpec((tm, n), lambda i: (i, 0))
    mod_spec = pl.BlockSpec((None, 1, n), lambda i: (i // tiles_per_b, 0, 0))
    out_specs = [row_spec, row_spec] if keep_x else [row_spec]
    out_shape = [jax.ShapeDtypeStruct((r, n), F32)] if keep_x else []
    out_shape.append(jax.ShapeDtypeStruct((r, n), norm_dtype))
    outs = pl.pallas_call(
        _matmul_res_norm_kernel,
        grid=(r // tm,),
        in_specs=[
            pl.BlockSpec((tm, k), lambda i: (i, 0)),
            pl.BlockSpec((None, k, n), lambda i: (layer, 0, 0), pipeline_mode=pl.Buffered(1)),
            row_spec,
            mod_spec,
            pl.BlockSpec((1, n), lambda i: (0, 0)),
            mod_spec,
            mod_spec,
        ],
        out_specs=out_specs,
        out_shape=out_shape,
        compiler_params=_params("parallel"),
        name="matmul_res_norm",
    )(a, w_bf16, res, gate, g, shift, scale)
    return (outs[0], outs[1]) if keep_x else (None, outs[0])


def _kv_project_kernel(x_ref, g_ref, sh_ref, sc_ref, w_ref, rows_ref, hm_ref, *, tm, transpose):
    x = x_ref[...]
    r = lax.rsqrt(jnp.mean(x * x, axis=-1, keepdims=True) + EPS)
    y = (x * r) * g_ref[...]
    h = (y * (1.0 + sc_ref[...]) + sh_ref[...]).astype(BF16)
    acc = jnp.dot(h, w_ref[...], preferred_element_type=F32)
    for head in range(H_B):
        cols = acc[:, head * DH_B:(head + 1) * DH_B]
        rows_ref[pl.ds(head, tm, stride=H_B), :] = cols
        hm_ref[head] = (cols.T if transpose else cols).astype(BF16)


def _kv_project(x, g, shift, scale, w_kv_bf16, part, b, t, transpose, tm=512):
    r, d = x.shape
    width = H_B * DH_B
    tiles_per_b = t // tm
    mod_spec = pl.BlockSpec((None, 1, d), lambda i: (i // tiles_per_b, 0, 0))
    if transpose:
        hm_spec = pl.BlockSpec((None, H_B, DH_B, tm), lambda i: (i // tiles_per_b, 0, 0, i % tiles_per_b))
        hm_shape = (b, H_B, DH_B, t)
    else:
        hm_spec = pl.BlockSpec((None, H_B, tm, DH_B), lambda i: (i // tiles_per_b, 0, i % tiles_per_b, 0))
        hm_shape = (b, H_B, t, DH_B)
    rows_out, hm = pl.pallas_call(
        functools.partial(_kv_project_kernel, tm=tm, transpose=transpose),
        grid=(r // tm,),
        in_specs=[
            pl.BlockSpec((tm, d), lambda i: (i, 0)),
            pl.BlockSpec((1, d), lambda i: (0, 0)),
            mod_spec,
            mod_spec,
            pl.BlockSpec((None, d, width), lambda i: (0, 0, part), pipeline_mode=pl.Buffered(1)),
        ],
        out_specs=[pl.BlockSpec((tm * H_B, DH_B), lambda i: (i, 0)), hm_spec],
        out_shape=[
            jax.ShapeDtypeStruct((r * H_B, DH_B), F32),
            jax.ShapeDtypeStruct(hm_shape, BF16),
        ],
        compiler_params=_params("parallel"),
        name="kv_project",
    )(x, g, shift, scale, w_kv_bf16)
    return rows_out.reshape(b, t, H_B, DH_B), hm


def _rope_tables(pos):
    half = DK_A // 2
    inv = 1.0 / (ROPE_BASE ** jnp.linspace(0.0, 1.0, half, dtype=F32))
    ang = pos.astype(F32)[:, None] * inv[None, :]
    return jnp.cos(ang), jnp.sin(ang)


def _retention_tables(chunk):
    log_g = jnp.log1p(-(2.0 ** (-5.0 - jnp.arange(H_A, dtype=F32))))
    i = jnp.arange(chunk, dtype=F32)
    diff = i[:, None] - i[None, :]
    decay = jnp.where(diff >= 0, jnp.exp(jnp.maximum(diff, 0.0)[None] * log_g[:, None, None]), 0.0)
    cross_scale = jnp.exp((i + 1.0)[:, None] * log_g[None, :]).T[:, :, None]
    wk = jnp.exp((chunk - 1.0 - i)[:, None] * log_g[None, :]).T[:, :, None]
    state_decay = jnp.exp(chunk * log_g)
    return decay, cross_scale, wk, state_decay


def _retention_chunk(q, k, v, g, cos, sin, decay, cross_scale, wk, state_decay, s):
    half = DK_A // 2

    def rot(x):
        x1 = x[:, :half]
        x2 = x[:, half:]
        return jnp.concatenate([x1 * cos - x2 * sin, x1 * sin + x2 * cos], axis=-1)

    q = rot(q)
    k = rot(k) * (DK_A ** -0.5)
    vb = v.astype(BF16)
    qb = q.astype(BF16)
    kb = k.astype(BF16)
    scores = lax.dot_general(qb, kb, (((1,), (1,)), ((), ())), preferred_element_type=F32) * decay
    inner = jnp.dot(scores.astype(BF16), vb, preferred_element_type=F32)
    cross = jnp.dot(qb, s.astype(BF16), preferred_element_type=F32) * cross_scale
    kw = (k * wk).astype(BF16)
    upd = lax.dot_general(kw, vb, (((0,), (0,)), ((), ())), preferred_element_type=F32)
    s_new = state_decay * s + upd
    o = inner + cross
    o = o * lax.rsqrt(jnp.mean(o * o, axis=-1, keepdims=True) + EPS)
    return g * jax.nn.sigmoid(g) * o, s_new


RET_HEADS_PER_STEP = 4
RET_CHUNKS_PER_STEP = 1


def _retention_kernel(sdec_ref, q_ref, k_ref, v_ref, g_ref, cos_ref, sin_ref, dec_ref, cs_ref,
                      wk_ref, *refs, chunk, heads, chunks, n_steps, zero_state, layer, init_stack):
    a_ref, sout_ref, s_scr = refs[-3:]
    hg = pl.program_id(1)
    c = pl.program_id(2)

    @pl.when(c == 0)
    def _():
        s_scr[...] = jnp.zeros(s_scr.shape, F32) if zero_state else refs[0][...]

    for hh in range(heads):
        s = s_scr[hh]
        kcols = slice(hh * DK_A, (hh + 1) * DK_A)
        vcols = slice(hh * DV_A, (hh + 1) * DV_A)
        for ci in range(chunks):
            rows = slice(ci * chunk, (ci + 1) * chunk)
            a, s = _retention_chunk(
                q_ref[rows, kcols], k_ref[rows, kcols], v_ref[rows, vcols], g_ref[rows, vcols],
                cos_ref[rows, :], sin_ref[rows, :], dec_ref[hh], cs_ref[hh], wk_ref[hh],
                sdec_ref[hg * heads + hh], s)
            a_ref[rows, vcols] = a.astype(a_ref.dtype)
        s_scr[hh] = s

        @pl.when(c == n_steps - 1)
        def _():
            _store_state(sout_ref, hh, s, layer, init_stack)


def _store_state(sout_ref, idx, s, layer, init_stack):
    if not init_stack:
        sout_ref[idx] = s
        return
    for slab in range(sout_ref.shape[0]):
        sout_ref[slab, idx] = s if slab == layer else jnp.zeros(s.shape, s.dtype)


def _retention(qkvg, s_in, layer, pos, out_dtype, s_out_prev):
    b, t, _ = qkvg.shape
    chunk = math.gcd(t, RET_CHUNK)
    n_chunks = t // chunk
    heads = RET_HEADS_PER_STEP
    chunks = math.gcd(n_chunks, RET_CHUNKS_PER_STEP)
    n_steps = n_chunks // chunks
    rows = chunk * chunks
    assert H_A % heads == 0
    cos, sin = _rope_tables(pos)
    decay, cross_scale, wk, state_decay = _retention_tables(chunk)
    k_off = QK_A // (heads * DK_A)
    v_off = 2 * QK_A // (heads * DV_A)
    g_off = (2 * QK_A + V_A) // (heads * DV_A)
    init_stack = s_out_prev is None
    kernel = functools.partial(_retention_kernel, chunk=chunk, heads=heads, chunks=chunks, n_steps=n_steps,
                               zero_state=s_in is None, layer=layer, init_stack=init_stack)
    state_spec = pl.BlockSpec((None, None, heads, DK_A, DV_A), lambda bi, h, c: (layer, bi, h, 0, 0))
    state_args, state_specs, aliases = [], [], {}
    if s_in is not None:
        state_args.append(s_in)
        state_specs.append(state_spec)
    if init_stack:
        out_state_spec = pl.BlockSpec((N_A, None, heads, DK_A, DV_A), lambda bi, h, c: (0, bi, h, 0, 0))
    else:
        out_state_spec = state_spec
        aliases = {10 + len(state_args): 1}
        state_args.append(s_out_prev)
        state_specs.append(pl.BlockSpec(memory_space=pl.ANY))
    return pl.pallas_call(
        kernel,
        grid=(b, H_A // heads, n_steps),
        in_specs=[
            pl.BlockSpec(memory_space=pltpu.SMEM),
            pl.BlockSpec((None, rows, heads * DK_A), lambda bi, h, c: (bi, c, h)),
            pl.BlockSpec((None, rows, heads * DK_A), lambda bi, h, c: (bi, c, k_off + h)),
            pl.BlockSpec((None, rows, heads * DV_A), lambda bi, h, c: (bi, c, v_off + h)),
            pl.BlockSpec((None, rows, heads * DV_A), lambda bi, h, c: (bi, c, g_off + h)),
            pl.BlockSpec((rows, DK_A // 2), lambda bi, h, c: (c, 0)),
            pl.BlockSpec((rows, DK_A // 2), lambda bi, h, c: (c, 0)),
            pl.BlockSpec((heads, chunk, chunk), lambda bi, h, c: (h, 0, 0)),
            pl.BlockSpec((heads, chunk, 1), lambda bi, h, c: (h, 0, 0)),
            pl.BlockSpec((heads, chunk, 1), lambda bi, h, c: (h, 0, 0)),
        ] + state_specs,
        out_specs=[
            pl.BlockSpec((None, rows, heads * DV_A), lambda bi, h, c: (bi, c, h)),
            out_state_spec,
        ],
        out_shape=[
            jax.ShapeDtypeStruct((b, t, V_A), out_dtype),
            jax.ShapeDtypeStruct((N_A, b, H_A, DK_A, DV_A), F32),
        ],
        scratch_shapes=[pltpu.VMEM((heads, DK_A, DV_A), F32)],
        input_output_aliases=aliases,
        compiler_params=_params("parallel", "parallel", "arbitrary"),
        name="retention",
    )(state_decay, qkvg, qkvg, qkvg, qkvg, cos, sin, decay, cross_scale, wk, *state_args)


PROJ_RET_ROWS = 1024


def _proj_retention_kernel(sdec_ref, h_ref, wq_ref, wkey_ref, wv_ref, wg_ref, cos_ref, sin_ref, dec_ref,
                           cs_ref, kdecay_ref, *refs, chunk, n_steps, layer, init_stack):
    a_ref, sout_ref, qkvg_scr, s_scr = refs[-4:]
    head = pl.program_id(1)
    r = pl.program_id(2)

    @pl.when(r == 0)
    def _():
        s_scr[...] = jnp.zeros(s_scr.shape, F32)

    hb = h_ref[...]
    col = 0
    for w_ref in (wq_ref, wkey_ref, wv_ref, wg_ref):
        width = w_ref.shape[1]
        qkvg_scr[:, col:col + width] = jnp.dot(hb, w_ref[...].astype(BF16), preferred_element_type=F32)
        col += width

    s = s_scr[...]
    for ci in range(h_ref.shape[0] // chunk):
        rows = slice(ci * chunk, (ci + 1) * chunk)
        a, s = _retention_chunk(
            qkvg_scr[rows, 0:DK_A], qkvg_scr[rows, DK_A:2 * DK_A],
            qkvg_scr[rows, 2 * DK_A:2 * DK_A + DV_A], qkvg_scr[rows, 2 * DK_A + DV_A:],
            cos_ref[rows, :], sin_ref[rows, :], dec_ref[...], cs_ref[...], kdecay_ref[...],
            sdec_ref[head], s)
        a_ref[rows, :] = a.astype(a_ref.dtype)
    s_scr[...] = s

    @pl.when(r == n_steps - 1)
    def _():
        _store_state(sout_ref, 0, s, layer, init_stack)


def _proj_retention(h_act, w_in, layer, b, t, pos, out_dtype, s_out_prev):
    chunk = math.gcd(t, RET_CHUNK)
    rows = min(PROJ_RET_ROWS, t)
    assert t % rows == 0 and rows % chunk == 0
    n_steps = t // rows
    cos, sin = _rope_tables(pos)
    decay, cross_scale, wk, state_decay = _retention_tables(chunk)
    k_off = QK_A // DK_A
    v_off = 2 * QK_A // DV_A
    g_off = (2 * QK_A + V_A) // DV_A
    d = h_act.shape[1]
    init_stack = s_out_prev is None
    kernel = functools.partial(_proj_retention_kernel, chunk=chunk, n_steps=n_steps, layer=layer,
                               init_stack=init_stack)
    extra_args, extra_specs, aliases = [], [], {}
    if init_stack:
        state_spec = pl.BlockSpec((N_A, None, 1, DK_A, DV_A), lambda bi, h, r: (0, bi, h, 0, 0))
    else:
        state_spec = pl.BlockSpec((None, None, 1, DK_A, DV_A), lambda bi, h, r: (layer, bi, h, 0, 0))
        aliases = {11: 1}
        extra_args.append(s_out_prev)
        extra_specs.append(pl.BlockSpec(memory_space=pl.ANY))
    return pl.pallas_call(
        kernel,
        grid=(b, H_A, n_steps),
        in_specs=[
            pl.BlockSpec(memory_space=pltpu.SMEM),
            pl.BlockSpec((rows, d), lambda bi, h, r: (bi * n_steps + r, 0)),
            pl.BlockSpec((None, d, DK_A), lambda bi, h, r: (layer, 0, h)),
            pl.BlockSpec((None, d, DK_A), lambda bi, h, r: (layer, 0, k_off + h)),
            pl.BlockSpec((None, d, DV_A), lambda bi, h, r: (layer, 0, v_off + h)),
            pl.BlockSpec((None, d, DV_A), lambda bi, h, r: (layer, 0, g_off + h)),
            pl.BlockSpec((rows, DK_A // 2), lambda bi, h, r: (r, 0)),
            pl.BlockSpec((rows, DK_A // 2), lambda bi, h, r: (r, 0)),
            pl.BlockSpec((None, chunk, chunk), lambda bi, h, r: (h, 0, 0)),
            pl.BlockSpec((None, chunk, 1), lambda bi, h, r: (h, 0, 0)),
            pl.BlockSpec((None, chunk, 1), lambda bi, h, r: (h, 0, 0)),
        ] + extra_specs,
        out_specs=[
            pl.BlockSpec((None, rows, DV_A), lambda bi, h, r: (bi, r, h)),
            state_spec,
        ],
        out_shape=[
            jax.ShapeDtypeStruct((b, t, V_A), out_dtype),
            jax.ShapeDtypeStruct((N_A, b, H_A, DK_A, DV_A), F32),
        ],
        scratch_shapes=[
            pltpu.VMEM((rows, 2 * DK_A + 2 * DV_A), F32),
            pltpu.VMEM((DK_A, DV_A), F32),
        ],
        input_output_aliases=aliases,
        compiler_params=_params("parallel", "parallel", "arbitrary"),
        name="proj_retention",
    )(state_decay, h_act, w_in, w_in, w_in, w_in, cos, sin, decay, cross_scale, wk, *extra_args)


def _select_topk(gate, n_sel, axis=1):
    pos = lax.broadcasted_iota(jnp.int32, gate.shape, axis).astype(F32)
    picks = []
    for _ in range(n_sel):
        m = jnp.max(gate, axis=axis, keepdims=True)
        idx = jnp.min(jnp.where(gate == m, pos, float(gate.shape[axis])), axis=axis, keepdims=True)
        valid = m > -jnp.inf
        picks.append((idx, valid))
        gate = jnp.where((pos == idx) & valid, -jnp.inf, gate)
    return picks


MOBA_HEADS_PER_STEP = 2
MOBA_PROJ_HEADS = 2


def _moba_prompt_kernel(h_ref, wq_ref, wg_ref, k_ref, vt_ref, a_ref, q_scr, g_scr, *, n_blocks, heads):
    hb = h_ref[...]
    for h0 in range(0, heads, MOBA_PROJ_HEADS):
        cols = slice(h0 * DH_B, (h0 + MOBA_PROJ_HEADS) * DH_B)
        q_scr[:, cols] = jnp.dot(hb, wq_ref[:, cols].astype(BF16), preferred_element_type=F32)
        g_scr[:, cols] = jnp.dot(hb, wg_ref[:, cols].astype(BF16), preferred_element_type=F32)
    for hh in range(heads):
        cols = slice(hh * DH_B, (hh + 1) * DH_B)
        _moba_prompt_head(q_scr, g_scr, k_ref.at[hh], vt_ref.at[hh], a_ref, cols, n_blocks)


def _moba_prompt_head(q_ref, g_ref, k_ref, vt_ref, a_ref, cols, n_blocks):
    bs = MOBA_BLOCK
    scale = DH_B ** -0.5
    n_sel = min(MOBA_TOPK, n_blocks)
    kb = k_ref[...]
    ones_rows = 16
    vt = jnp.concatenate([vt_ref[...], jnp.ones((ones_rows, kb.shape[0]), BF16)], axis=0)
    means_b = jnp.mean(kb.astype(F32).reshape(n_blocks, bs, DH_B), axis=1).astype(BF16)
    key = lax.broadcasted_iota(jnp.int32, (bs, bs), 0)
    qry = lax.broadcasted_iota(jnp.int32, (bs, bs), 1)
    causal = key <= qry

    for i in range(n_blocks):
        rows = slice(i * bs, (i + 1) * bs)
        q = q_ref[rows, cols]
        qs = (q * (scale * LOG2_E)).astype(BF16)
        picks = None
        if i > n_sel:
            gate = lax.dot_general(means_b, q.astype(BF16), (((1,), (1,)), ((), ())),
                                   preferred_element_type=F32)
            blk = lax.broadcasted_iota(jnp.int32, gate.shape, 0)
            picks = _select_topk(jnp.where(blk < i, gate, -jnp.inf), n_sel, axis=0)
        blocks = []
        for j in range(i + 1):
            sj = lax.dot_general(kb[j * bs:(j + 1) * bs, :], qs, (((1,), (1,)), ((), ())),
                                 preferred_element_type=F32)
            if j == i:
                sj = jnp.where(causal, sj, -jnp.inf)
            elif picks is not None:
                chosen = picks[0][0] == float(j)
                for idx, _ in picks[1:]:
                    chosen = chosen | (idx == float(j))
                sj = jnp.where(chosen, sj, -jnp.inf)
            blocks.append(sj)
        m = jnp.max(functools.reduce(jnp.maximum, blocks), axis=0, keepdims=True)
        acc = jnp.zeros((DH_B + ones_rows, bs), F32)
        for j, sj in enumerate(blocks):
            p = jnp.exp2(sj - m).astype(BF16)
            acc = acc + jnp.dot(vt[:, j * bs:(j + 1) * bs], p, preferred_element_type=F32)
        o = (acc[:DH_B, :] / acc[DH_B:DH_B + 1, :]).T
        g = g_ref[rows, cols]
        a_ref[rows, cols] = (g * jax.nn.sigmoid(g) * o).astype(a_ref.dtype)


def _moba_prompt(h_act, w_q, layer, b, t, k_hm, vt_hm, out_dtype):
    d = h_act.shape[1]
    n_blocks = t // MOBA_BLOCK
    heads = MOBA_HEADS_PER_STEP
    assert n_blocks * MOBA_BLOCK == t and n_blocks <= 128 and H_B % heads == 0
    g_off = H_B // heads
    kernel = functools.partial(_moba_prompt_kernel, n_blocks=n_blocks, heads=heads)
    return pl.pallas_call(
        kernel,
        grid=(b, H_B // heads),
        in_specs=[
            pl.BlockSpec((t, d), lambda bi, h: (bi, 0)),
            pl.BlockSpec((None, d, heads * DH_B), lambda bi, h: (layer, 0, h)),
            pl.BlockSpec((None, d, heads * DH_B), lambda bi, h: (layer, 0, g_off + h)),
            pl.BlockSpec((None, heads, t, DH_B), lambda bi, h: (bi, h, 0, 0)),
            pl.BlockSpec((None, heads, DH_B, t), lambda bi, h: (bi, h, 0, 0)),
        ],
        out_specs=pl.BlockSpec((None, t, heads * DH_B), lambda bi, h: (bi, 0, h)),
        out_shape=jax.ShapeDtypeStruct((b, t, D_MODEL), out_dtype),
        scratch_shapes=[pltpu.VMEM((t, heads * DH_B), F32), pltpu.VMEM((t, heads * DH_B), F32)],
        compiler_params=_params("parallel", "arbitrary"),
        name="moba_prompt",
    )(h_act, w_q, w_q, k_hm, vt_hm)


HEAD_GROUP = 8


GATHER_PAGES_PER_STEP = 4


def _gather_cache_kernel(pt_ref, *refs, page, n_groups, pages_per_step, pages_per_block):
    n_slabs = pages_per_step * n_groups
    o_ref, m_ref = refs[2 * n_slabs], refs[2 * n_slabs + 1]
    for part in range(2):
        lanes = slice(part * DH_B, (part + 1) * DH_B)
        for pg in range(pages_per_step):
            for grp in range(n_groups):
                src = refs[part * n_slabs + pg * n_groups + grp]
                src = src.reshape(page * HEAD_GROUP, DH_B)
                for hg in range(HEAD_GROUP):
                    rows = src[pl.ds(hg, page, stride=HEAD_GROUP), :]
                    o_ref[grp * HEAD_GROUP + hg, pg * page:(pg + 1) * page, lanes] = rows.astype(BF16)
    for blk in range(pages_per_step // pages_per_block):
        for grp in range(n_groups):
            total = None
            for pg in range(blk * pages_per_block, (blk + 1) * pages_per_block):
                part_sum = jnp.sum(refs[pg * n_groups + grp][...], axis=0)
                total = part_sum if total is None else total + part_sum
            m_ref[blk, grp * HEAD_GROUP:(grp + 1) * HEAD_GROUP, :] = total / MOBA_BLOCK


def _gather_cache(cache_k, cache_v, page_table_flat, n_seq, n_pages):
    n_pool, page, n_h, dh = cache_k.shape
    pages_per_block = MOBA_BLOCK // page
    pps = GATHER_PAGES_PER_STEP
    assert n_h == H_B and dh == DH_B and n_h % HEAD_GROUP == 0
    assert pps % pages_per_block == 0 and n_pages % pps == 0
    n_blocks = n_pages // pages_per_block
    n_groups = n_h // HEAD_GROUP
    grouped = [c.reshape(n_pool, page, n_groups, HEAD_GROUP, dh) for c in (cache_k, cache_v)]

    def page_map(pg, grp):
        return lambda bi, j, pt: (pt[bi * n_pages + pps * j + pg], 0, grp, 0, 0)

    def slab_specs():
        return [pl.BlockSpec((None, page, None, HEAD_GROUP, dh), page_map(pg, grp))
                for pg in range(pps) for grp in range(n_groups)]

    n_slabs = pps * n_groups
    blocks_per_step = pps // pages_per_block
    kernel = functools.partial(_gather_cache_kernel, page=page, n_groups=n_groups, pages_per_step=pps,
                               pages_per_block=pages_per_block)
    return pl.pallas_call(
        kernel,
        grid_spec=pltpu.PrefetchScalarGridSpec(
            num_scalar_prefetch=1,
            grid=(n_seq, n_pages // pps),
            in_specs=slab_specs() + slab_specs(),
            out_specs=[
                pl.BlockSpec((None, n_h, pps * page, 2 * dh), lambda bi, j, pt: (bi, 0, j, 0)),
                pl.BlockSpec((None, blocks_per_step, n_h, dh), lambda bi, j, pt: (bi, j, 0, 0)),
            ],
        ),
        out_shape=[
            jax.ShapeDtypeStruct((n_seq, n_h, n_pages * page, 2 * dh), BF16),
            jax.ShapeDtypeStruct((n_seq, n_blocks, n_h, dh), F32),
        ],
        compiler_params=_params("parallel", "arbitrary"),
        name="gather_cache",
    )(page_table_flat, *([grouped[0]] * n_slabs), *([grouped[1]] * n_slabs))


def _moba_select_kernel(q_ref, means_ref, o_ref, *, t, n_blocks, past):
    q = q_ref[...]
    rows = t * H_B
    qrep = jnp.concatenate([jnp.broadcast_to(q[ti:ti + 1, :], (H_B, D_MODEL)) for ti in range(t)], axis=0)
    lane_h = lax.broadcasted_iota(jnp.int32, (rows, D_MODEL), 1) >> (DH_B.bit_length() - 1)
    row_h = lax.broadcasted_iota(jnp.int32, (rows, D_MODEL), 0) & (H_B - 1)
    qexp = jnp.where(lane_h == row_h, qrep, 0.0)
    means = jnp.concatenate([means_ref[...], jnp.zeros((128 - n_blocks, D_MODEL), F32)], axis=0)
    gate = lax.dot_general(qexp.astype(BF16), means.astype(BF16), (((1,), (1,)), ((), ())),
                           preferred_element_type=F32)
    lane = lax.broadcasted_iota(jnp.int32, gate.shape, 1)
    token = lax.broadcasted_iota(jnp.int32, gate.shape, 0) >> (H_B.bit_length() - 1)
    own = (past + token) >> (MOBA_BLOCK.bit_length() - 1)
    gate = jnp.where(lane < own, gate, -jnp.inf)
    out = jnp.zeros(gate.shape, F32)
    for r, (idx, _) in enumerate(_select_topk(gate, MOBA_TOPK)):
        out = jnp.where(lane == r, idx, out)
    o_ref[...] = out.astype(jnp.int32)


def _moba_select(qg, means, past):
    b, t, _ = qg.shape
    n_blocks = means.shape[1]
    assert MOBA_TOPK <= n_blocks <= 128
    kernel = functools.partial(_moba_select_kernel, t=t, n_blocks=n_blocks, past=past)
    return pl.pallas_call(
        kernel,
        grid=(b,),
        in_specs=[
            pl.BlockSpec((None, t, D_MODEL), lambda bi: (bi, 0, 0)),
            pl.BlockSpec((None, n_blocks, D_MODEL), lambda bi: (bi, 0, 0)),
        ],
        out_specs=pl.BlockSpec((None, t * H_B, 128), lambda bi: (bi, 0, 0)),
        out_shape=jax.ShapeDtypeStruct((b, t * H_B, 128), jnp.int32),
        compiler_params=_params("parallel"),
        name="moba_select",
    )(qg, means)


SAMPLE_HEADS_PER_STEP = 4


def _moba_sample_kernel(sel_ref, new_ref, *refs, t, n_slots, heads):
    a_ref = refs[heads * n_slots]
    for hh in range(heads):
        cols = slice(hh * DH_B, (hh + 1) * DH_B)
        _moba_sample_head(new_ref.at[hh], refs[hh * n_slots:(hh + 1) * n_slots], a_ref, cols, t, n_slots)


def _moba_sample_head(new_ref, kv_refs, a_ref, cols, t, n_slots):
    keys_per_token = (n_slots // t) * kv_refs[0].shape[0]

    q = new_ref[:, 0:DH_B]
    qb = q.astype(BF16)
    kall = jnp.concatenate([r[:, :DH_B] for r in kv_refs], axis=0)
    vall = jnp.concatenate([r[:, DH_B:] for r in kv_refs], axis=0)
    s = lax.dot_general(qb, kall, (((1,), (1,)), ((), ())), preferred_element_type=F32) * (DH_B ** -0.5)
    row = lax.broadcasted_iota(jnp.int32, s.shape, 0)
    col = lax.broadcasted_iota(jnp.int32, s.shape, 1)
    mine = (col >= row * keys_per_token) & (col < (row + 1) * keys_per_token)
    s = jnp.where(mine, s, -jnp.inf)

    kn = new_ref[:, 2 * DH_B:3 * DH_B]
    vn = new_ref[:, 3 * DH_B:4 * DH_B]
    rown = lax.broadcasted_iota(jnp.int32, (t, 1), 0)
    s_own = []
    for tj in range(t):
        sj = jnp.sum(q * kn[tj:tj + 1, :], axis=-1, keepdims=True) * (DH_B ** -0.5)
        s_own.append(jnp.where(rown >= tj, sj, -jnp.inf))
    m = jnp.max(s, axis=-1, keepdims=True)
    for sj in s_own:
        m = jnp.maximum(m, sj)
    p = jnp.exp(s - m)
    l = jnp.sum(p, axis=-1, keepdims=True)
    acc = jnp.dot(p.astype(BF16), vall, preferred_element_type=F32)
    for tj, sj in enumerate(s_own):
        pj = jnp.exp(sj - m)
        l = l + pj
        acc = acc + pj * vn[tj:tj + 1, :]
    o = acc / l
    g = new_ref[:, DH_B:2 * DH_B]
    a_ref[:, cols] = (g * jax.nn.sigmoid(g) * o).astype(a_ref.dtype)


def _moba_sample(qg, kv_new, past_kv, sel_flat):
    b, t, _ = qg.shape
    n_slots = t * MOBA_TOPK
    heads = SAMPLE_HEADS_PER_STEP
    assert H_B % heads == 0
    per_head = [a.reshape(b, t, H_B, DH_B) for a in
                (qg[:, :, :D_MODEL], qg[:, :, D_MODEL:], kv_new[:, :, :D_MODEL], kv_new[:, :, D_MODEL:])]
    new_rows = jnp.concatenate(per_head, axis=-1).transpose(0, 2, 1, 3)

    def slot_map(hh, ti, r):
        def index_map(bi, hg, sel):
            h = hg * heads + hh
            return (bi, h, sel[((bi * t + ti) * H_B + h) * MOBA_TOPK + r], 0)
        return index_map

    slot_specs = [pl.BlockSpec((None, None, MOBA_BLOCK, 2 * DH_B), slot_map(hh, ti, r))
                  for hh in range(heads) for ti in range(t) for r in range(MOBA_TOPK)]

    kernel = functools.partial(_moba_sample_kernel, t=t, n_slots=n_slots, heads=heads)
    return pl.pallas_call(
        kernel,
        grid_spec=pltpu.PrefetchScalarGridSpec(
            num_scalar_prefetch=1,
            grid=(b, H_B // heads),
            in_specs=[pl.BlockSpec((None, heads, t, 4 * DH_B), lambda bi, hg, sel: (bi, hg, 0, 0))] + slot_specs,
            out_specs=pl.BlockSpec((None, t, heads * DH_B), lambda bi, hg, sel: (bi, 0, hg)),
        ),
        out_shape=jax.ShapeDtypeStruct((b, t, D_MODEL), F32),
        compiler_params=_params("parallel", "arbitrary"),
        name="moba_sample",
    )(sel_flat, new_rows, *([past_kv] * len(slot_specs)))


PROMPT_NORM_ROWS = 512
PROMPT_RES_ROWS = 512
SAMPLE_COLS = 1024


def _split_mod(v, n_parts, d, expand):
    return tuple(expand(v[:, i * d:(i + 1) * d]) for i in range(n_parts))


def _run_prompt(x, mods, mods_kv, mods_f, weights):
    (norm_g, w_in_a, w_out_a, w_q_b, w_o_b, kv_norm_g, w_kv, final_g) = weights
    b, t, d = x.shape
    rows = b * t
    pos = jnp.arange(t, dtype=jnp.int32)
    per_batch = lambda v: v[:, None, :]
    x2 = x.reshape(rows, d)
    shift, scale, _ = _split_mod(mods[0], 3, d, per_batch)
    h = _norm_mod(x2, norm_g[0][None, :], shift, scale, t, BF16, PROMPT_NORM_ROWS)
    new_s = k_new = v_new = k_hm = vt_hm = None
    for l in range(DEPTH):
        gate = _split_mod(mods[l], 3, d, per_batch)[2]
        if l < N_A:
            a, new_s = _proj_retention(h, w_in_a, l, b, t, pos, BF16, new_s)
            a, w_res, w_layer = a.reshape(rows, V_A), w_out_a, l
        else:
            a = _moba_prompt(h, w_q_b, l - N_A, b, t, k_hm, vt_hm, BF16)
            a, w_res, w_layer = a.reshape(rows, d), w_o_b, l - N_A
        last = l == DEPTH - 1
        if last:
            g_next = final_g
            sh_next, sc_next = _split_mod(mods_f, 2, d, per_batch)
        else:
            g_next = norm_g[l + 1]
            sh_next, sc_next, _ = _split_mod(mods[l + 1], 3, d, per_batch)
        x2, h = _matmul_res_norm(a, w_res, w_layer, x2, gate, g_next[None, :], sh_next, sc_next, t,
                                 PROMPT_RES_ROWS, F32 if last else BF16, keep_x=not last)
        if l == N_A - 1:
            kv_shift, kv_scale = _split_mod(mods_kv, 2, d, per_batch)
            k_new, k_hm = _kv_project(x2, kv_norm_g[None, :], kv_shift, kv_scale, w_kv, 0, b, t, False)
            v_new, vt_hm = _kv_project(x2, kv_norm_g[None, :], kv_shift, kv_scale, w_kv, 1, b, t, True)
    return h.reshape(b, t, d), new_s, k_new, v_new


def _run_sample(x, mods, mods_kv, mods_f, s_in, past, weights, past_kv, means):
    (norm_g, w_in_a, w_out_a, w_q_b, w_o_b, kv_norm_g, w_kv, final_g) = weights
    b, t, d = x.shape
    rows = b * t
    pos = past + jnp.arange(t, dtype=jnp.int32)
    per_row = lambda v: jnp.repeat(v, t, axis=0)

    def project(x2, g, shift, scale, w, layer):
        h = _norm_mod(x2, g[None, :], shift, scale, 0, F32, rows)
        return _matmul(h, w, layer, rows, SAMPLE_COLS)

    x2 = x.reshape(rows, d)
    new_s = kv = None
    for l in range(DEPTH):
        shift, scale, gate = _split_mod(mods[l], 3, d, per_row)
        if l < N_A:
            qkvg = project(x2, norm_g[l], shift, scale, w_in_a, l).reshape(b, t, -1)
            a, new_s = _retention(qkvg, s_in, l, pos, F32, new_s)
            x2 = _matmul(a.reshape(rows, V_A), w_out_a, l, rows, SAMPLE_COLS, res=x2, gate=gate)
        else:
            lb = l - N_A
            qg = project(x2, norm_g[l], shift, scale, w_q_b, lb).reshape(b, t, -1)
            sel = _moba_select(qg, means, past)[:, :, :MOBA_TOPK].reshape(-1)
            a = _moba_sample(qg, kv.reshape(b, t, -1), past_kv, sel)
            x2 = _matmul(a.reshape(rows, d), w_o_b, lb, rows, SAMPLE_COLS, res=x2, gate=gate)
        if l == N_A - 1:
            kv_shift, kv_scale = _split_mod(mods_kv, 2, d, per_row)
            kv = project(x2, kv_norm_g, kv_shift, kv_scale, w_kv, 0)
    f_shift, f_scale = _split_mod(mods_f, 2, d, per_row)
    y = _norm_mod(x2, final_g[None, :], f_shift, f_scale, 0, F32, rows)
    k_new = kv[:, :d].reshape(b, t, H_B, DH_B)
    v_new = kv[:, d:].reshape(b, t, H_B, DH_B)
    return y.reshape(b, t, d), new_s, k_new, v_new


def kernel(x_prompt, x_sample, state_ret, cache_k, cache_v, page_table, c_prompt, c_sample,
           norm_g, w_mod, b_mod, w_in_a, w_out_a, w_q_b, w_o_b,
           kv_norm_g, w_mod_kv, b_mod_kv, w_kv, final_g, w_mod_f, b_mod_f):
    bp = x_prompt.shape[0]
    bd, n_pages = page_table.shape
    n_pool, page, _, _ = cache_k.shape
    past_len = n_pages * page
    assert bp + bd <= MOD_ROWS

    c_all = jnp.concatenate([c_prompt, c_sample, jnp.zeros((MOD_ROWS - bp - bd, D_MODEL), F32)], axis=0)
    mods = _mod_matmul(c_all, w_mod, b_mod[:, None, :])
    mods_kv = _mod_matmul(c_all, w_mod_kv[None], b_mod_kv[None, None, :])[0]
    mods_f = _mod_matmul(c_all, w_mod_f[None], b_mod_f[None, None, :])[0]

    weights = (norm_g, w_in_a, _cast_bf16(w_out_a), w_q_b, _cast_bf16(w_o_b), kv_norm_g,
               _cast_bf16(w_kv[None]), final_g)

    y_p, s_p, k_p, v_p = _run_prompt(x_prompt, mods[:, :bp], mods_kv[:bp], mods_f[:bp], weights)

    pt_flat = page_table.reshape(-1)
    past_kv, means = _gather_cache(cache_k, cache_v, pt_flat, bd, n_pages)
    means = means.reshape(bd, -1, H_B * DH_B)
    y_s, s_s, k_s, v_s = _run_sample(
        x_sample, mods[:, bp:bp + bd], mods_kv[bp:bp + bd], mods_f[bp:bp + bd], state_ret, past_len,
        weights, past_kv, means)
    return (y_p, y_s, s_p, s_s, k_p, v_p, k_s, v_s)
```

```python
import functools
import math

import jax
import jax.numpy as jnp
from jax import lax
from jax.experimental import pallas as pl
from jax.experimental.pallas import tpu as pltpu

F32 = jnp.float32
BF16 = jnp.bfloat16

D_MODEL = 2048
DEPTH = 4
N_A = DEPTH // 2
H_A = 8
DK_A = D_MODEL // H_A
DV_A = 2 * DK_A
QK_A = H_A * DK_A
V_A = H_A * DV_A
RET_CHUNK = 256
H_B = 16
DH_B = D_MODEL // H_B
MOBA_BLOCK = 256
MOBA_TOPK = 3
ROPE_BASE = 10000.0
EPS = 1e-6
LOG2_E = math.log2(math.e)

VMEM_LIMIT_BYTES = 56 * 1024 * 1024
MOD_ROWS = 16


def _params(*sem):
    return pltpu.CompilerParams(dimension_semantics=sem, vmem_limit_bytes=VMEM_LIMIT_BYTES)


def _mod_kernel(c_ref, w_ref, b_ref, o_ref):
    acc = jnp.dot(c_ref[...].astype(BF16), w_ref[...].astype(BF16), preferred_element_type=F32)
    o_ref[...] = acc + b_ref[...]


def _mod_matmul(c, w, b, tn=1024):
    n_l, d, n = w.shape
    return pl.pallas_call(
        _mod_kernel,
        grid=(n_l, n // tn),
        in_specs=[
            pl.BlockSpec((MOD_ROWS, d), lambda l, j: (0, 0)),
            pl.BlockSpec((None, d, tn), lambda l, j: (l, 0, j)),
            pl.BlockSpec((None, 1, tn), lambda l, j: (l, 0, j)),
        ],
        out_specs=pl.BlockSpec((None, MOD_ROWS, tn), lambda l, j: (l, 0, j)),
        out_shape=jax.ShapeDtypeStruct((n_l, MOD_ROWS, n), F32),
        compiler_params=_params("parallel", "parallel"),
        name="mod_matmul",
    )(c, w, b)


def _norm_mod_kernel(x_ref, g_ref, sh_ref, sc_ref, o_ref):
    x = x_ref[...]
    r = lax.rsqrt(jnp.mean(x * x, axis=-1, keepdims=True) + EPS)
    y = (x * r) * g_ref[...]
    o_ref[...] = (y * (1.0 + sc_ref[...]) + sh_ref[...]).astype(o_ref.dtype)


def _norm_mod(x, g, shift, scale, rows_per_batch, out_dtype, tm):
    r, d = x.shape
    if rows_per_batch:
        tiles_per_b = rows_per_batch // tm
        mod_spec = pl.BlockSpec((None, 1, d), lambda i: (i // tiles_per_b, 0, 0))
    else:
        mod_spec = pl.BlockSpec((tm, d), lambda i: (i, 0))
    return pl.pallas_call(
        _norm_mod_kernel,
        grid=(r // tm,),
        in_specs=[
            pl.BlockSpec((tm, d), lambda i: (i, 0)),
            pl.BlockSpec((1, d), lambda i: (0, 0)),
            mod_spec,
            mod_spec,
        ],
        out_specs=pl.BlockSpec((tm, d), lambda i: (i, 0)),
        out_shape=jax.ShapeDtypeStruct((r, d), out_dtype),
        compiler_params=_params("parallel"),
        name="norm_mod",
    )(x, g, shift, scale)


def _matmul_kernel(a_ref, w_ref, o_ref):
    o_ref[...] = jnp.dot(
        a_ref[...].astype(BF16), w_ref[...].astype(BF16), preferred_element_type=F32
    ).astype(o_ref.dtype)


def _matmul_res_kernel(a_ref, w_ref, x_ref, gm_ref, o_ref):
    acc = jnp.dot(a_ref[...].astype(BF16), w_ref[...].astype(BF16), preferred_element_type=F32)
    o_ref[...] = x_ref[...] + gm_ref[...] * acc


MATMUL_W_TILE_ELEMS = 2 * 1024 * 1024
MATMUL_A_TILE_ELEMS = 4 * 1024 * 1024


def _matmul(a, w, layer, tm, tn, res=None, gate=None, rows_per_batch=0):
    r, k = a.shape
    n = w.shape[2]
    tm = min(tm, MATMUL_A_TILE_ELEMS // k)
    tn = min(tn, MATMUL_W_TILE_ELEMS // k, MATMUL_W_TILE_ELEMS * 1024 // (tm * k))
    in_specs = [
        pl.BlockSpec((tm, k), lambda i, j: (i, 0)),
        pl.BlockSpec((None, k, tn), lambda i, j: (layer, 0, j)),
    ]
    args = [a, w]
    kernel = _matmul_kernel
    if res is not None:
        kernel = _matmul_res_kernel
        in_specs.append(pl.BlockSpec((tm, tn), lambda i, j: (i, j)))
        if rows_per_batch:
            tiles_per_b = rows_per_batch // tm
            in_specs.append(pl.BlockSpec((None, 1, tn), lambda i, j: (i // tiles_per_b, 0, j)))
        else:
            in_specs.append(pl.BlockSpec((tm, tn), lambda i, j: (i, j)))
        args += [res, gate]
    return pl.pallas_call(
        kernel,
        grid=(r // tm, n // tn),
        in_specs=in_specs,
        out_specs=pl.BlockSpec((tm, tn), lambda i, j: (i, j)),
        out_shape=jax.ShapeDtypeStruct((r, n), F32),
        compiler_params=_params("parallel", "parallel"),
        name="matmul_res" if res is not None else "matmul",
    )(*args)


def _cast_kernel(x_ref, o_ref):
    o_ref[...] = x_ref[...].astype(o_ref.dtype)


def _cast_bf16(w, rows=512):
    n_l, k, n = w.shape
    spec = pl.BlockSpec((None, rows, n), lambda l, i: (l, i, 0))
    return pl.pallas_call(
        _cast_kernel,
        grid=(n_l, k // rows),
        in_specs=[spec],
        out_specs=spec,
        out_shape=jax.ShapeDtypeStruct(w.shape, BF16),
        compiler_params=_params("parallel", "parallel"),
        name="cast_bf16",
    )(w)


def _matmul_res_norm_kernel(a_ref, w_ref, x_ref, gm_ref, g_ref, sh_ref, sc_ref, *out_refs):
    acc = jnp.dot(a_ref[...].astype(BF16), w_ref[...], preferred_element_type=F32)
    x = x_ref[...] + gm_ref[...] * acc
    if len(out_refs) == 2:
        out_refs[0][...] = x
    r = lax.rsqrt(jnp.mean(x * x, axis=-1, keepdims=True) + EPS)
    y = (x * r) * g_ref[...]
    out_refs[-1][...] = (y * (1.0 + sc_ref[...]) + sh_ref[...]).astype(out_refs[-1].dtype)


def _matmul_res_norm(a, w_bf16, layer, res, gate, g, shift, scale, rows_per_batch, tm, norm_dtype, keep_x):
    r, k = a.shape
    n = w_bf16.shape[2]
    tiles_per_b = rows_per_batch // tm
    row_spec = pl.BlockSpec((tm, n), lambda i: (i, 0))
    mod_spec = pl.BlockSpec((None, 1, n), lambda i: (i // tiles_per_b, 0, 0))
    out_specs = [row_spec, row_spec] if keep_x else [row_spec]
    out_shape = [jax.ShapeDtypeStruct((r, n), F32)] if keep_x else []
    out_shape.append(jax.ShapeDtypeStruct((r, n), norm_dtype))
    outs = pl.pallas_call(
        _matmul_res_norm_kernel,
        grid=(r // tm,),
        in_specs=[
            pl.BlockSpec((tm, k), lambda i: (i, 0)),
            pl.BlockSpec((None, k, n), lambda i: (layer, 0, 0), pipeline_mode=pl.Buffered(1)),
            row_spec,
            mod_spec,
            pl.BlockSpec((1, n), lambda i: (0, 0)),
            mod_spec,
            mod_spec,
        ],
        out_specs=out_specs,
        out_shape=out_shape,
        compiler_params=_params("parallel"),
        name="matmul_res_norm",
    )(a, w_bf16, res, gate, g, shift, scale)
    return (outs[0], outs[1]) if keep_x else (None, outs[0])


def _kv_project_kernel(x_ref, g_ref, sh_ref, sc_ref, w_ref, rows_ref, hm_ref, *, tm, transpose):
    x = x_ref[...]
    r = lax.rsqrt(jnp.mean(x * x, axis=-1, keepdims=True) + EPS)
    y = (x * r) * g_ref[...]
    h = (y * (1.0 + sc_ref[...]) + sh_ref[...]).astype(BF16)
    acc = jnp.dot(h, w_ref[...], preferred_element_type=F32)
    for head in range(H_B):
        cols = acc[:, head * DH_B:(head + 1) * DH_B]
        rows_ref[pl.ds(head, tm, stride=H_B), :] = cols
        hm_ref[head] = (cols.T if transpose else cols).astype(BF16)


def _kv_project(x, g, shift, scale, w_kv_bf16, part, b, t, transpose, tm=512):
    r, d = x.shape
    width = H_B * DH_B
    tiles_per_b = t // tm
    mod_spec = pl.BlockSpec((None, 1, d), lambda i: (i // tiles_per_b, 0, 0))
    if transpose:
        hm_spec = pl.BlockSpec((None, H_B, DH_B, tm), lambda i: (i // tiles_per_b, 0, 0, i % tiles_per_b))
        hm_shape = (b, H_B, DH_B, t)
    else:
        hm_spec = pl.BlockSpec((None, H_B, tm, DH_B), lambda i: (i // tiles_per_b, 0, i % tiles_per_b, 0))
        hm_shape = (b, H_B, t, DH_B)
    rows_out, hm = pl.pallas_call(
        functools.partial(_kv_project_kernel, tm=tm, transpose=transpose),
        grid=(r // tm,),
        in_specs=[
            pl.BlockSpec((tm, d), lambda i: (i, 0)),
            pl.BlockSpec((1, d), lambda i: (0, 0)),
            mod_spec,
            mod_spec,
            pl.BlockSpec((None, d, width), lambda i: (0, 0, part), pipeline_mode=pl.Buffered(1)),
        ],
        out_specs=[pl.BlockSpec((tm * H_B, DH_B), lambda i: (i, 0)), hm_spec],
        out_shape=[
            jax.ShapeDtypeStruct((r * H_B, DH_B), F32),
            jax.ShapeDtypeStruct(hm_shape, BF16),
        ],
        compiler_params=_params("parallel"),
        name="kv_project",
    )(x, g, shift, scale, w_kv_bf16)
    return rows_out.reshape(b, t, H_B, DH_B), hm


def _rope_tables(pos):
    half = DK_A // 2
    inv = 1.0 / (ROPE_BASE ** jnp.linspace(0.0, 1.0, half, dtype=F32))
    ang = pos.astype(F32)[:, None] * inv[None, :]
    return jnp.cos(ang), jnp.sin(ang)


def _retention_tables(chunk):
    log_g = jnp.log1p(-(2.0 ** (-5.0 - jnp.arange(H_A, dtype=F32))))
    i = jnp.arange(chunk, dtype=F32)
    diff = i[:, None] - i[None, :]
    decay = jnp.where(diff >= 0, jnp.exp(jnp.maximum(diff, 0.0)[None] * log_g[:, None, None]), 0.0)
    cross_scale = jnp.exp((i + 1.0)[:, None] * log_g[None, :]).T[:, :, None]
    wk = jnp.exp((chunk - 1.0 - i)[:, None] * log_g[None, :]).T[:, :, None]
    state_decay = jnp.exp(chunk * log_g)
    return decay, cross_scale, wk, state_decay


def _retention_chunk(q, k, v, g, cos, sin, decay, cross_scale, wk, state_decay, s):
    half = DK_A // 2

    def rot(x):
        x1 = x[:, :half]
        x2 = x[:, half:]
        return jnp.concatenate([x1 * cos - x2 * sin, x1 * sin + x2 * cos], axis=-1)

    q = rot(q)
    k = rot(k) * (DK_A ** -0.5)
    vb = v.astype(BF16)
    qb = q.astype(BF16)
    kb = k.astype(BF16)
    scores = lax.dot_general(qb, kb, (((1,), (1,)), ((), ())), preferred_element_type=F32) * decay
    inner = jnp.dot(scores.astype(BF16), vb, preferred_element_type=F32)
    cross = jnp.dot(qb, s.astype(BF16), preferred_element_type=F32) * cross_scale
    kw = (k * wk).astype(BF16)
    upd = lax.dot_general(kw, vb, (((0,), (0,)), ((), ())), preferred_element_type=F32)
    s_new = state_decay * s + upd
    o = inner + cross
    o = o * lax.rsqrt(jnp.mean(o * o, axis=-1, keepdims=True) + EPS)
    return g * jax.nn.sigmoid(g) * o, s_new


RET_HEADS_PER_STEP = 4
RET_CHUNKS_PER_STEP = 1


def _retention_kernel(sdec_ref, q_ref, k_ref, v_ref, g_ref, cos_ref, sin_ref, dec_ref, cs_ref,
                      wk_ref, *refs, chunk, heads, chunks, n_steps, zero_state, layer, init_stack):
    a_ref, sout_ref, s_scr = refs[-3:]
    hg = pl.program_id(1)
    c = pl.program_id(2)

    @pl.when(c == 0)
    def _():
        s_scr[...] = jnp.zeros(s_scr.shape, F32) if zero_state else refs[0][...]

    for hh in range(heads):
        s = s_scr[hh]
        kcols = slice(hh * DK_A, (hh + 1) * DK_A)
        vcols = slice(hh * DV_A, (hh + 1) * DV_A)
        for ci in range(chunks):
            rows = slice(ci * chunk, (ci + 1) * chunk)
            a, s = _retention_chunk(
                q_ref[rows, kcols], k_ref[rows, kcols], v_ref[rows, vcols], g_ref[rows, vcols],
                cos_ref[rows, :], sin_ref[rows, :], dec_ref[hh], cs_ref[hh], wk_ref[hh],
                sdec_ref[hg * heads + hh], s)
            a_ref[rows, vcols] = a.astype(a_ref.dtype)
        s_scr[hh] = s

        @pl.when(c == n_steps - 1)
        def _():
            _store_state(sout_ref, hh, s, layer, init_stack)


def _store_state(sout_ref, idx, s, layer, init_stack):
    if not init_stack:
        sout_ref[idx] = s
        return
    for slab in range(sout_ref.shape[0]):
        sout_ref[slab, idx] = s if slab == layer else jnp.zeros(s.shape, s.dtype)


def _retention(qkvg, s_in, layer, pos, out_dtype, s_out_prev):
    b, t, _ = qkvg.shape
    chunk = math.gcd(t, RET_CHUNK)
    n_chunks = t // chunk
    heads = RET_HEADS_PER_STEP
    chunks = math.gcd(n_chunks, RET_CHUNKS_PER_STEP)
    n_steps = n_chunks // chunks
    rows = chunk * chunks
    assert H_A % heads == 0
    cos, sin = _rope_tables(pos)
    decay, cross_scale, wk, state_decay = _retention_tables(chunk)
    k_off = QK_A // (heads * DK_A)
    v_off = 2 * QK_A // (heads * DV_A)
    g_off = (2 * QK_A + V_A) // (heads * DV_A)
    init_stack = s_out_prev is None
    kernel = functools.partial(_retention_kernel, chunk=chunk, heads=heads, chunks=chunks, n_steps=n_steps,
                               zero_state=s_in is None, layer=layer, init_stack=init_stack)
    state_spec = pl.BlockSpec((None, None, heads, DK_A, DV_A), lambda bi, h, c: (layer, bi, h, 0, 0))
    state_args, state_specs, aliases = [], [], {}
    if s_in is not None:
        state_args.append(s_in)
        state_specs.append(state_spec)
    if init_stack:
        out_state_spec = pl.BlockSpec((N_A, None, heads, DK_A, DV_A), lambda bi, h, c: (0, bi, h, 0, 0))
    else:
        out_state_spec = state_spec
        aliases = {10 + len(state_args): 1}
        state_args.append(s_out_prev)
        state_specs.append(pl.BlockSpec(memory_space=pl.ANY))
    return pl.pallas_call(
        kernel,
        grid=(b, H_A // heads, n_steps),
        in_specs=[
            pl.BlockSpec(memory_space=pltpu.SMEM),
            pl.BlockSpec((None, rows, heads * DK_A), lambda bi, h, c: (bi, c, h)),
            pl.BlockSpec((None, rows, heads * DK_A), lambda bi, h, c: (bi, c, k_off + h)),
            pl.BlockSpec((None, rows, heads * DV_A), lambda bi, h, c: (bi, c, v_off + h)),
            pl.BlockSpec((None, rows, heads * DV_A), lambda bi, h, c: (bi, c, g_off + h)),
            pl.BlockSpec((rows, DK_A // 2), lambda bi, h, c: (c, 0)),
            pl.BlockSpec((rows, DK_A // 2), lambda bi, h, c: (c, 0)),
            pl.BlockSpec((heads, chunk, chunk), lambda bi, h, c: (h, 0, 0)),
            pl.BlockSpec((heads, chunk, 1), lambda bi, h, c: (h, 0, 0)),
            pl.BlockSpec((heads, chunk, 1), lambda bi, h, c: (h, 0, 0)),
        ] + state_specs,
        out_specs=[
            pl.BlockSpec((None, rows, heads * DV_A), lambda bi, h, c: (bi, c, h)),
            out_state_spec,
        ],
        out_shape=[
            jax.ShapeDtypeStruct((b, t, V_A), out_dtype),
            jax.ShapeDtypeStruct((N_A, b, H_A, DK_A, DV_A), F32),
        ],
        scratch_shapes=[pltpu.VMEM((heads, DK_A, DV_A), F32)],
        input_output_aliases=aliases,
        compiler_params=_params("parallel", "parallel", "arbitrary"),
        name="retention",
    )(state_decay, qkvg, qkvg, qkvg, qkvg, cos, sin, decay, cross_scale, wk, *state_args)


PROJ_RET_ROWS = 1024


def _proj_retention_kernel(sdec_ref, h_ref, wq_ref, wkey_ref, wv_ref, wg_ref, cos_ref, sin_ref, dec_ref,
                           cs_ref, kdecay_ref, *refs, chunk, n_steps, layer, init_stack):
    a_ref, sout_ref, qkvg_scr, s_scr = refs[-4:]
    head = pl.program_id(1)
    r = pl.program_id(2)

    @pl.when(r == 0)
    def _():
        s_scr[...] = jnp.zeros(s_scr.shape, F32)

    hb = h_ref[...]
    col = 0
    for w_ref in (wq_ref, wkey_ref, wv_ref, wg_ref):
        width = w_ref.shape[1]
        qkvg_scr[:, col:col + width] = jnp.dot(hb, w_ref[...].astype(BF16), preferred_element_type=F32)
        col += width

    s = s_scr[...]
    for ci in range(h_ref.shape[0] // chunk):
        rows = slice(ci * chunk, (ci + 1) * chunk)
        a, s = _retention_chunk(
            qkvg_scr[rows, 0:DK_A], qkvg_scr[rows, DK_A:2 * DK_A],
            qkvg_scr[rows, 2 * DK_A:2 * DK_A + DV_A], qkvg_scr[rows, 2 * DK_A + DV_A:],
            cos_ref[rows, :], sin_ref[rows, :], dec_ref[...], cs_ref[...], kdecay_ref[...],
            sdec_ref[head], s)
        a_ref[rows, :] = a.astype(a_ref.dtype)
    s_scr[...] = s

    @pl.when(r == n_steps - 1)
    def _():
        _store_state(sout_ref, 0, s, layer, init_stack)


def _proj_retention(h_act, w_in, layer, b, t, pos, out_dtype, s_out_prev):
    chunk = math.gcd(t, RET_CHUNK)
    rows = min(PROJ_RET_ROWS, t)
    assert t % rows == 0 and rows % chunk == 0
    n_steps = t // rows
    cos, sin = _rope_tables(pos)
    decay, cross_scale, wk, state_decay = _retention_tables(chunk)
    k_off = QK_A // DK_A
    v_off = 2 * QK_A // DV_A
    g_off = (2 * QK_A + V_A) // DV_A
    d = h_act.shape[1]
    init_stack = s_out_prev is None
    kernel = functools.partial(_proj_retention_kernel, chunk=chunk, n_steps=n_steps, layer=layer,
                               init_stack=init_stack)
    extra_args, extra_specs, aliases = [], [], {}
    if init_stack:
        state_spec = pl.BlockSpec((N_A, None, 1, DK_A, DV_A), lambda bi, h, r: (0, bi, h, 0, 0))
    else:
        state_spec = pl.BlockSpec((None, None, 1, DK_A, DV_A), lambda bi, h, r: (layer, bi, h, 0, 0))
        aliases = {11: 1}
        extra_args.append(s_out_prev)
        extra_specs.append(pl.BlockSpec(memory_space=pl.ANY))
    return pl.pallas_call(
        kernel,
        grid=(b, H_A, n_steps),
        in_specs=[
            pl.BlockSpec(memory_space=pltpu.SMEM),
            pl.BlockSpec((rows, d), lambda bi, h, r: (bi * n_steps + r, 0)),
            pl.BlockSpec((None, d, DK_A), lambda bi, h, r: (layer, 0, h)),
            pl.BlockSpec((None, d, DK_A), lambda bi, h, r: (layer, 0, k_off + h)),
            pl.BlockSpec((None, d, DV_A), lambda bi, h, r: (layer, 0, v_off + h)),
            pl.BlockSpec((None, d, DV_A), lambda bi, h, r: (layer, 0, g_off + h)),
            pl.BlockSpec((rows, DK_A // 2), lambda bi, h, r: (r, 0)),
            pl.BlockSpec((rows, DK_A // 2), lambda bi, h, r: (r, 0)),
            pl.BlockSpec((None, chunk, chunk), lambda bi, h, r: (h, 0, 0)),
            pl.BlockSpec((None, chunk, 1), lambda bi, h, r: (h, 0, 0)),
            pl.BlockSpec((None, chunk, 1), lambda bi, h, r: (h, 0, 0)),
        ] + extra_specs,
        out_specs=[
            pl.BlockSpec((None, rows, DV_A), lambda bi, h, r: (bi, r, h)),
            state_spec,
        ],
        out_shape=[
            jax.ShapeDtypeStruct((b, t, V_A), out_dtype),
            jax.ShapeDtypeStruct((N_A, b, H_A, DK_A, DV_A), F32),
        ],
        scratch_shapes=[
            pltpu.VMEM((rows, 2 * DK_A + 2 * DV_A), F32),
            pltpu.VMEM((DK_A, DV_A), F32),
        ],
        input_output_aliases=aliases,
        compiler_params=_params("parallel", "parallel", "arbitrary"),
        name="proj_retention",
    )(state_decay, h_act, w_in, w_in, w_in, w_in, cos, sin, decay, cross_scale, wk, *extra_args)


def _select_topk(gate, n_sel, axis=1):
    pos = lax.broadcasted_iota(jnp.int32, gate.shape, axis).astype(F32)
    picks = []
    for _ in range(n_sel):
        m = jnp.max(gate, axis=axis, keepdims=True)
        idx = jnp.min(jnp.where(gate == m, pos, float(gate.shape[axis])), axis=axis, keepdims=True)
        valid = m > -jnp.inf
        picks.append((idx, valid))
        gate = jnp.where((pos == idx) & valid, -jnp.inf, gate)
    return picks


MOBA_HEADS_PER_STEP = 2
MOBA_PROJ_HEADS = 2


def _moba_prompt_kernel(h_ref, wq_ref, wg_ref, k_ref, vt_ref, a_ref, q_scr, g_scr, *, n_blocks, heads):
    hb = h_ref[...]
    for h0 in range(0, heads, MOBA_PROJ_HEADS):
        cols = slice(h0 * DH_B, (h0 + MOBA_PROJ_HEADS) * DH_B)
        q_scr[:, cols] = jnp.dot(hb, wq_ref[:, cols].astype(BF16), preferred_element_type=F32)
        g_scr[:, cols] = jnp.dot(hb, wg_ref[:, cols].astype(BF16), preferred_element_type=F32)
    for hh in range(heads):
        cols = slice(hh * DH_B, (hh + 1) * DH_B)
        _moba_prompt_head(q_scr, g_scr, k_ref.at[hh], vt_ref.at[hh], a_ref, cols, n_blocks)


def _moba_prompt_head(q_ref, g_ref, k_ref, vt_ref, a_ref, cols, n_blocks):
    bs = MOBA_BLOCK
    scale = DH_B ** -0.5
    n_sel = min(MOBA_TOPK, n_blocks)
    kb = k_ref[...]
    ones_rows = 16
    vt = jnp.concatenate([vt_ref[...], jnp.ones((ones_rows, kb.shape[0]), BF16)], axis=0)
    means_b = jnp.mean(kb.astype(F32).reshape(n_blocks, bs, DH_B), axis=1).astype(BF16)
    key = lax.broadcasted_iota(jnp.int32, (bs, bs), 0)
    qry = lax.broadcasted_iota(jnp.int32, (bs, bs), 1)
    causal = key <= qry

    for i in range(n_blocks):
        rows = slice(i * bs, (i + 1) * bs)
        q = q_ref[rows, cols]
        qs = (q * (scale * LOG2_E)).astype(BF16)
        picks = None
        if i > n_sel:
            gate = lax.dot_general(means_b, q.astype(BF16), (((1,), (1,)), ((), ())),
                                   preferred_element_type=F32)
            blk = lax.broadcasted_iota(jnp.int32, gate.shape, 0)
            picks = _select_topk(jnp.where(blk < i, gate, -jnp.inf), n_sel, axis=0)
        blocks = []
        for j in range(i + 1):
            sj = lax.dot_general(kb[j * bs:(j + 1) * bs, :], qs, (((1,), (1,)), ((), ())),
                                 preferred_element_type=F32)
            if j == i:
                sj = jnp.where(causal, sj, -jnp.inf)
            elif picks is not None:
                chosen = picks[0][0] == float(j)
                for idx, _ in picks[1:]:
                    chosen = chosen | (idx == float(j))
                sj = jnp.where(chosen, sj, -jnp.inf)
            blocks.append(sj)
        m = jnp.max(functools.reduce(jnp.maximum, blocks), axis=0, keepdims=True)
        acc = jnp.zeros((DH_B + ones_rows, bs), F32)
        for j, sj in enumerate(blocks):
            p = jnp.exp2(sj - m).astype(BF16)
            acc = acc + jnp.dot(vt[:, j * bs:(j + 1) * bs], p, preferred_element_type=F32)
        o = (acc[:DH_B, :] / acc[DH_B:DH_B + 1, :]).T
        g = g_ref[rows, cols]
        a_ref[rows, cols] = (g * jax.nn.sigmoid(g) * o).astype(a_ref.dtype)


def _moba_prompt(h_act, w_q, layer, b, t, k_hm, vt_hm, out_dtype):
    d = h_act.shape[1]
    n_blocks = t // MOBA_BLOCK
    heads = MOBA_HEADS_PER_STEP
    assert n_blocks * MOBA_BLOCK == t and n_blocks <= 128 and H_B % heads == 0
    g_off = H_B // heads
    kernel = functools.partial(_moba_prompt_kernel, n_blocks=n_blocks, heads=heads)
    return pl.pallas_call(
        kernel,
        grid=(b, H_B // heads),
        in_specs=[
            pl.BlockSpec((t, d), lambda bi, h: (bi, 0)),
            pl.BlockSpec((None, d, heads * DH_B), lambda bi, h: (layer, 0, h)),
            pl.BlockSpec((None, d, heads * DH_B), lambda bi, h: (layer, 0, g_off + h)),
            pl.BlockSpec((None, heads, t, DH_B), lambda bi, h: (bi, h, 0, 0)),
            pl.BlockSpec((None, heads, DH_B, t), lambda bi, h: (bi, h, 0, 0)),
        ],
        out_specs=pl.BlockSpec((None, t, heads * DH_B), lambda bi, h: (bi, 0, h)),
        out_shape=jax.ShapeDtypeStruct((b, t, D_MODEL), out_dtype),
        scratch_shapes=[pltpu.VMEM((t, heads * DH_B), F32), pltpu.VMEM((t, heads * DH_B), F32)],
        compiler_params=_params("parallel", "arbitrary"),
        name="moba_prompt",
    )(h_act, w_q, w_q, k_hm, vt_hm)


HEAD_GROUP = 8


GATHER_PAGES_PER_STEP = 8


def _gather_cache_kernel(pt_ref, *refs, page, n_groups, pages_per_step, pages_per_block):
    n_slabs = pages_per_step * n_groups
    o_ref, m_ref = refs[2 * n_slabs], refs[2 * n_slabs + 1]
    for part in range(2):
        lanes = slice(part * DH_B, (part + 1) * DH_B)
        for pg in range(pages_per_step):
            for grp in range(n_groups):
                src = refs[part * n_slabs + pg * n_groups + grp]
                src = src.reshape(page * HEAD_GROUP, DH_B)
                for hg in range(HEAD_GROUP):
                    rows = src[pl.ds(hg, page, stride=HEAD_GROUP), :]
                    o_ref[grp * HEAD_GROUP + hg, pg * page:(pg + 1) * page, lanes] = rows.astype(BF16)
    for blk in range(pages_per_step // pages_per_block):
        for grp in range(n_groups):
            total = None
            for pg in range(blk * pages_per_block, (blk + 1) * pages_per_block):
                part_sum = jnp.sum(refs[pg * n_groups + grp][...], axis=0)
                total = part_sum if total is None else total + part_sum
            m_ref[blk, grp * HEAD_GROUP:(grp + 1) * HEAD_GROUP, :] = total / MOBA_BLOCK


def _gather_cache(cache_k, cache_v, page_table_flat, n_seq, n_pages):
    n_pool, page, n_h, dh = cache_k.shape
    pages_per_block = MOBA_BLOCK // page
    pps = GATHER_PAGES_PER_STEP
    assert n_h == H_B and dh == DH_B and n_h % HEAD_GROUP == 0
    assert pps % pages_per_block == 0 and n_pages % pps == 0
    n_blocks = n_pages // pages_per_block
    n_groups = n_h // HEAD_GROUP
    grouped = [c.reshape(n_pool, page, n_groups, HEAD_GROUP, dh) for c in (cache_k, cache_v)]

    def page_map(pg, grp):
        return lambda bi, j, pt: (pt[bi * n_pages + pps * j + pg], 0, grp, 0, 0)

    def slab_specs():
        return [pl.BlockSpec((None, page, None, HEAD_GROUP, dh), page_map(pg, grp))
                for pg in range(pps) for grp in range(n_groups)]

    n_slabs = pps * n_groups
    blocks_per_step = pps // pages_per_block
    kernel = functools.partial(_gather_cache_kernel, page=page, n_groups=n_groups, pages_per_step=pps,
                               pages_per_block=pages_per_block)
    return pl.pallas_call(
        kernel,
        grid_spec=pltpu.PrefetchScalarGridSpec(
            num_scalar_prefetch=1,
            grid=(n_seq, n_pages // pps),
            in_specs=slab_specs() + slab_specs(),
            out_specs=[
                pl.BlockSpec((None, n_h, pps * page, 2 * dh), lambda bi, j, pt: (bi, 0, j, 0)),
                pl.BlockSpec((None, blocks_per_step, n_h, dh), lambda bi, j, pt: (bi, j, 0, 0)),
            ],
        ),
        out_shape=[
            jax.ShapeDtypeStruct((n_seq, n_h, n_pages * page, 2 * dh), BF16),
            jax.ShapeDtypeStruct((n_seq, n_blocks, n_h, dh), F32),
        ],
        compiler_params=_params("parallel", "arbitrary"),
        name="gather_cache",
    )(page_table_flat, *([grouped[0]] * n_slabs), *([grouped[1]] * n_slabs))


def _moba_select_kernel(q_ref, means_ref, o_ref, *, t, n_blocks, past):
    q = q_ref[...]
    rows = t * H_B
    qrep = jnp.concatenate([jnp.broadcast_to(q[ti:ti + 1, :], (H_B, D_MODEL)) for ti in range(t)], axis=0)
    lane_h = lax.broadcasted_iota(jnp.int32, (rows, D_MODEL), 1) >> (DH_B.bit_length() - 1)
    row_h = lax.broadcasted_iota(jnp.int32, (rows, D_MODEL), 0) & (H_B - 1)
    qexp = jnp.where(lane_h == row_h, qrep, 0.0)
    means = jnp.concatenate([means_ref[...], jnp.zeros((128 - n_blocks, D_MODEL), F32)], axis=0)
    gate = lax.dot_general(qexp.astype(BF16), means.astype(BF16), (((1,), (1,)), ((), ())),
                           preferred_element_type=F32)
    lane = lax.broadcasted_iota(jnp.int32, gate.shape, 1)
    token = lax.broadcasted_iota(jnp.int32, gate.shape, 0) >> (H_B.bit_length() - 1)
    own = (past + token) >> (MOBA_BLOCK.bit_length() - 1)
    gate = jnp.where(lane < own, gate, -jnp.inf)
    out = jnp.zeros(gate.shape, F32)
    for r, (idx, _) in enumerate(_select_topk(gate, MOBA_TOPK)):
        out = jnp.where(lane == r, idx, out)
    o_ref[...] = out.astype(jnp.int32)


def _moba_select(qg, means, past):
    b, t, _ = qg.shape
    n_blocks = means.shape[1]
    assert MOBA_TOPK <= n_blocks <= 128
    kernel = functools.partial(_moba_select_kernel, t=t, n_blocks=n_blocks, past=past)
    return pl.pallas_call(
        kernel,
        grid=(b,),
        in_specs=[
            pl.BlockSpec((None, t, D_MODEL), lambda bi: (bi, 0, 0)),
            pl.BlockSpec((None, n_blocks, D_MODEL), lambda bi: (bi, 0, 0)),
        ],
        out_specs=pl.BlockSpec((None, t * H_B, 128), lambda bi: (bi, 0, 0)),
        out_shape=jax.ShapeDtypeStruct((b, t * H_B, 128), jnp.int32),
        compiler_params=_params("parallel"),
        name="moba_select",
    )(qg, means)


SAMPLE_HEADS_PER_STEP = 4


def _moba_sample_kernel(sel_ref, new_ref, *refs, t, n_slots, heads):
    a_ref = refs[heads * n_slots]
    for hh in range(heads):
        cols = slice(hh * DH_B, (hh + 1) * DH_B)
        _moba_sample_head(new_ref.at[hh], refs[hh * n_slots:(hh + 1) * n_slots], a_ref, cols, t, n_slots)


def _moba_sample_head(new_ref, kv_refs, a_ref, cols, t, n_slots):
    keys_per_token = (n_slots // t) * kv_refs[0].shape[0]

    q = new_ref[:, 0:DH_B]
    qb = q.astype(BF16)
    kall = jnp.concatenate([r[:, :DH_B] for r in kv_refs], axis=0)
    vall = jnp.concatenate([r[:, DH_B:] for r in kv_refs], axis=0)
    s = lax.dot_general(qb, kall, (((1,), (1,)), ((), ())), preferred_element_type=F32) * (DH_B ** -0.5)
    row = lax.broadcasted_iota(jnp.int32, s.shape, 0)
    col = lax.broadcasted_iota(jnp.int32, s.shape, 1)
    mine = (col >= row * keys_per_token) & (col < (row + 1) * keys_per_token)
    s = jnp.where(mine, s, -jnp.inf)

    kn = new_ref[:, 2 * DH_B:3 * DH_B]
    vn = new_ref[:, 3 * DH_B:4 * DH_B]
    rown = lax.broadcasted_iota(jnp.int32, (t, 1), 0)
    s_own = []
    for tj in range(t):
        sj = jnp.sum(q * kn[tj:tj + 1, :], axis=-1, keepdims=True) * (DH_B ** -0.5)
        s_own.append(jnp.where(rown >= tj, sj, -jnp.inf))
    m = jnp.max(s, axis=-1, keepdims=True)
    for sj in s_own:
        m = jnp.maximum(m, sj)
    p = jnp.exp(s - m)
    l = jnp.sum(p, axis=-1, keepdims=True)
    acc = jnp.dot(p.astype(BF16), vall, preferred_element_type=F32)
    for tj, sj in enumerate(s_own):
        pj = jnp.exp(sj - m)
        l = l + pj
        acc = acc + pj * vn[tj:tj + 1, :]
    o = acc / l
    g = new_ref[:, DH_B:2 * DH_B]
    a_ref[:, cols] = (g * jax.nn.sigmoid(g) * o).astype(a_ref.dtype)


def _moba_sample(qg, kv_new, past_kv, sel_flat):
    b, t, _ = qg.shape
    n_slots = t * MOBA_TOPK
    heads = SAMPLE_HEADS_PER_STEP
    assert H_B % heads == 0
    per_head = [a.reshape(b, t, H_B, DH_B) for a in
                (qg[:, :, :D_MODEL], qg[:, :, D_MODEL:], kv_new[:, :, :D_MODEL], kv_new[:, :, D_MODEL:])]
    new_rows = jnp.concatenate(per_head, axis=-1).transpose(0, 2, 1, 3)

    def slot_map(hh, ti, r):
        def index_map(bi, hg, sel):
            h = hg * heads + hh
            return (bi, h, sel[((bi * t + ti) * H_B + h) * MOBA_TOPK + r], 0)
        return index_map

    slot_specs = [pl.BlockSpec((None, None, MOBA_BLOCK, 2 * DH_B), slot_map(hh, ti, r))
                  for hh in range(heads) for ti in range(t) for r in range(MOBA_TOPK)]

    kernel = functools.partial(_moba_sample_kernel, t=t, n_slots=n_slots, heads=heads)
    return pl.pallas_call(
        kernel,
        grid_spec=pltpu.PrefetchScalarGridSpec(
            num_scalar_prefetch=1,
            grid=(b, H_B // heads),
            in_specs=[pl.BlockSpec((None, heads, t, 4 * DH_B), lambda bi, hg, sel: (bi, hg, 0, 0))] + slot_specs,
            out_specs=pl.BlockSpec((None, t, heads * DH_B), lambda bi, hg, sel: (bi, 0, hg)),
        ),
        out_shape=jax.ShapeDtypeStruct((b, t, D_MODEL), F32),
        compiler_params=_params("parallel", "arbitrary"),
        name="moba_sample",
    )(sel_flat, new_rows, *([past_kv] * len(slot_specs)))


PROMPT_NORM_ROWS = 512
PROMPT_RES_ROWS = 512
SAMPLE_COLS = 1024


def _split_mod(v, n_parts, d, expand):
    return tuple(expand(v[:, i * d:(i + 1) * d]) for i in range(n_parts))


def _run_prompt(x, mods, mods_kv, mods_f, weights):
    (norm_g, w_in_a, w_out_a, w_q_b, w_o_b, kv_norm_g, w_kv, final_g) = weights
    b, t, d = x.shape
    rows = b * t
    pos = jnp.arange(t, dtype=jnp.int32)
    per_batch = lambda v: v[:, None, :]
    x2 = x.reshape(rows, d)
    shift, scale, _ = _split_mod(mods[0], 3, d, per_batch)
    h = _norm_mod(x2, norm_g[0][None, :], shift, scale, t, BF16, PROMPT_NORM_ROWS)
    new_s = k_new = v_new = k_hm = vt_hm = None
    for l in range(DEPTH):
        gate = _split_mod(mods[l], 3, d, per_batch)[2]
        if l < N_A:
            a, new_s = _proj_retention(h, w_in_a, l, b, t, pos, BF16, new_s)
            a, w_res, w_layer = a.reshape(rows, V_A), w_out_a, l
        else:
            a = _moba_prompt(h, w_q_b, l - N_A, b, t, k_hm, vt_hm, BF16)
            a, w_res, w_layer = a.reshape(rows, d), w_o_b, l - N_A
        last = l == DEPTH - 1
        if last:
            g_next = final_g
            sh_next, sc_next = _split_mod(mods_f, 2, d, per_batch)
        else:
            g_next = norm_g[l + 1]
            sh_next, sc_next, _ = _split_mod(mods[l + 1], 3, d, per_batch)
        x2, h = _matmul_res_norm(a, w_res, w_layer, x2, gate, g_next[None, :], sh_next, sc_next, t,
                                 PROMPT_RES_ROWS, F32 if last else BF16, keep_x=not last)
        if l == N_A - 1:
            kv_shift, kv_scale = _split_mod(mods_kv, 2, d, per_batch)
            k_new, k_hm = _kv_project(x2, kv_norm_g[None, :], kv_shift, kv_scale, w_kv, 0, b, t, False)
            v_new, vt_hm = _kv_project(x2, kv_norm_g[None, :], kv_shift, kv_scale, w_kv, 1, b, t, True)
    return h.reshape(b, t, d), new_s, k_new, v_new


def _run_sample(x, mods, mods_kv, mods_f, s_in, past, weights, past_kv, means):
    (norm_g, w_in_a, w_out_a, w_q_b, w_o_b, kv_norm_g, w_kv, final_g) = weights
    b, t, d = x.shape
    rows = b * t
    pos = past + jnp.arange(t, dtype=jnp.int32)
    per_row = lambda v: jnp.repeat(v, t, axis=0)

    def project(x2, g, shift, scale, w, layer):
        h = _norm_mod(x2, g[None, :], shift, scale, 0, F32, rows)
        return _matmul(h, w, layer, rows, SAMPLE_COLS)

    x2 = x.reshape(rows, d)
    new_s = kv = None
    for l in range(DEPTH):
        shift, scale, gate = _split_mod(mods[l], 3, d, per_row)
        if l < N_A:
            qkvg = project(x2, norm_g[l], shift, scale, w_in_a, l).reshape(b, t, -1)
            a, new_s = _retention(qkvg, s_in, l, pos, F32, new_s)
            x2 = _matmul(a.reshape(rows, V_A), w_out_a, l, rows, SAMPLE_COLS, res=x2, gate=gate)
        else:
            lb = l - N_A
            qg = project(x2, norm_g[l], shift, scale, w_q_b, lb).reshape(b, t, -1)
            sel = _moba_select(qg, means, past)[:, :, :MOBA_TOPK].reshape(-1)
            a = _moba_sample(qg, kv.reshape(b, t, -1), past_kv, sel)
            x2 = _matmul(a.reshape(rows, d), w_o_b, lb, rows, SAMPLE_COLS, res=x2, gate=gate)
        if l == N_A - 1:
            kv_shift, kv_scale = _split_mod(mods_kv, 2, d, per_row)
            kv = project(x2, kv_norm_g, kv_shift, kv_scale, w_kv, 0)
    f_shift, f_scale = _split_mod(mods_f, 2, d, per_row)
    y = _norm_mod(x2, final_g[None, :], f_shift, f_scale, 0, F32, rows)
    k_new = kv[:, :d].reshape(b, t, H_B, DH_B)
    v_new = kv[:, d:].reshape(b, t, H_B, DH_B)
    return y.reshape(b, t, d), new_s, k_new, v_new


def kernel(x_prompt, x_sample, state_ret, cache_k, cache_v, page_table, c_prompt, c_sample,
           norm_g, w_mod, b_mod, w_in_a, w_out_a, w_q_b, w_o_b,
           kv_norm_g, w_mod_kv, b_mod_kv, w_kv, final_g, w_mod_f, b_mod_f):
    bp = x_prompt.shape[0]
    bd, n_pages = page_table.shape
    n_pool, page, _, _ = cache_k.shape
    past_len = n_pages * page
    assert bp + bd <= MOD_ROWS

    c_all = jnp.concatenate([c_prompt, c_sample, jnp.zeros((MOD_ROWS - bp - bd, D_MODEL), F32)], axis=0)
    mods = _mod_matmul(c_all, w_mod, b_mod[:, None, :])
    mods_kv = _mod_matmul(c_all, w_mod_kv[None], b_mod_kv[None, None, :])[0]
    mods_f = _mod_matmul(c_all, w_mod_f[None], b_mod_f[None, None, :])[0]

    weights = (norm_g, w_in_a, _cast_bf16(w_out_a), w_q_b, _cast_bf16(w_o_b), kv_norm_g,
               _cast_bf16(w_kv[None]), final_g)

    y_p, s_p, k_p, v_p = _run_prompt(x_prompt, mods[:, :bp], mods_kv[:bp], mods_f[:bp], weights)

    pt_flat = page_table.reshape(-1)
    past_kv, means = _gather_cache(cache_k, cache_v, pt_flat, bd, n_pages)
    means = means.reshape(bd, -1, H_B * DH_B)
    y_s, s_s, k_s, v_s = _run_sample(
        x_sample, mods[:, bp:bp + bd], mods_kv[bp:bp + bd], mods_f[bp:bp + bd], state_ret, past_len,
        weights, past_kv, means)
    return (y_p, y_s, s_p, s_s, k_p, v_p, k_s, v_s)
```

```python
import functools
import math

import jax
import jax.numpy as jnp
from jax import lax
from jax.experimental import pallas as pl
from jax.experimental.pallas import tpu as pltpu

F32 = jnp.float32
BF16 = jnp.bfloat16

D_MODEL = 2048
DEPTH = 4
N_A = DEPTH // 2
H_A = 8
DK_A = D_MODEL // H_A
DV_A = 2 * DK_A
QK_A = H_A * DK_A
V_A = H_A * DV_A
RET_CHUNK = 256
H_B = 16
DH_B = D_MODEL // H_B
MOBA_BLOCK = 256
MOBA_TOPK = 3
ROPE_BASE = 10000.0
EPS = 1e-6
LOG2_E = math.log2(math.e)

VMEM_LIMIT_BYTES = 56 * 1024 * 1024
MOD_ROWS = 16


def _params(*sem):
    return pltpu.CompilerParams(dimension_semantics=sem, vmem_limit_bytes=VMEM_LIMIT_BYTES)


def _mod_kernel(c_ref, w_ref, b_ref, o_ref):
    acc = jnp.dot(c_ref[...].astype(BF16), w_ref[...].astype(BF16), preferred_element_type=F32)
    o_ref[...] = acc + b_ref[...]


def _mod_matmul(c, w, b, tn=2048):
    n_l, d, n = w.shape
    return pl.pallas_call(
        _mod_kernel,
        grid=(n_l, n // tn),
        in_specs=[
            pl.BlockSpec((MOD_ROWS, d), lambda l, j: (0, 0)),
            pl.BlockSpec((None, d, tn), lambda l, j: (l, 0, j)),
            pl.BlockSpec((None, 1, tn), lambda l, j: (l, 0, j)),
        ],
        out_specs=pl.BlockSpec((None, MOD_ROWS, tn), lambda l, j: (l, 0, j)),
        out_shape=jax.ShapeDtypeStruct((n_l, MOD_ROWS, n), F32),
        compiler_params=_params("parallel", "parallel"),
        name="mod_matmul",
    )(c, w, b)


def _norm_mod_kernel(x_ref, g_ref, sh_ref, sc_ref, o_ref):
    x = x_ref[...]
    r = lax.rsqrt(jnp.mean(x * x, axis=-1, keepdims=True) + EPS)
    y = (x * r) * g_ref[...]
    o_ref[...] = (y * (1.0 + sc_ref[...]) + sh_ref[...]).astype(o_ref.dtype)


def _norm_mod(x, g, shift, scale, rows_per_batch, out_dtype, tm):
    r, d = x.shape
    if rows_per_batch:
        tiles_per_b = rows_per_batch // tm
        mod_spec = pl.BlockSpec((None, 1, d), lambda i: (i // tiles_per_b, 0, 0))
    else:
        mod_spec = pl.BlockSpec((tm, d), lambda i: (i, 0))
    return pl.pallas_call(
        _norm_mod_kernel,
        grid=(r // tm,),
        in_specs=[
            pl.BlockSpec((tm, d), lambda i: (i, 0)),
            pl.BlockSpec((1, d), lambda i: (0, 0)),
            mod_spec,
            mod_spec,
        ],
        out_specs=pl.BlockSpec((tm, d), lambda i: (i, 0)),
        out_shape=jax.ShapeDtypeStruct((r, d), out_dtype),
        compiler_params=_params("parallel"),
        name="norm_mod",
    )(x, g, shift, scale)


def _matmul_kernel(a_ref, w_ref, o_ref):
    o_ref[...] = jnp.dot(
        a_ref[...].astype(BF16), w_ref[...].astype(BF16), preferred_element_type=F32
    ).astype(o_ref.dtype)


def _matmul_res_kernel(a_ref, w_ref, x_ref, gm_ref, o_ref):
    acc = jnp.dot(a_ref[...].astype(BF16), w_ref[...].astype(BF16), preferred_element_type=F32)
    o_ref[...] = x_ref[...] + gm_ref[...] * acc


MATMUL_W_TILE_ELEMS = 2 * 1024 * 1024
MATMUL_A_TILE_ELEMS = 4 * 1024 * 1024


def _matmul(a, w, layer, tm, tn, res=None, gate=None, rows_per_batch=0):
    r, k = a.shape
    n = w.shape[2]
    tm = min(tm, MATMUL_A_TILE_ELEMS // k)
    tn = min(tn, MATMUL_W_TILE_ELEMS // k, MATMUL_W_TILE_ELEMS * 1024 // (tm * k))
    in_specs = [
        pl.BlockSpec((tm, k), lambda i, j: (i, 0)),
        pl.BlockSpec((None, k, tn), lambda i, j: (layer, 0, j)),
    ]
    args = [a, w]
    kernel = _matmul_kernel
    if res is not None:
        kernel = _matmul_res_kernel
        in_specs.append(pl.BlockSpec((tm, tn), lambda i, j: (i, j)))
        if rows_per_batch:
            tiles_per_b = rows_per_batch // tm
            in_specs.append(pl.BlockSpec((None, 1, tn), lambda i, j: (i // tiles_per_b, 0, j)))
        else:
            in_specs.append(pl.BlockSpec((tm, tn), lambda i, j: (i, j)))
        args += [res, gate]
    return pl.pallas_call(
        kernel,
        grid=(r // tm, n // tn),
        in_specs=in_specs,
        out_specs=pl.BlockSpec((tm, tn), lambda i, j: (i, j)),
        out_shape=jax.ShapeDtypeStruct((r, n), F32),
        compiler_params=_params("parallel", "parallel"),
        name="matmul_res" if res is not None else "matmul",
    )(*args)


def _cast_kernel(x_ref, o_ref):
    o_ref[...] = x_ref[...].astype(o_ref.dtype)


def _cast_bf16(w, rows=512):
    n_l, k, n = w.shape
    spec = pl.BlockSpec((None, rows, n), lambda l, i: (l, i, 0))
    return pl.pallas_call(
        _cast_kernel,
        grid=(n_l, k // rows),
        in_specs=[spec],
        out_specs=spec,
        out_shape=jax.ShapeDtypeStruct(w.shape, BF16),
        compiler_params=_params("parallel", "parallel"),
        name="cast_bf16",
    )(w)


def _matmul_res_norm_kernel(a_ref, w_ref, x_ref, gm_ref, g_ref, sh_ref, sc_ref, *out_refs):
    acc = jnp.dot(a_ref[...].astype(BF16), w_ref[...], preferred_element_type=F32)
    x = x_ref[...] + gm_ref[...] * acc
    if len(out_refs) == 2:
        out_refs[0][...] = x
    r = lax.rsqrt(jnp.mean(x * x, axis=-1, keepdims=True) + EPS)
    y = (x * r) * g_ref[...]
    out_refs[-1][...] = (y * (1.0 + sc_ref[...]) + sh_ref[...]).astype(out_refs[-1].dtype)


def _matmul_res_norm(a, w_bf16, layer, res, gate, g, shift, scale, rows_per_batch, tm, norm_dtype, keep_x):
    r, k = a.shape
    n = w_bf16.shape[2]
    tiles_per_b = rows_per_batch // tm
    row_spec = pl.BlockSpec((tm, n), lambda i: (i, 0))
    mod_spec = pl.BlockSpec((None, 1, n), lambda i: (i // tiles_per_b, 0, 0))
    out_specs = [row_spec, row_spec] if keep_x else [row_spec]
    out_shape = [jax.ShapeDtypeStruct((r, n), F32)] if keep_x else []
    out_shape.append(jax.ShapeDtypeStruct((r, n), norm_dtype))
    outs = pl.pallas_call(
        _matmul_res_norm_kernel,
        grid=(r // tm,),
        in_specs=[
            pl.BlockSpec((tm, k), lambda i: (i, 0)),
            pl.BlockSpec((None, k, n), lambda i: (layer, 0, 0), pipeline_mode=pl.Buffered(1)),
            row_spec,
            mod_spec,
            pl.BlockSpec((1, n), lambda i: (0, 0)),
            mod_spec,
            mod_spec,
        ],
        out_specs=out_specs,
        out_shape=out_shape,
        compiler_params=_params("parallel"),
        name="matmul_res_norm",
    )(a, w_bf16, res, gate, g, shift, scale)
    return (outs[0], outs[1]) if keep_x else (None, outs[0])


def _kv_project_kernel(x_ref, g_ref, sh_ref, sc_ref, w_ref, rows_ref, hm_ref, *, tm, transpose):
    x = x_ref[...]
    r = lax.rsqrt(jnp.mean(x * x, axis=-1, keepdims=True) + EPS)
    y = (x * r) * g_ref[...]
    h = (y * (1.0 + sc_ref[...]) + sh_ref[...]).astype(BF16)
    acc = jnp.dot(h, w_ref[...], preferred_element_type=F32)
    for head in range(H_B):
        cols = acc[:, head * DH_B:(head + 1) * DH_B]
        rows_ref[pl.ds(head, tm, stride=H_B), :] = cols
        hm_ref[head] = (cols.T if transpose else cols).astype(BF16)


def _kv_project(x, g, shift, scale, w_kv_bf16, part, b, t, transpose, tm=512):
    r, d = x.shape
    width = H_B * DH_B
    tiles_per_b = t // tm
    mod_spec = pl.BlockSpec((None, 1, d), lambda i: (i // tiles_per_b, 0, 0))
    if transpose:
        hm_spec = pl.BlockSpec((None, H_B, DH_B, tm), lambda i: (i // tiles_per_b, 0, 0, i % tiles_per_b))
        hm_shape = (b, H_B, DH_B, t)
    else:
        hm_spec = pl.BlockSpec((None, H_B, tm, DH_B), lambda i: (i // tiles_per_b, 0, i % tiles_per_b, 0))
        hm_shape = (b, H_B, t, DH_B)
    rows_out, hm = pl.pallas_call(
        functools.partial(_kv_project_kernel, tm=tm, transpose=transpose),
        grid=(r // tm,),
        in_specs=[
            pl.BlockSpec((tm, d), lambda i: (i, 0)),
            pl.BlockSpec((1, d), lambda i: (0, 0)),
            mod_spec,
            mod_spec,
            pl.BlockSpec((None, d, width), lambda i: (0, 0, part), pipeline_mode=pl.Buffered(1)),
        ],
        out_specs=[pl.BlockSpec((tm * H_B, DH_B), lambda i: (i, 0)), hm_spec],
        out_shape=[
            jax.ShapeDtypeStruct((r * H_B, DH_B), F32),
            jax.ShapeDtypeStruct(hm_shape, BF16),
        ],
        compiler_params=_params("parallel"),
        name="kv_project",
    )(x, g, shift, scale, w_kv_bf16)
    return rows_out.reshape(b, t, H_B, DH_B), hm


def _rope_tables(pos):
    half = DK_A // 2
    inv = 1.0 / (ROPE_BASE ** jnp.linspace(0.0, 1.0, half, dtype=F32))
    ang = pos.astype(F32)[:, None] * inv[None, :]
    return jnp.cos(ang), jnp.sin(ang)


def _retention_tables(chunk):
    log_g = jnp.log1p(-(2.0 ** (-5.0 - jnp.arange(H_A, dtype=F32))))
    i = jnp.arange(chunk, dtype=F32)
    diff = i[:, None] - i[None, :]
    decay = jnp.where(diff >= 0, jnp.exp(jnp.maximum(diff, 0.0)[None] * log_g[:, None, None]), 0.0)
    cross_scale = jnp.exp((i + 1.0)[:, None] * log_g[None, :]).T[:, :, None]
    wk = jnp.exp((chunk - 1.0 - i)[:, None] * log_g[None, :]).T[:, :, None]
    state_decay = jnp.exp(chunk * log_g)
    return decay, cross_scale, wk, state_decay


def _retention_chunk(q, k, v, g, cos, sin, decay, cross_scale, wk, state_decay, s):
    half = DK_A // 2

    def rot(x):
        x1 = x[:, :half]
        x2 = x[:, half:]
        return jnp.concatenate([x1 * cos - x2 * sin, x1 * sin + x2 * cos], axis=-1)

    q = rot(q)
    k = rot(k) * (DK_A ** -0.5)
    vb = v.astype(BF16)
    qb = q.astype(BF16)
    kb = k.astype(BF16)
    scores = lax.dot_general(qb, kb, (((1,), (1,)), ((), ())), preferred_element_type=F32) * decay
    inner = jnp.dot(scores.astype(BF16), vb, preferred_element_type=F32)
    cross = jnp.dot(qb, s.astype(BF16), preferred_element_type=F32) * cross_scale
    kw = (k * wk).astype(BF16)
    upd = lax.dot_general(kw, vb, (((0,), (0,)), ((), ())), preferred_element_type=F32)
    s_new = state_decay * s + upd
    o = inner + cross
    o = o * lax.rsqrt(jnp.mean(o * o, axis=-1, keepdims=True) + EPS)
    return g * jax.nn.sigmoid(g) * o, s_new


RET_HEADS_PER_STEP = 4
RET_CHUNKS_PER_STEP = 1


def _retention_kernel(sdec_ref, q_ref, k_ref, v_ref, g_ref, cos_ref, sin_ref, dec_ref, cs_ref,
                      wk_ref, *refs, chunk, heads, chunks, n_steps, zero_state, layer, init_stack):
    a_ref, sout_ref, s_scr = refs[-3:]
    hg = pl.program_id(1)
    c = pl.program_id(2)

    @pl.when(c == 0)
    def _():
        s_scr[...] = jnp.zeros(s_scr.shape, F32) if zero_state else refs[0][...]

    for hh in range(heads):
        s = s_scr[hh]
        kcols = slice(hh * DK_A, (hh + 1) * DK_A)
        vcols = slice(hh * DV_A, (hh + 1) * DV_A)
        for ci in range(chunks):
            rows = slice(ci * chunk, (ci + 1) * chunk)
            a, s = _retention_chunk(
                q_ref[rows, kcols], k_ref[rows, kcols], v_ref[rows, vcols], g_ref[rows, vcols],
                cos_ref[rows, :], sin_ref[rows, :], dec_ref[hh], cs_ref[hh], wk_ref[hh],
                sdec_ref[hg * heads + hh], s)
            a_ref[rows, vcols] = a.astype(a_ref.dtype)
        s_scr[hh] = s

        @pl.when(c == n_steps - 1)
        def _():
            _store_state(sout_ref, hh, s, layer, init_stack)


def _store_state(sout_ref, idx, s, layer, init_stack):
    if not init_stack:
        sout_ref[idx] = s
        return
    for slab in range(sout_ref.shape[0]):
        sout_ref[slab, idx] = s if slab == layer else jnp.zeros(s.shape, s.dtype)


def _retention(qkvg, s_in, layer, pos, out_dtype, s_out_prev):
    b, t, _ = qkvg.shape
    chunk = math.gcd(t, RET_CHUNK)
    n_chunks = t // chunk
    heads = RET_HEADS_PER_STEP
    chunks = math.gcd(n_chunks, RET_CHUNKS_PER_STEP)
    n_steps = n_chunks // chunks
    rows = chunk * chunks
    assert H_A % heads == 0
    cos, sin = _rope_tables(pos)
    decay, cross_scale, wk, state_decay = _retention_tables(chunk)
    k_off = QK_A // (heads * DK_A)
    v_off = 2 * QK_A // (heads * DV_A)
    g_off = (2 * QK_A + V_A) // (heads * DV_A)
    init_stack = s_out_prev is None
    kernel = functools.partial(_retention_kernel, chunk=chunk, heads=heads, chunks=chunks, n_steps=n_steps,
                               zero_state=s_in is None, layer=layer, init_stack=init_stack)
    state_spec = pl.BlockSpec((None, None, heads, DK_A, DV_A), lambda bi, h, c: (layer, bi, h, 0, 0))
    state_args, state_specs, aliases = [], [], {}
    if s_in is not None:
        state_args.append(s_in)
        state_specs.append(state_spec)
    if init_stack:
        out_state_spec = pl.BlockSpec((N_A, None, heads, DK_A, DV_A), lambda bi, h, c: (0, bi, h, 0, 0))
    else:
        out_state_spec = state_spec
        aliases = {10 + len(state_args): 1}
        state_args.append(s_out_prev)
        state_specs.append(pl.BlockSpec(memory_space=pl.ANY))
    return pl.pallas_call(
        kernel,
        grid=(b, H_A // heads, n_steps),
        in_specs=[
            pl.BlockSpec(memory_space=pltpu.SMEM),
            pl.BlockSpec((None, rows, heads * DK_A), lambda bi, h, c: (bi, c, h)),
            pl.BlockSpec((None, rows, heads * DK_A), lambda bi, h, c: (bi, c, k_off + h)),
            pl.BlockSpec((None, rows, heads * DV_A), lambda bi, h, c: (bi, c, v_off + h)),
            pl.BlockSpec((None, rows, heads * DV_A), lambda bi, h, c: (bi, c, g_off + h)),
            pl.BlockSpec((rows, DK_A // 2), lambda bi, h, c: (c, 0)),
            pl.BlockSpec((rows, DK_A // 2), lambda bi, h, c: (c, 0)),
            pl.BlockSpec((heads, chunk, chunk), lambda bi, h, c: (h, 0, 0)),
            pl.BlockSpec((heads, chunk, 1), lambda bi, h, c: (h, 0, 0)),
            pl.BlockSpec((heads, chunk, 1), lambda bi, h, c: (h, 0, 0)),
        ] + state_specs,
        out_specs=[
            pl.BlockSpec((None, rows, heads * DV_A), lambda bi, h, c: (bi, c, h)),
            out_state_spec,
        ],
        out_shape=[
            jax.ShapeDtypeStruct((b, t, V_A), out_dtype),
            jax.ShapeDtypeStruct((N_A, b, H_A, DK_A, DV_A), F32),
        ],
        scratch_shapes=[pltpu.VMEM((heads, DK_A, DV_A), F32)],
        input_output_aliases=aliases,
        compiler_params=_params("parallel", "parallel", "arbitrary"),
        name="retention",
    )(state_decay, qkvg, qkvg, qkvg, qkvg, cos, sin, decay, cross_scale, wk, *state_args)


PROJ_RET_ROWS = 1024


def _proj_retention_kernel(sdec_ref, h_ref, wq_ref, wkey_ref, wv_ref, wg_ref, cos_ref, sin_ref, dec_ref,
                           cs_ref, kdecay_ref, *refs, chunk, n_steps, layer, init_stack):
    a_ref, sout_ref, qkvg_scr, s_scr = refs[-4:]
    head = pl.program_id(1)
    r = pl.program_id(2)

    @pl.when(r == 0)
    def _():
        s_scr[...] = jnp.zeros(s_scr.shape, F32)

    hb = h_ref[...]
    col = 0
    for w_ref in (wq_ref, wkey_ref, wv_ref, wg_ref):
        width = w_ref.shape[1]
        qkvg_scr[:, col:col + width] = jnp.dot(hb, w_ref[...].astype(BF16), preferred_element_type=F32)
        col += width

    s = s_scr[...]
    for ci in range(h_ref.shape[0] // chunk):
        rows = slice(ci * chunk, (ci + 1) * chunk)
        a, s = _retention_chunk(
            qkvg_scr[rows, 0:DK_A], qkvg_scr[rows, DK_A:2 * DK_A],
            qkvg_scr[rows, 2 * DK_A:2 * DK_A + DV_A], qkvg_scr[rows, 2 * DK_A + DV_A:],
            cos_ref[rows, :], sin_ref[rows, :], dec_ref[...], cs_ref[...], kdecay_ref[...],
            sdec_ref[head], s)
        a_ref[rows, :] = a.astype(a_ref.dtype)
    s_scr[...] = s

    @pl.when(r == n_steps - 1)
    def _():
        _store_state(sout_ref, 0, s, layer, init_stack)


def _proj_retention(h_act, w_in, layer, b, t, pos, out_dtype, s_out_prev):
    chunk = math.gcd(t, RET_CHUNK)
    rows = min(PROJ_RET_ROWS, t)
    assert t % rows == 0 and rows % chunk == 0
    n_steps = t // rows
    cos, sin = _rope_tables(pos)
    decay, cross_scale, wk, state_decay = _retention_tables(chunk)
    k_off = QK_A // DK_A
    v_off = 2 * QK_A // DV_A
    g_off = (2 * QK_A + V_A) // DV_A
    d = h_act.shape[1]
    init_stack = s_out_prev is None
    kernel = functools.partial(_proj_retention_kernel, chunk=chunk, n_steps=n_steps, layer=layer,
                               init_stack=init_stack)
    extra_args, extra_specs, aliases = [], [], {}
    if init_stack:
        state_spec = pl.BlockSpec((N_A, None, 1, DK_A, DV_A), lambda bi, h, r: (0, bi, h, 0, 0))
    else:
        state_spec = pl.BlockSpec((None, None, 1, DK_A, DV_A), lambda bi, h, r: (layer, bi, h, 0, 0))
        aliases = {11: 1}
        extra_args.append(s_out_prev)
        extra_specs.append(pl.BlockSpec(memory_space=pl.ANY))
    return pl.pallas_call(
        kernel,
        grid=(b, H_A, n_steps),
        in_specs=[
            pl.BlockSpec(memory_space=pltpu.SMEM),
            pl.BlockSpec((rows, d), lambda bi, h, r: (bi * n_steps + r, 0)),
            pl.BlockSpec((None, d, DK_A), lambda bi, h, r: (layer, 0, h)),
            pl.BlockSpec((None, d, DK_A), lambda bi, h, r: (layer, 0, k_off + h)),
            pl.BlockSpec((None, d, DV_A), lambda bi, h, r: (layer, 0, v_off + h)),
            pl.BlockSpec((None, d, DV_A), lambda bi, h, r: (layer, 0, g_off + h)),
            pl.BlockSpec((rows, DK_A // 2), lambda bi, h, r: (r, 0)),
            pl.BlockSpec((rows, DK_A // 2), lambda bi, h, r: (r, 0)),
            pl.BlockSpec((None, chunk, chunk), lambda bi, h, r: (h, 0, 0)),
            pl.BlockSpec((None, chunk, 1), lambda bi, h, r: (h, 0, 0)),
            pl.BlockSpec((None, chunk, 1), lambda bi, h, r: (h, 0, 0)),
        ] + extra_specs,
        out_specs=[
            pl.BlockSpec((None, rows, DV_A), lambda bi, h, r: (bi, r, h)),
            state_spec,
        ],
        out_shape=[
            jax.ShapeDtypeStruct((b, t, V_A), out_dtype),
            jax.ShapeDtypeStruct((N_A, b, H_A, DK_A, DV_A), F32),
        ],
        scratch_shapes=[
            pltpu.VMEM((rows, 2 * DK_A + 2 * DV_A), F32),
            pltpu.VMEM((DK_A, DV_A), F32),
        ],
        input_output_aliases=aliases,
        compiler_params=_params("parallel", "parallel", "arbitrary"),
        name="proj_retention",
    )(state_decay, h_act, w_in, w_in, w_in, w_in, cos, sin, decay, cross_scale, wk, *extra_args)


def _select_topk(gate, n_sel, axis=1):
    pos = lax.broadcasted_iota(jnp.int32, gate.shape, axis).astype(F32)
    picks = []
    for _ in range(n_sel):
        m = jnp.max(gate, axis=axis, keepdims=True)
        idx = jnp.min(jnp.where(gate == m, pos, float(gate.shape[axis])), axis=axis, keepdims=True)
        valid = m > -jnp.inf
        picks.append((idx, valid))
        gate = jnp.where((pos == idx) & valid, -jnp.inf, gate)
    return picks


MOBA_HEADS_PER_STEP = 2


def _moba_prompt_kernel(h_ref, wq_ref, wg_ref, k_ref, vt_ref, a_ref, *, n_blocks, heads):
    hb = h_ref[...]
    q_all = jnp.dot(hb, wq_ref[...].astype(BF16), preferred_element_type=F32)
    g_all = jnp.dot(hb, wg_ref[...].astype(BF16), preferred_element_type=F32)
    for hh in range(heads):
        cols = slice(hh * DH_B, (hh + 1) * DH_B)
        _moba_prompt_head(q_all, g_all, k_ref.at[hh], vt_ref.at[hh], a_ref, cols, n_blocks)


def _moba_prompt_head(q_ref, g_ref, k_ref, vt_ref, a_ref, cols, n_blocks):
    bs = MOBA_BLOCK
    scale = DH_B ** -0.5
    n_sel = min(MOBA_TOPK, n_blocks)
    kb = k_ref[...]
    ones_rows = 16
    vt = jnp.concatenate([vt_ref[...], jnp.ones((ones_rows, kb.shape[0]), BF16)], axis=0)
    means_b = jnp.mean(kb.astype(F32).reshape(n_blocks, bs, DH_B), axis=1).astype(BF16)
    key = lax.broadcasted_iota(jnp.int32, (bs, bs), 0)
    qry = lax.broadcasted_iota(jnp.int32, (bs, bs), 1)
    causal = key <= qry

    for i in range(n_blocks):
        rows = slice(i * bs, (i + 1) * bs)
        q = q_ref[rows, cols]
        qs = (q * (scale * LOG2_E)).astype(BF16)
        picks = None
        if i > n_sel:
            gate = lax.dot_general(means_b, q.astype(BF16), (((1,), (1,)), ((), ())),
                                   preferred_element_type=F32)
            blk = lax.broadcasted_iota(jnp.int32, gate.shape, 0)
            picks = _select_topk(jnp.where(blk < i, gate, -jnp.inf), n_sel, axis=0)
        blocks = []
        for j in range(i + 1):
            sj = lax.dot_general(kb[j * bs:(j + 1) * bs, :], qs, (((1,), (1,)), ((), ())),
                                 preferred_element_type=F32)
            if j == i:
                sj = jnp.where(causal, sj, -jnp.inf)
            elif picks is not None:
                chosen = picks[0][0] == float(j)
                for idx, _ in picks[1:]:
                    chosen = chosen | (idx == float(j))
                sj = jnp.where(chosen, sj, -jnp.inf)
            blocks.append(sj)
        m = jnp.max(functools.reduce(jnp.maximum, blocks), axis=0, keepdims=True)
        acc = jnp.zeros((DH_B + ones_rows, bs), F32)
        for j, sj in enumerate(blocks):
            p = jnp.exp2(sj - m).astype(BF16)
            acc = acc + jnp.dot(vt[:, j * bs:(j + 1) * bs], p, preferred_element_type=F32)
        o = (acc[:DH_B, :] / acc[DH_B:DH_B + 1, :]).T
        g = g_ref[rows, cols]
        a_ref[rows, cols] = (g * jax.nn.sigmoid(g) * o).astype(a_ref.dtype)


def _moba_prompt(h_act, w_q, layer, b, t, k_hm, vt_hm, out_dtype):
    d = h_act.shape[1]
    n_blocks = t // MOBA_BLOCK
    heads = MOBA_HEADS_PER_STEP
    assert n_blocks * MOBA_BLOCK == t and n_blocks <= 128 and H_B % heads == 0
    g_off = H_B // heads
    kernel = functools.partial(_moba_prompt_kernel, n_blocks=n_blocks, heads=heads)
    return pl.pallas_call(
        kernel,
        grid=(b, H_B // heads),
        in_specs=[
            pl.BlockSpec((t, d), lambda bi, h: (bi, 0)),
            pl.BlockSpec((None, d, heads * DH_B), lambda bi, h: (layer, 0, h)),
            pl.BlockSpec((None, d, heads * DH_B), lambda bi, h: (layer, 0, g_off + h)),
            pl.BlockSpec((None, heads, t, DH_B), lambda bi, h: (bi, h, 0, 0)),
            pl.BlockSpec((None, heads, DH_B, t), lambda bi, h: (bi, h, 0, 0)),
        ],
        out_specs=pl.BlockSpec((None, t, heads * DH_B), lambda bi, h: (bi, 0, h)),
        out_shape=jax.ShapeDtypeStruct((b, t, D_MODEL), out_dtype),
        compiler_params=_params("parallel", "arbitrary"),
        name="moba_prompt",
    )(h_act, w_q, w_q, k_hm, vt_hm)


HEAD_GROUP = 8


GATHER_PAGES_PER_STEP = 8


def _gather_cache_kernel(pt_ref, *refs, page, n_groups, pages_per_step, pages_per_block):
    n_slabs = pages_per_step * n_groups
    o_ref, m_ref = refs[2 * n_slabs], refs[2 * n_slabs + 1]
    for part in range(2):
        lanes = slice(part * DH_B, (part + 1) * DH_B)
        for pg in range(pages_per_step):
            for grp in range(n_groups):
                src = refs[part * n_slabs + pg * n_groups + grp]
                src = src.reshape(page * HEAD_GROUP, DH_B)
                for hg in range(HEAD_GROUP):
                    rows = src[pl.ds(hg, page, stride=HEAD_GROUP), :]
                    o_ref[grp * HEAD_GROUP + hg, pg * page:(pg + 1) * page, lanes] = rows.astype(BF16)
    for blk in range(pages_per_step // pages_per_block):
        for grp in range(n_groups):
            total = None
            for pg in range(blk * pages_per_block, (blk + 1) * pages_per_block):
                part_sum = jnp.sum(refs[pg * n_groups + grp][...], axis=0)
                total = part_sum if total is None else total + part_sum
            m_ref[blk, grp * HEAD_GROUP:(grp + 1) * HEAD_GROUP, :] = total / MOBA_BLOCK


def _gather_cache(cache_k, cache_v, page_table_flat, n_seq, n_pages):
    n_pool, page, n_h, dh = cache_k.shape
    pages_per_block = MOBA_BLOCK // page
    pps = GATHER_PAGES_PER_STEP
    assert n_h == H_B and dh == DH_B and n_h % HEAD_GROUP == 0
    assert pps % pages_per_block == 0 and n_pages % pps == 0
    n_blocks = n_pages // pages_per_block
    n_groups = n_h // HEAD_GROUP
    grouped = [c.reshape(n_pool, page, n_groups, HEAD_GROUP, dh) for c in (cache_k, cache_v)]

    def page_map(pg, grp):
        return lambda bi, j, pt: (pt[bi * n_pages + pps * j + pg], 0, grp, 0, 0)

    def slab_specs():
        return [pl.BlockSpec((None, page, None, HEAD_GROUP, dh), page_map(pg, grp))
                for pg in range(pps) for grp in range(n_groups)]

    n_slabs = pps * n_groups
    blocks_per_step = pps // pages_per_block
    kernel = functools.partial(_gather_cache_kernel, page=page, n_groups=n_groups, pages_per_step=pps,
                               pages_per_block=pages_per_block)
    return pl.pallas_call(
        kernel,
        grid_spec=pltpu.PrefetchScalarGridSpec(
            num_scalar_prefetch=1,
            grid=(n_seq, n_pages // pps),
            in_specs=slab_specs() + slab_specs(),
            out_specs=[
                pl.BlockSpec((None, n_h, pps * page, 2 * dh), lambda bi, j, pt: (bi, 0, j, 0)),
                pl.BlockSpec((None, blocks_per_step, n_h, dh), lambda bi, j, pt: (bi, j, 0, 0)),
            ],
        ),
        out_shape=[
            jax.ShapeDtypeStruct((n_seq, n_h, n_pages * page, 2 * dh), BF16),
            jax.ShapeDtypeStruct((n_seq, n_blocks, n_h, dh), F32),
        ],
        compiler_params=_params("parallel", "arbitrary"),
        name="gather_cache",
    )(page_table_flat, *([grouped[0]] * n_slabs), *([grouped[1]] * n_slabs))


def _moba_select_kernel(q_ref, means_ref, o_ref, *, t, n_blocks, past):
    q = q_ref[...]
    rows = t * H_B
    qrep = jnp.concatenate([jnp.broadcast_to(q[ti:ti + 1, :], (H_B, D_MODEL)) for ti in range(t)], axis=0)
    lane_h = lax.broadcasted_iota(jnp.int32, (rows, D_MODEL), 1) >> (DH_B.bit_length() - 1)
    row_h = lax.broadcasted_iota(jnp.int32, (rows, D_MODEL), 0) & (H_B - 1)
    qexp = jnp.where(lane_h == row_h, qrep, 0.0)
    means = jnp.concatenate([means_ref[...], jnp.zeros((128 - n_blocks, D_MODEL), F32)], axis=0)
    gate = lax.dot_general(qexp.astype(BF16), means.astype(BF16), (((1,), (1,)), ((), ())),
                           preferred_element_type=F32)
    lane = lax.broadcasted_iota(jnp.int32, gate.shape, 1)
    token = lax.broadcasted_iota(jnp.int32, gate.shape, 0) >> (H_B.bit_length() - 1)
    own = (past + token) >> (MOBA_BLOCK.bit_length() - 1)
    gate = jnp.where(lane < own, gate, -jnp.inf)
    out = jnp.zeros(gate.shape, F32)
    for r, (idx, _) in enumerate(_select_topk(gate, MOBA_TOPK)):
        out = jnp.where(lane == r, idx, out)
    o_ref[...] = out.astype(jnp.int32)


def _moba_select(qg, means, past):
    b, t, _ = qg.shape
    n_blocks = means.shape[1]
    assert MOBA_TOPK <= n_blocks <= 128
    kernel = functools.partial(_moba_select_kernel, t=t, n_blocks=n_blocks, past=past)
    return pl.pallas_call(
        kernel,
        grid=(b,),
        in_specs=[
            pl.BlockSpec((None, t, D_MODEL), lambda bi: (bi, 0, 0)),
            pl.BlockSpec((None, n_blocks, D_MODEL), lambda bi: (bi, 0, 0)),
        ],
        out_specs=pl.BlockSpec((None, t * H_B, 128), lambda bi: (bi, 0, 0)),
        out_shape=jax.ShapeDtypeStruct((b, t * H_B, 128), jnp.int32),
        compiler_params=_params("parallel"),
        name="moba_select",
    )(qg, means)


SAMPLE_HEADS_PER_STEP = 8


def _moba_sample_kernel(sel_ref, new_ref, *refs, t, n_slots, heads):
    a_ref = refs[heads * n_slots]
    for hh in range(heads):
        cols = slice(hh * DH_B, (hh + 1) * DH_B)
        _moba_sample_head(new_ref.at[hh], refs[hh * n_slots:(hh + 1) * n_slots], a_ref, cols, t, n_slots)


def _moba_sample_head(new_ref, kv_refs, a_ref, cols, t, n_slots):
    keys_per_token = (n_slots // t) * kv_refs[0].shape[0]

    q = new_ref[:, 0:DH_B]
    qb = q.astype(BF16)
    kall = jnp.concatenate([r[:, :DH_B] for r in kv_refs], axis=0)
    vall = jnp.concatenate([r[:, DH_B:] for r in kv_refs], axis=0)
    s = lax.dot_general(qb, kall, (((1,), (1,)), ((), ())), preferred_element_type=F32) * (DH_B ** -0.5)
    row = lax.broadcasted_iota(jnp.int32, s.shape, 0)
    col = lax.broadcasted_iota(jnp.int32, s.shape, 1)
    mine = (col >= row * keys_per_token) & (col < (row + 1) * keys_per_token)
    s = jnp.where(mine, s, -jnp.inf)

    kn = new_ref[:, 2 * DH_B:3 * DH_B]
    vn = new_ref[:, 3 * DH_B:4 * DH_B]
    rown = lax.broadcasted_iota(jnp.int32, (t, 1), 0)
    s_own = []
    for tj in range(t):
        sj = jnp.sum(q * kn[tj:tj + 1, :], axis=-1, keepdims=True) * (DH_B ** -0.5)
        s_own.append(jnp.where(rown >= tj, sj, -jnp.inf))
    m = jnp.max(s, axis=-1, keepdims=True)
    for sj in s_own:
        m = jnp.maximum(m, sj)
    p = jnp.exp(s - m)
    l = jnp.sum(p, axis=-1, keepdims=True)
    acc = jnp.dot(p.astype(BF16), vall, preferred_element_type=F32)
    for tj, sj in enumerate(s_own):
        pj = jnp.exp(sj - m)
        l = l + pj
        acc = acc + pj * vn[tj:tj + 1, :]
    o = acc / l
    g = new_ref[:, DH_B:2 * DH_B]
    a_ref[:, cols] = (g * jax.nn.sigmoid(g) * o).astype(a_ref.dtype)


def _moba_sample(qg, kv_new, past_kv, sel_flat):
    b, t, _ = qg.shape
    n_slots = t * MOBA_TOPK
    heads = SAMPLE_HEADS_PER_STEP
    assert H_B % heads == 0
    per_head = [a.reshape(b, t, H_B, DH_B) for a in
                (qg[:, :, :D_MODEL], qg[:, :, D_MODEL:], kv_new[:, :, :D_MODEL], kv_new[:, :, D_MODEL:])]
    new_rows = jnp.concatenate(per_head, axis=-1).transpose(0, 2, 1, 3)

    def slot_map(hh, ti, r):
        def index_map(bi, hg, sel):
            h = hg * heads + hh
            return (bi, h, sel[((bi * t + ti) * H_B + h) * MOBA_TOPK + r], 0)
        return index_map

    slot_specs = [pl.BlockSpec((None, None, MOBA_BLOCK, 2 * DH_B), slot_map(hh, ti, r))
                  for hh in range(heads) for ti in range(t) for r in range(MOBA_TOPK)]

    kernel = functools.partial(_moba_sample_kernel, t=t, n_slots=n_slots, heads=heads)
    return pl.pallas_call(
        kernel,
        grid_spec=pltpu.PrefetchScalarGridSpec(
            num_scalar_prefetch=1,
            grid=(b, H_B // heads),
            in_specs=[pl.BlockSpec((None, heads, t, 4 * DH_B), lambda bi, hg, sel: (bi, hg, 0, 0))] + slot_specs,
            out_specs=pl.BlockSpec((None, t, heads * DH_B), lambda bi, hg, sel: (bi, 0, hg)),
        ),
        out_shape=jax.ShapeDtypeStruct((b, t, D_MODEL), F32),
        compiler_params=_params("parallel", "arbitrary"),
        name="moba_sample",
    )(sel_flat, new_rows, *([past_kv] * len(slot_specs)))


PROMPT_NORM_ROWS = 512
PROMPT_RES_ROWS = 512
SAMPLE_COLS = 1024


def _split_mod(v, n_parts, d, expand):
    return tuple(expand(v[:, i * d:(i + 1) * d]) for i in range(n_parts))


def _run_prompt(x, mods, mods_kv, mods_f, weights):
    (norm_g, w_in_a, w_out_a, w_q_b, w_o_b, kv_norm_g, w_kv, final_g) = weights
    b, t, d = x.shape
    rows = b * t
    pos = jnp.arange(t, dtype=jnp.int32)
    per_batch = lambda v: v[:, None, :]
    x2 = x.reshape(rows, d)
    shift, scale, _ = _split_mod(mods[0], 3, d, per_batch)
    h = _norm_mod(x2, norm_g[0][None, :], shift, scale, t, BF16, PROMPT_NORM_ROWS)
    new_s = k_new = v_new = k_hm = vt_hm = None
    for l in range(DEPTH):
        gate = _split_mod(mods[l], 3, d, per_batch)[2]
        if l < N_A:
            a, new_s = _proj_retention(h, w_in_a, l, b, t, pos, BF16, new_s)
            a, w_res, w_layer = a.reshape(rows, V_A), w_out_a, l
        else:
            a = _moba_prompt(h, w_q_b, l - N_A, b, t, k_hm, vt_hm, BF16)
            a, w_res, w_layer = a.reshape(rows, d), w_o_b, l - N_A
        last = l == DEPTH - 1
        if last:
            g_next = final_g
            sh_next, sc_next = _split_mod(mods_f, 2, d, per_batch)
        else:
            g_next = norm_g[l + 1]
            sh_next, sc_next, _ = _split_mod(mods[l + 1], 3, d, per_batch)
        x2, h = _matmul_res_norm(a, w_res, w_layer, x2, gate, g_next[None, :], sh_next, sc_next, t,
                                 PROMPT_RES_ROWS, F32 if last else BF16, keep_x=not last)
        if l == N_A - 1:
            kv_shift, kv_scale = _split_mod(mods_kv, 2, d, per_batch)
            k_new, k_hm = _kv_project(x2, kv_norm_g[None, :], kv_shift, kv_scale, w_kv, 0, b, t, False)
            v_new, vt_hm = _kv_project(x2, kv_norm_g[None, :], kv_shift, kv_scale, w_kv, 1, b, t, True)
    return h.reshape(b, t, d), new_s, k_new, v_new


def _run_sample(x, mods, mods_kv, mods_f, s_in, past, weights, past_kv, means):
    (norm_g, w_in_a, w_out_a, w_q_b, w_o_b, kv_norm_g, w_kv, final_g) = weights
    b, t, d = x.shape
    rows = b * t
    pos = past + jnp.arange(t, dtype=jnp.int32)
    per_row = lambda v: jnp.repeat(v, t, axis=0)

    def project(x2, g, shift, scale, w, layer):
        h = _norm_mod(x2, g[None, :], shift, scale, 0, F32, rows)
        return _matmul(h, w, layer, rows, SAMPLE_COLS)

    x2 = x.reshape(rows, d)
    new_s = kv = None
    for l in range(DEPTH):
        shift, scale, gate = _split_mod(mods[l], 3, d, per_row)
        if l < N_A:
            qkvg = project(x2, norm_g[l], shift, scale, w_in_a, l).reshape(b, t, -1)
            a, new_s = _retention(qkvg, s_in, l, pos, F32, new_s)
            x2 = _matmul(a.reshape(rows, V_A), w_out_a, l, rows, SAMPLE_COLS, res=x2, gate=gate)
        else:
            lb = l - N_A
            qg = project(x2, norm_g[l], shift, scale, w_q_b, lb).reshape(b, t, -1)
            sel = _moba_select(qg, means, past)[:, :, :MOBA_TOPK].reshape(-1)
            a = _moba_sample(qg, kv.reshape(b, t, -1), past_kv, sel)
            x2 = _matmul(a.reshape(rows, d), w_o_b, lb, rows, SAMPLE_COLS, res=x2, gate=gate)
        if l == N_A - 1:
            kv_shift, kv_scale = _split_mod(mods_kv, 2, d, per_row)
            kv = project(x2, kv_norm_g, kv_shift, kv_scale, w_kv, 0)
    f_shift, f_scale = _split_mod(mods_f, 2, d, per_row)
    y = _norm_mod(x2, final_g[None, :], f_shift, f_scale, 0, F32, rows)
    k_new = kv[:, :d].reshape(b, t, H_B, DH_B)
    v_new = kv[:, d:].reshape(b, t, H_B, DH_B)
    return y.reshape(b, t, d), new_s, k_new, v_new


def kernel(x_prompt, x_sample, state_ret, cache_k, cache_v, page_table, c_prompt, c_sample,
           norm_g, w_mod, b_mod, w_in_a, w_out_a, w_q_b, w_o_b,
           kv_norm_g, w_mod_kv, b_mod_kv, w_kv, final_g, w_mod_f, b_mod_f):
    bp = x_prompt.shape[0]
    bd, n_pages = page_table.shape
    n_pool, page, _, _ = cache_k.shape
    past_len = n_pages * page
    assert bp + bd <= MOD_ROWS

    c_all = jnp.concatenate([c_prompt, c_sample, jnp.zeros((MOD_ROWS - bp - bd, D_MODEL), F32)], axis=0)
    mods = _mod_matmul(c_all, w_mod, b_mod[:, None, :])
    mods_kv = _mod_matmul(c_all, w_mod_kv[None], b_mod_kv[None, None, :])[0]
    mods_f = _mod_matmul(c_all, w_mod_f[None], b_mod_f[None, None, :])[0]

    weights = (norm_g, w_in_a, _cast_bf16(w_out_a), w_q_b, _cast_bf16(w_o_b), kv_norm_g,
               _cast_bf16(w_kv[None]), final_g)

    y_p, s_p, k_p, v_p = _run_prompt(x_prompt, mods[:, :bp], mods_kv[:bp], mods_f[:bp], weights)

    pt_flat = page_table.reshape(-1)
    past_kv, means = _gather_cache(cache_k, cache_v, pt_flat, bd, n_pages)
    means = means.reshape(bd, -1, H_B * DH_B)
    y_s, s_s, k_s, v_s = _run_sample(
        x_sample, mods[:, bp:bp + bd], mods_kv[bp:bp + bd], mods_f[bp:bp + bd], state_ret, past_len,
        weights, past_kv, means)
    return (y_p, y_s, s_p, s_s, k_p, v_p, k_s, v_s)
```
